```python
import math
import jax, jax.numpy as jnp
from jax import lax
import numpy as np

D_MODEL = 2048
BATCH = 4
SEQ = 8192
DEPTH = 4
DEC_BATCH = 8
DEC_SEQ = 32
PAST_LEN = 1024

CHUNK = 64
D_MIX = D_MODEL
D_POOL = D_MIX // 4
POOL_WINDOWS = (2, 4, 8, 16)
N_POOL_GROUPS = len(POOL_WINDOWS)
POOL_GROUP = D_POOL // N_POOL_GROUPS
POOL_HIST = max(POOL_WINDOWS) - 1
HEAD_DIM = 64
D_ATTN = D_MIX // 2
N_Q_HEADS = D_ATTN // HEAD_DIM
N_KV_HEADS = 2
GQA_GROUP = N_Q_HEADS // N_KV_HEADS
WINDOW = 128
WIN_CHUNKS = WINDOW // CHUNK
BAND = (WIN_CHUNKS + 1) * CHUNK
NUM_BUCKETS = 32
MAX_DISTANCE = 128
NEG = -1e30
D_RWKV = D_MIX - D_POOL - D_ATTN
RWKV_HEAD = 64
N_RWKV_HEADS = D_RWKV // RWKV_HEAD
DECAY_LORA = 64
ICLR_LORA = 64
D_SHIFT = 3 * D_RWKV + DECAY_LORA + ICLR_LORA
NORM_EPS = 1e-5
LNX_EPS = 1e-5 * RWKV_HEAD
SPLITS = (D_POOL, D_POOL, N_Q_HEADS * HEAD_DIM, N_KV_HEADS * HEAD_DIM, N_KV_HEADS * HEAD_DIM, D_ATTN, D_SHIFT, D_RWKV)
D_IN = sum(SPLITS)
SPLIT_IDX = [int(s) for s in np.cumsum(SPLITS)[:-1]]
RWKV_SPLIT_IDX = [D_RWKV, 2 * D_RWKV, 3 * D_RWKV, 3 * D_RWKV + DECAY_LORA]

kernel_name = "hybrid_pool_swa_rwkv7_stream_step"


def rms_norm(x, w):
    xf = x.astype(jnp.float32)
    y = xf * lax.rsqrt(jnp.mean(xf * xf, axis=-1, keepdims=True) + NORM_EPS)
    return (y * w.astype(jnp.float32)).astype(x.dtype)


def t5_bucket(rel):
    nb = NUM_BUCKETS // 2
    max_exact = nb // 2
    ret = jnp.where(rel > 0, nb, 0)
    n = jnp.abs(rel)
    nf = jnp.maximum(n, 1).astype(jnp.float32)
    large = max_exact + (jnp.log(nf / max_exact) / math.log(MAX_DISTANCE / max_exact) * (nb - max_exact)).astype(jnp.int32)
    large = jnp.minimum(large, nb - 1)
    return ret + jnp.where(n < max_exact, n, large)


def rel_bias(rel, table):
    b = table.astype(jnp.float32)[t5_bucket(rel)]
    return jnp.transpose(b, (2, 0, 1)).reshape(N_KV_HEADS, GQA_GROUP, rel.shape[0], rel.shape[1])


def sink_softmax(logits, sinks):
    s = sinks[..., None, None]
    m = jnp.maximum(jnp.max(logits, axis=-1, keepdims=True), s)
    e = jnp.exp(logits - m)
    return e / (jnp.sum(e, axis=-1, keepdims=True) + jnp.exp(s - m))


def swa_prompt(q, k, v, sinks, table):
    B, T = q.shape[:2]
    nC = T // CHUNK
    qc = q.reshape(B, nC, CHUNK, N_KV_HEADS, GQA_GROUP, HEAD_DIM)

    def band(x):
        xc = x.reshape(B, nC, CHUNK, N_KV_HEADS, HEAD_DIM)
        xp = jnp.pad(xc, ((0, 0), (WIN_CHUNKS, 0), (0, 0), (0, 0), (0, 0)))
        return jnp.concatenate([xp[:, j:j + nC] for j in range(WIN_CHUNKS + 1)], axis=2)

    kb, vb = band(k), band(v)
    logits = jnp.einsum('bcqkgd,bcskd->bckgqs', qc, kb).astype(jnp.float32) * (HEAD_DIM ** -0.5)
    i = jnp.arange(CHUNK)
    j = jnp.arange(BAND)
    rel = j[None, :] - WIN_CHUNKS * CHUNK - i[:, None]
    logits = logits + rel_bias(rel, table)
    key_chunk = jnp.arange(nC)[:, None] - WIN_CHUNKS + j[None, :] // CHUNK
    valid = key_chunk >= 0
    logits = jnp.where(valid[None, :, None, None, None, :], logits, NEG)
    p = sink_softmax(logits, sinks).astype(v.dtype)
    o = jnp.einsum('bckgqs,bcskd->bcqkgd', p, vb)
    return o.reshape(B, T, D_ATTN)


def swa_sample(q, k, v, k_cache, v_cache, sinks, table):
    Bd, Ts = q.shape[:2]
    kf = jnp.concatenate([k_cache.astype(k.dtype), k], axis=1)
    vf = jnp.concatenate([v_cache.astype(v.dtype), v], axis=1)
    qg = q.reshape(Bd, Ts, N_KV_HEADS, GQA_GROUP, HEAD_DIM)
    logits = jnp.einsum('bqkgd,bskd->bkgqs', qg, kf).astype(jnp.float32) * (HEAD_DIM ** -0.5)
    q_pos = PAST_LEN + jnp.arange(Ts)
    k_pos = PAST_LEN - WINDOW + jnp.arange(WINDOW + Ts)
    rel = k_pos[None, :] - q_pos[:, None]
    logits = logits + rel_bias(rel, table)
    qcn = q_pos // CHUNK
    kcn = k_pos // CHUNK
    valid = (kcn[None, :] <= qcn[:, None]) & (kcn[None, :] >= qcn[:, None] - WIN_CHUNKS)
    logits = jnp.where(valid, logits, NEG)
    p = sink_softmax(logits, sinks).astype(vf.dtype)
    o = jnp.einsum('bkgqs,bskd->bqkgd', p, vf).reshape(Bd, Ts, D_ATTN)
    return o, kf[:, -WINDOW:], vf[:, -WINDOW:]


def pool_mixer(p_hist, p, pos0, pool_w, pool_scale):
    B, T = p.shape[:2]
    ext = jnp.concatenate([p_hist.astype(p.dtype), p], axis=1).astype(jnp.float32)
    S = jnp.concatenate([jnp.zeros_like(ext[:, :1]), jnp.cumsum(ext, axis=1)], axis=1)
    end = S[:, POOL_HIST + 1:]
    pos = pos0 + jnp.arange(T)
    outs = []
    for g, w in enumerate(POOL_WINDOWS):
        sl = slice(g * POOL_GROUP, (g + 1) * POOL_GROUP)
        start = S[:, POOL_HIST + 1 - w:POOL_HIST + 1 - w + T, sl]
        cnt = jnp.minimum(pos + 1, w).astype(jnp.float32)[:, None]
        outs.append((end[..., sl] - start) / cnt - ext[:, POOL_HIST:, sl])
    d = jnp.stack(outs, axis=2).astype(p.dtype)
    y = jnp.einsum('btgc,gcd->btgd', d, pool_w).reshape(B, T, D_POOL) * pool_scale
    return y, ext[:, -POOL_HIST:].astype(p.dtype)


def wkv_scan(S0, r, w, k, v, a_vec, b_vec):
    def step(S, inp):
        r_t, w_t, k_t, v_t, a_t, b_t = inp
        sa = jnp.einsum('bhij,bhj->bhi', S, a_t)
        S = S * w_t[:, :, None, :] + sa[..., None] * b_t[:, :, None, :] + v_t[..., None] * k_t[:, :, None, :]
        o = jnp.einsum('bhij,bhj->bhi', S, r_t)
        return S, o
    xs = tuple(jnp.moveaxis(t, 1, 0) for t in (r, w, k, v, a_vec, b_vec))
    S, o = lax.scan(step, S0, xs)
    return S, jnp.moveaxis(o, 0, 1)


def rwkv_mixer(xc, shift_prev, S0, mu, w0, w_up, a0, a_up, k_k, k_a, r_k, lnx_w, lnx_b):
    B, T = xc.shape[:2]
    prev = jnp.concatenate([shift_prev[:, None].astype(xc.dtype), xc[:, :-1]], axis=1)
    xs = xc + (prev - xc) * mu
    r, k, v, wl, al = jnp.split(xs.astype(jnp.float32), RWKV_SPLIT_IDX, axis=-1)
    w_raw = -jax.nn.softplus(-(w0 + jnp.tanh(wl) @ w_up)) - 0.5
    decay = jnp.exp(-jnp.exp(w_raw))
    a = jax.nn.sigmoid(a0 + al @ a_up)
    heads = lambda t: t.reshape(B, T, N_RWKV_HEADS, RWKV_HEAD)
    kk = heads(k * k_k)
    kk = kk / jnp.maximum(jnp.linalg.norm(kk, axis=-1, keepdims=True), 1e-12)
    k = k * (1.0 + (a - 1.0) * k_a)
    rh, kh, vh, wh, ah = heads(r), heads(k), heads(v), heads(decay), heads(a)
    S, o = wkv_scan(S0.astype(jnp.float32), rh, wh, kh, vh, -kk, kk * ah)
    mean = jnp.mean(o, axis=-1, keepdims=True)
    var = jnp.mean(jnp.square(o - mean), axis=-1, keepdims=True)
    o = ((o - mean) * lax.rsqrt(var + LNX_EPS)).reshape(B, T, D_RWKV) * lnx_w + lnx_b
    bonus = jnp.sum(rh * kh * r_k, axis=-1, keepdims=True) * vh
    o = o + bonus.reshape(B, T, D_RWKV)
    return o.astype(xc.dtype), xc[:, -1], S.astype(S0.dtype)


def mixer_layer(h, lp, table, pool_hist, shift_prev, wkv0, kv_cache, pos0):
    B, T = h.shape[:2]
    xn = rms_norm(h, lp['norm_w'])
    u = xn @ lp['w_in']
    p, g_pool, q, k, v, g_attn, xc, g_rwkv = jnp.split(u, SPLIT_IDX, axis=-1)
    y_pool, new_pool = pool_mixer(pool_hist, p, pos0, lp['pool_w'], lp['pool_scale'])
    q = q.reshape(B, T, N_Q_HEADS, HEAD_DIM)
    k = k.reshape(B, T, N_KV_HEADS, HEAD_DIM)
    v = v.reshape(B, T, N_KV_HEADS, HEAD_DIM)
    if kv_cache is None:
        y_attn = swa_prompt(q, k, v, lp['sinks'], table)
        new_k, new_v = k[:, -WINDOW:], v[:, -WINDOW:]
    else:
        y_attn, new_k, new_v = swa_sample(q, k, v, kv_cache[0], kv_cache[1], lp['sinks'], table)
    y_rwkv, new_shift, new_wkv = rwkv_mixer(xc, shift_prev, wkv0, lp['mu'], lp['w0'], lp['w_up'], lp['a0'], lp['a_up'],
                                            lp['k_k'], lp['k_a'], lp['r_k'], lp['lnx_w'], lp['lnx_b'])
    mixed = jnp.concatenate([y_pool * jax.nn.silu(g_pool), y_attn * jax.nn.silu(g_attn), y_rwkv * jax.nn.silu(g_rwkv)], axis=-1)
    return h + mixed @ lp['w_out'], (new_pool, new_k, new_v, new_shift, new_wkv)


def setup_inputs(seed: int = 0) -> dict:
    key = jax.random.key(seed)
    ks = jax.random.split(key, 26)
    nrm = lambda kk, shape, s: jax.random.normal(kk, shape, jnp.float32) * s
    return {
        "x_prompt": nrm(ks[0], (BATCH, SEQ, D_MODEL), 1.0),
        "x_sample": nrm(ks[1], (DEC_BATCH, DEC_SEQ, D_MODEL), 1.0),
        "state_pool": nrm(ks[2], (DEPTH, DEC_BATCH, POOL_HIST, D_POOL), 1.0),
        "cache_swa_k": nrm(ks[3], (DEPTH, DEC_BATCH, WINDOW, N_KV_HEADS, HEAD_DIM), 1.0),
        "cache_swa_v": nrm(ks[4], (DEPTH, DEC_BATCH, WINDOW, N_KV_HEADS, HEAD_DIM), 1.0),
        "state_rwkv_shift": nrm(ks[5], (DEPTH, DEC_BATCH, D_SHIFT), 1.0),
        "state_rwkv_wkv": nrm(ks[6], (DEPTH, DEC_BATCH, N_RWKV_HEADS, RWKV_HEAD, RWKV_HEAD), 0.1),
        "norm_w": 1.0 + nrm(ks[7], (DEPTH, D_MODEL), 0.02),
        "w_in": nrm(ks[8], (DEPTH, D_MODEL, D_IN), D_MODEL ** -0.5),
        "w_out": nrm(ks[9], (DEPTH, D_MIX, D_MODEL), 0.5 * D_MIX ** -0.5),
        "pool_w": nrm(ks[10], (DEPTH, N_POOL_GROUPS, POOL_GROUP, POOL_GROUP), POOL_GROUP ** -0.5),
        "pool_scale": 1.0 + nrm(ks[11], (DEPTH, D_POOL), 0.02),
        "attn_sinks": nrm(ks[12], (DEPTH, N_Q_HEADS), 0.5),
        "rel_bias_table": nrm(ks[13], (NUM_BUCKETS, N_Q_HEADS), 0.5),
        "rwkv_mu": jax.random.uniform(ks[14], (DEPTH, D_SHIFT), jnp.float32),
        "rwkv_w0": jax.random.uniform(ks[15], (DEPTH, D_RWKV), jnp.float32, -4.0, -1.0),
        "rwkv_w_up": nrm(ks[16], (DEPTH, DECAY_LORA, D_RWKV), 0.5 * DECAY_LORA ** -0.5),
        "rwkv_a0": nrm(ks[17], (DEPTH, D_RWKV), 0.1),
        "rwkv_a_up": nrm(ks[18], (DEPTH, ICLR_LORA, D_RWKV), 0.5 * ICLR_LORA ** -0.5),
        "rwkv_k_k": 0.85 + nrm(ks[19], (DEPTH, D_RWKV), 0.02),
        "rwkv_k_a": 1.0 + nrm(ks[20], (DEPTH, D_RWKV), 0.02),
        "rwkv_r_k": nrm(ks[21], (DEPTH, N_RWKV_HEADS, RWKV_HEAD), 0.1),
        "rwkv_lnx_w": 1.0 + nrm(ks[22], (DEPTH, D_RWKV), 0.02),
        "rwkv_lnx_b": nrm(ks[23], (DEPTH, D_RWKV), 0.02),
        "final_norm_w": 1.0 + nrm(ks[24], (D_MODEL,), 0.02),
    }


def reference(x_prompt, x_sample, state_pool, cache_swa_k, cache_swa_v, state_rwkv_shift, state_rwkv_wkv,
              norm_w, w_in, w_out, pool_w, pool_scale, attn_sinks, rel_bias_table,
              rwkv_mu, rwkv_w0, rwkv_w_up, rwkv_a0, rwkv_a_up, rwkv_k_k, rwkv_k_a, rwkv_r_k,
              rwkv_lnx_w, rwkv_lnx_b, final_norm_w):
    hp, hs = x_prompt, x_sample
    prompt_states, sample_states = [], []
    for l in range(DEPTH):
        lp = {
            'norm_w': norm_w[l], 'w_in': w_in[l], 'w_out': w_out[l],
            'pool_w': pool_w[l], 'pool_scale': pool_scale[l],
            'sinks': attn_sinks[l].astype(jnp.float32).reshape(N_KV_HEADS, GQA_GROUP),
            'mu': rwkv_mu[l], 'w0': rwkv_w0[l].astype(jnp.float32), 'w_up': rwkv_w_up[l].astype(jnp.float32),
            'a0': rwkv_a0[l].astype(jnp.float32), 'a_up': rwkv_a_up[l].astype(jnp.float32),
            'k_k': rwkv_k_k[l].astype(jnp.float32), 'k_a': rwkv_k_a[l].astype(jnp.float32),
            'r_k': rwkv_r_k[l].astype(jnp.float32),
            'lnx_w': rwkv_lnx_w[l].astype(jnp.float32), 'lnx_b': rwkv_lnx_b[l].astype(jnp.float32),
        }
        Bp = hp.shape[0]
        pool0 = jnp.zeros((Bp, POOL_HIST, D_POOL), hp.dtype)
        shift0 = jnp.zeros((Bp, D_SHIFT), hp.dtype)
        wkv_init = jnp.zeros((Bp, N_RWKV_HEADS, RWKV_HEAD, RWKV_HEAD), state_rwkv_wkv.dtype)
        hp, sp = mixer_layer(hp, lp, rel_bias_table, pool0, shift0, wkv_init, None, 0)
        hs, ss = mixer_layer(hs, lp, rel_bias_table, state_pool[l], state_rwkv_shift[l], state_rwkv_wkv[l],
                             (cache_swa_k[l], cache_swa_v[l]), PAST_LEN)
        prompt_states.append(sp)
        sample_states.append(ss)
    new_pool_p, new_k_p, new_v_p, new_shift_p, new_wkv_p = [jnp.stack(t) for t in zip(*prompt_states)]
    new_pool_s, new_k_s, new_v_s, new_shift_s, new_wkv_s = [jnp.stack(t) for t in zip(*sample_states)]
    y_prompt = rms_norm(hp, final_norm_w)
    y_sample = rms_norm(hs, final_norm_w)
    return (y_prompt, y_sample, new_pool_p, new_k_p, new_v_p, new_shift_p, new_wkv_p,
            new_pool_s, new_k_s, new_v_s, new_shift_s, new_wkv_s)
```

```python
import functools
import math

import jax
import jax.numpy as jnp
from jax import lax
from jax.experimental import pallas as pl
from jax.experimental.pallas import tpu as pltpu

F32 = jnp.float32
BF16 = jnp.bfloat16

D_MODEL = 2048
DEPTH = 4
PAST_LEN = 1024
CHUNK = 64
D_POOL = 512
POOL_WINDOWS = (2, 4, 8, 16)
POOL_GROUP = 128
POOL_HIST = 15
HEAD_DIM = 64
D_ATTN = 1024
N_Q_HEADS = 16
N_KV_HEADS = 2
WINDOW = 128
WIN_CHUNKS = 2
NUM_BUCKETS = 32
MAX_DISTANCE = 128
NEG = -1e30
D_RWKV = 512
RWKV_HEAD = 64
N_RWKV_HEADS = 8
LORA = 64
D_SHIFT = 3 * D_RWKV + 2 * LORA
NORM_EPS = 1e-5
LNX_EPS = 1e-5 * RWKV_HEAD
SEGMENTS = (2 * D_POOL, D_ATTN, 2 * N_KV_HEADS * HEAD_DIM, D_ATTN, D_SHIFT, D_RWKV)
D_IN = sum(SEGMENTS)

LANES = 128
ATT_TILE = 128
RWKV_CHUNK = 128
INV_BASE = 16
VMEM_LIMIT = 56 * 1024 * 1024


def _cparams(sem):
    return pltpu.CompilerParams(dimension_semantics=sem, vmem_limit_bytes=VMEM_LIMIT)


def _sigmoid(x):
    return 1.0 / (1.0 + jnp.exp(-x))


def _silu(x):
    return x * _sigmoid(x)


def _split(x, terms):
    parts = []
    rem = x
    for _ in range(terms):
        p = rem.astype(BF16)
        parts.append(p)
        rem = rem - p.astype(F32)
    return parts


def _dot(a, b, dims=None):
    if dims is None:
        return jnp.dot(a, b, preferred_element_type=F32)
    return lax.dot_general(a, b, (dims, ((), ())), preferred_element_type=F32)


def _mm(a, b, passes=1, dims=None):
    if passes == 1:
        return _dot(a.astype(BF16), b.astype(BF16), dims)
    ah, al = _split(a, 2)
    bh, bl = _split(b, 2)
    return _dot(ah, bh, dims) + (_dot(ah, bl, dims) + _dot(al, bh, dims))


def _mm_exact_rhs(a, b_bf16, terms=3):
    out = None
    for p in _split(a, terms):
        d = _dot(p, b_bf16)
        out = d if out is None else out + d
    return out


HEAD_SHIFT = RWKV_HEAD.bit_length() - 1


def _shr(x, bits):
    return lax.shift_right_logical(x, jnp.full_like(x, bits))


NT = ((1,), (1,))
TN = ((0,), (0,))


def _inproj_kernel(h_ref, nw_ref, w_ref, *o_refs):
    x = h_ref[...]
    ms = jnp.mean(x * x, axis=-1, keepdims=True)
    xn = (x * lax.rsqrt(ms + NORM_EPS) * nw_ref[...]).astype(BF16)
    off = 0
    for o_ref in o_refs:
        width = o_ref.shape[-1]
        o_ref[...] = jnp.dot(xn, w_ref[:, off:off + width], preferred_element_type=F32)
        off += width


def _inproj(h2d, norm_w, w_in_bf16, tm):
    m = h2d.shape[0]
    return pl.pallas_call(
        _inproj_kernel,
        grid=(m // tm,),
        in_specs=[
            pl.BlockSpec((tm, D_MODEL), lambda i: (i, 0)),
            pl.BlockSpec((1, D_MODEL), lambda i: (0, 0)),
            pl.BlockSpec(memory_space=pltpu.VMEM),
        ],
        out_specs=[pl.BlockSpec((tm, s), lambda i: (i, 0)) for s in SEGMENTS],
        out_shape=[jax.ShapeDtypeStruct((m, s), F32) for s in SEGMENTS],
        compiler_params=_cparams(("arbitrary",)),
        name="inproj",
    )(h2d, norm_w.reshape(1, D_MODEL), w_in_bf16)


def _outproj_kernel(h_ref, mp_ref, ma_ref, mr_ref, w_ref, fw_ref, o_ref, *, final):
    acc = jnp.dot(mp_ref[...], w_ref[0:D_POOL, :], preferred_element_type=F32)
    acc += jnp.dot(ma_ref[...], w_ref[D_POOL:D_POOL + D_ATTN, :], preferred_element_type=F32)
    acc += jnp.dot(mr_ref[...], w_ref[D_POOL + D_ATTN:, :], preferred_element_type=F32)
    hn = h_ref[...] + acc
    if final:
        ms = jnp.mean(hn * hn, axis=-1, keepdims=True)
        hn = hn * lax.rsqrt(ms + NORM_EPS) * fw_ref[...]
    o_ref[...] = hn


def _outproj(h2d, mp, ma, mr, w_out_bf16, final_w, final, tm):
    m = h2d.shape[0]
    row = lambda width: pl.BlockSpec((tm, width), lambda i: (i, 0))
    return pl.pallas_call(
        functools.partial(_outproj_kernel, final=final),
        grid=(m // tm,),
        in_specs=[row(D_MODEL), row(D_POOL), row(D_ATTN), row(D_RWKV),
                  pl.BlockSpec(memory_space=pltpu.VMEM),
                  pl.BlockSpec((1, D_MODEL), lambda i: (0, 0))],
        out_specs=row(D_MODEL),
        out_shape=jax.ShapeDtypeStruct((m, D_MODEL), F32),
        compiler_params=_cparams(("arbitrary",)),
        name="outproj",
    )(h2d, mp, ma, mr, w_out_bf16, final_w.reshape(1, D_MODEL))


def _pool_kernel(pg_ref, halo_ref, hist_ref, pw_ref, ps_ref, o_ref, *, tt, pos0):
    i = pl.program_id(1)
    p = pg_ref[0, :, 0:D_POOL]
    gate = pg_ref[0, :, D_POOL:2 * D_POOL]
    halo = jnp.where(i == 0, hist_ref[0], halo_ref[0, :, 0:D_POOL])
    ext = jnp.concatenate([halo, p], axis=0)
    pos = pos0 + i * tt + lax.broadcasted_iota(jnp.int32, (tt, 1), 0)
    outs = []
    for g, w in enumerate(POOL_WINDOWS):
        s = ext[:, g * POOL_GROUP:(g + 1) * POOL_GROUP]
        span = 1
        while span < w:
            n = s.shape[0]
            s = s[span:n] + s[0:n - span]
            span *= 2
        win = s[16 - (w - 1):16 - (w - 1) + tt]
        cnt = jnp.minimum(pos + 1, w).astype(F32)
        d = win / cnt - p[:, g * POOL_GROUP:(g + 1) * POOL_GROUP]
        outs.append(jnp.dot(d.astype(BF16), pw_ref[g], preferred_element_type=F32))
    y = jnp.concatenate(outs, axis=1) * ps_ref[...]
    o_ref[0] = (y * _silu(gate)).astype(BF16)


def _pool(pg, hist16, pool_w_bf16, pool_scale, pos0, tt):
    b, t, _ = pg.shape
    nh = tt // 16
    return pl.pallas_call(
        functools.partial(_pool_kernel, tt=tt, pos0=pos0),
        grid=(b, t // tt),
        in_specs=[
            pl.BlockSpec((1, tt, 2 * D_POOL), lambda bi, i: (bi, i, 0)),
            pl.BlockSpec((1, 16, 2 * D_POOL), lambda bi, i: (bi, jnp.maximum(i * nh - 1, 0), 0)),
            pl.BlockSpec((1, 16, D_POOL), lambda bi, i: (bi, 0, 0)),
            pl.BlockSpec((4, POOL_GROUP, POOL_GROUP), lambda bi, i: (0, 0, 0)),
            pl.BlockSpec((1, D_POOL), lambda bi, i: (0, 0)),
        ],
        out_specs=pl.BlockSpec((1, tt, D_POOL), lambda bi, i: (bi, i, 0)),
        out_shape=jax.ShapeDtypeStruct((b, t, D_POOL), BF16),
        compiler_params=_cparams(("arbitrary", "arbitrary")),
        name="pool",
    )(pg, pg, hist16, pool_w_bf16, pool_scale.reshape(1, D_POOL))


def _t5_bucket(rel):
    nb = NUM_BUCKETS // 2
    max_exact = nb // 2
    ret = jnp.where(rel > 0, nb, 0)
    n = jnp.abs(rel)
    nf = jnp.maximum(n, 1).astype(F32)
    large = max_exact + (jnp.log(nf / max_exact) / math.log(MAX_DISTANCE / max_exact) * (nb - max_exact)).astype(jnp.int32)
    large = jnp.minimum(large, nb - 1)
    return ret + jnp.where(n < max_exact, n, large)


def _bias_kernel(bucket_ref, tab_ref, o_ref):
    bucket = bucket_ref[...]
    tab = tab_ref[...]
    out = jnp.full(bucket.shape, NEG, F32)
    for b in range(NUM_BUCKETS):
        out = jnp.where(bucket == b, tab[:, b:b + 1], out)
    o_ref[...] = out


def _attn_bias(table, tq, nk, prefix_valid):
    qi = jnp.arange(tq)
    kj = jnp.arange(nk)
    rel = kj[None, :] - WINDOW - qi[:, None]
    bucket = _t5_bucket(rel)
    qc = qi // CHUNK
    kc = (kj - WINDOW) // CHUNK
    valid = (kc[None, :] <= qc[:, None]) & (kc[None, :] >= qc[:, None] - WIN_CHUNKS)
    valid &= (kj < WINDOW + tq)[None, :]
    if not prefix_valid:
        valid &= (kj >= WINDOW)[None, :]
    bucket = jnp.where(valid, bucket, -1).astype(jnp.int32)
    heads = jnp.array([[[8 * g + 2 * p + par for p in range(4)] for par in range(2)] for g in range(2)]).reshape(-1)
    rows = heads.shape[0] * tq
    bucket_rows = jnp.broadcast_to(bucket[None], (heads.shape[0], tq, nk)).reshape(rows, nk)
    tab_rows = jnp.broadcast_to(table.astype(F32).T[heads][:, None, :], (heads.shape[0], tq, NUM_BUCKETS)).reshape(rows, NUM_BUCKETS)
    tr = 4 * tq
    return pl.pallas_call(
        _bias_kernel,
        grid=(rows // tr,),
        in_specs=[pl.BlockSpec((tr, nk), lambda i: (i, 0)), pl.BlockSpec((tr, NUM_BUCKETS), lambda i: (i, 0))],
        out_specs=pl.BlockSpec((tr, nk), lambda i: (i, 0)),
        out_shape=jax.ShapeDtypeStruct((rows, nk), F32),
        name="attn_bias",
    )(bucket_rows, tab_rows)


def _sink_rows(sinks, tq):
    heads = jnp.array([[[8 * g + 2 * p + par for p in range(4)] for par in range(2)] for g in range(2)]).reshape(-1)
    return jnp.broadcast_to(sinks.astype(F32)[heads][:, None], (heads.shape[0], tq)).reshape(-1, 1)


def _attn_kernel(*refs, tq, has_prev):
    if has_prev:
        q_ref, kvc_ref, kvp_ref, pre_ref, g_ref, bias_ref, sink_ref, o_ref = refs
    else:
        q_ref, kvc_ref, pre_ref, g_ref, bias_ref, sink_ref, o_ref = refs
    i = pl.program_id(1)
    if has_prev:
        kvp = jnp.where(i == 0, pre_ref[0], kvp_ref[0])
    else:
        kvp = pre_ref[0]
    kv = jnp.concatenate([kvp, kvc_ref[0]], axis=0)
    nk = kv.shape[0]
    k = kv[:, 0:LANES] * (HEAD_DIM ** -0.5)
    v = kv[:, LANES:2 * LANES]
    low = lax.broadcasted_iota(jnp.int32, (nk, LANES), 1) < HEAD_DIM
    k_sw = pltpu.roll(k, HEAD_DIM, axis=1)
    v_sw = pltpu.roll(v, HEAD_DIM, axis=1)
    zero = jnp.zeros_like(k)
    k_even = (jnp.where(low, k, zero).astype(BF16), jnp.where(low, k_sw, zero).astype(BF16))
    k_odd = (jnp.where(low, zero, k_sw).astype(BF16), jnp.where(low, zero, k).astype(BF16))
    v_even = (jnp.where(low, v, zero).astype(BF16), jnp.where(low, v_sw, zero).astype(BF16))
    v_odd = (jnp.where(low, zero, v_sw).astype(BF16), jnp.where(low, zero, v).astype(BF16))
    for g in range(N_KV_HEADS):
        qs = jnp.concatenate([q_ref[0, :, (4 * g + p) * LANES:(4 * g + p + 1) * LANES] for p in range(4)], axis=0).astype(BF16)
        acc = None
        for par, (kx, vx) in enumerate(((k_even[g], v_even[g]), (k_odd[g], v_odd[g]))):
            r0 = (2 * g + par) * 4 * tq
            s = lax.dot_general(qs, kx, (NT, ((), ())), preferred_element_type=F32)
            s = s + bias_ref[r0:r0 + 4 * tq, :]
            sink = sink_ref[r0:r0 + 4 * tq, :]
            m = jnp.maximum(jnp.max(s, axis=-1, keepdims=True), sink)
            e = jnp.exp(s - m)
            den = jnp.sum(e, axis=-1, keepdims=True) + jnp.exp(sink - m)
            pv = jnp.dot(e.astype(BF16), vx, preferred_element_type=F32)
            pv = pv / den
            acc = pv if acc is None else acc + pv
        for p in range(4):
            cols = slice((4 * g + p) * LANES, (4 * g + p + 1) * LANES)
            o_ref[0, :, cols] = (acc[p * tq:(p + 1) * tq] * _silu(g_ref[0, :, cols])).astype(BF16)


def _attn(q, kv, prefix, gate, bias, sink_rows, tq):
    b, t, _ = q.shape
    n_tiles = t // tq
    has_prev = n_tiles > 1
    nk = WINDOW + tq
    rows = bias.shape[-2]
    in_specs = [
        pl.BlockSpec((1, tq, D_ATTN), lambda bi, i: (bi, i, 0)),
        pl.BlockSpec((1, tq, 4 * HEAD_DIM), lambda bi, i: (bi, i, 0)),
    ]
    args = [q, kv]
    if has_prev:
        in_specs.append(pl.BlockSpec((1, WINDOW, 4 * HEAD_DIM), lambda bi, i: (bi, jnp.maximum(i - 1, 0), 0)))
        args.append(kv)
    in_specs += [
        pl.BlockSpec((1, WINDOW, 4 * HEAD_DIM), lambda bi, i: (bi, 0, 0)),
        pl.BlockSpec((1, tq, D_ATTN), lambda bi, i: (bi, i, 0)),
        pl.BlockSpec((None, rows, nk), lambda bi, i: (jnp.minimum(i, bias.shape[0] - 1), 0, 0)),
        pl.BlockSpec((rows, 1), lambda bi, i: (0, 0)),
    ]
    args += [prefix, gate, bias, sink_rows]
    return pl.pallas_call(
        functools.partial(_attn_kernel, tq=tq, has_prev=has_prev),
        grid=(b, n_tiles),
        in_specs=in_specs,
        out_specs=pl.BlockSpec((1, tq, D_ATTN), lambda bi, i: (bi, i, 0)),
        out_shape=jax.ShapeDtypeStruct((b, t, D_ATTN), BF16),
        compiler_params=_cparams(("arbitrary", "arbitrary")),
        name="attn",
    )(*args)


def _tri_inverse(a, blk_mask, merge_masks, eye):
    d = jnp.where(blk_mask, a, 0.0)
    t = eye + d
    power = d
    span = 1
    while 2 * span < INV_BASE:
        power = _mm(power, power, 3)
        t = t + _mm(power, t, 3)
        span *= 2
    for mask in merge_masks:
        a21 = jnp.where(mask, a, 0.0)
        t = t + _mm(_mm(t, a21, 3), t, 3)
    return t


def _rwkv_kernel(xc_ref, g_ref, shift_ref, p0_ref, mu_ref, w0_ref, a0_ref, lora_ref, kk_ref, ka_ref, rk_ref,
                 lnw_ref, lnb_ref, o_ref, pout_ref, prev_scr, state_scr, *, c, t_valid):
    i = pl.program_id(1)
    n_pairs = N_RWKV_HEADS // 2

    @pl.when(i == 0)
    def _():
        prev_scr[...] = shift_ref[0]
        state_scr[...] = p0_ref[0]

    xc = xc_ref[0]
    row = lax.broadcasted_iota(jnp.int32, (c, 1), 0)
    prev = jnp.where(row == 0, prev_scr[...], pltpu.roll(xc, 1, axis=0))
    prev_scr[...] = xc[t_valid - 1:t_valid] if t_valid < c else xc[c - 1:c]
    xs = xc + (prev - xc) * mu_ref[...]
    r = xs[:, 0:D_RWKV]
    k = xs[:, D_RWKV:2 * D_RWKV]
    v = xs[:, 2 * D_RWKV:3 * D_RWKV]
    lo = xs[:, 3 * D_RWKV:3 * D_RWKV + 2 * LORA]
    lane = lax.broadcasted_iota(jnp.int32, (c, LANES), 1)
    first_half = lane < RWKV_HEAD
    lora_in = jnp.where(first_half, jnp.tanh(lo), lo)
    lora = _mm(lora_in, lora_ref[...], 3)
    z = -(w0_ref[...] + lora[:, 0:D_RWKV])
    softplus = jnp.maximum(z, 0.0) + jnp.log(1.0 + jnp.exp(-jnp.abs(z)))
    logw = -jnp.exp(-softplus - 0.5)
    a = _sigmoid(a0_ref[...] + lora[:, D_RWKV:2 * D_RWKV])

    hi = _shr(lax.broadcasted_iota(jnp.int32, (D_RWKV, D_RWKV), 0), HEAD_SHIFT)
    hj = _shr(lax.broadcasted_iota(jnp.int32, (D_RWKV, D_RWKV), 1), HEAD_SHIFT)
    head_ones = (hi == hj).astype(BF16)

    kk = k * kk_ref[...]
    kk = kk / jnp.maximum(jnp.sqrt(_mm_exact_rhs(kk * kk, head_ones)), 1e-12)
    k = k * (1.0 + (a - 1.0) * ka_ref[...])
    alpha = -kk
    beta = kk * a
    if t_valid < c:
        live = row < t_valid
        logw = jnp.where(live, logw, 0.0)
        alpha = jnp.where(live, alpha, 0.0)
        beta = jnp.where(live, beta, 0.0)
        k = jnp.where(live, k, 0.0)

    ti = lax.broadcasted_iota(jnp.int32, (c, c), 0)
    tj = lax.broadcasted_iota(jnp.int32, (c, c), 1)
    tri_incl = ti >= tj
    tri_strict = ti > tj
    cum = _dot_tri(tri_incl, logw)
    cum_last = cum[c - 1:c]
    g_incl = jnp.exp(cum)
    g_prev = jnp.exp(cum - logw)
    g_inv = jnp.exp(-cum)
    g_rest = jnp.exp(cum_last - cum)
    g_all = jnp.exp(cum_last)

    r_bar = r * g_incl
    a_bar = alpha * g_prev
    b_bar = beta * g_inv
    k_bar = k * g_inv
    b_til = beta * g_rest
    k_til = k * g_rest

    eye = (ti == tj).astype(F32)
    base_shift = INV_BASE.bit_length() - 1
    blk_mask = _shr(ti, base_shift) == _shr(tj, base_shift)
    merge_masks = []
    sh = base_shift
    while (1 << sh) < c:
        merge_masks.append((_shr(ti, sh + 1) == _shr(tj, sh + 1)) & ((_shr(ti, sh) & 1) == 1) & ((_shr(tj, sh) & 1) == 0))
        sh += 1
    pi = lax.broadcasted_iota(jnp.int32, (LANES, LANES), 0)
    pj = lax.broadcasted_iota(jnp.int32, (LANES, LANES), 1)
    head_block = _shr(pi, HEAD_SHIFT) == _shr(pj, HEAD_SHIFT)
    diag128 = pi == pj
    zeros_c = jnp.zeros((c, LANES), F32)

    o_pairs = []
    for p in range(n_pairs):
        cols = slice(p * LANES, (p + 1) * LANES)
        rb, ab, bb, kb, bt, kt, vp = (x[:, cols] for x in (r_bar, a_bar, b_bar, k_bar, b_til, k_til, v))
        lhs = jnp.concatenate([jnp.where(first_half, ab, 0.0), jnp.where(first_half, 0.0, ab),
                               jnp.where(first_half, rb, 0.0), jnp.where(first_half, 0.0, rb)], axis=0)
        rhs = jnp.concatenate([bb, kb], axis=0)
        gm = _mm(lhs, rhs, 3, NT)
        xh = []
        lr = []
        for h in range(2):
            a_ab = jnp.where(tri_strict, gm[h * c:(h + 1) * c, 0:c], 0.0)
            a_ak = jnp.where(tri_strict, gm[h * c:(h + 1) * c, c:2 * c], 0.0)
            l_rb = jnp.where(tri_incl, gm[(2 + h) * c:(3 + h) * c, 0:c], 0.0)
            l_rk = jnp.where(tri_incl, gm[(2 + h) * c:(3 + h) * c, c:2 * c], 0.0)
            t_inv = _tri_inverse(a_ab, blk_mask, merge_masks, eye)
            akv = _mm(a_ak, vp)
            xh.append(_mm(t_inv, jnp.concatenate([ab, akv], axis=1)))
            lr.append(jnp.concatenate([l_rb, l_rk], axis=1))
        a_new = jnp.where(first_half, xh[0][:, 0:LANES], xh[1][:, 0:LANES])
        u0 = jnp.where(first_half, xh[0][:, LANES:], xh[1][:, LANES:])
        zmat = jnp.concatenate([jnp.concatenate([a_new, u0], axis=1),
                                jnp.concatenate([zeros_c, vp], axis=1)], axis=0)
        yh = [_mm(lr[h], zmat) for h in range(2)]
        r_new = rb + jnp.where(first_half, yh[0][:, 0:LANES], yh[1][:, 0:LANES])
        o0 = jnp.where(first_half, yh[0][:, LANES:], yh[1][:, LANES:])
        mn = _mm(jnp.concatenate([bt, kt], axis=0), zmat, 1, TN)
        m_mat = jnp.where(diag128, g_all[:, cols], 0.0) + jnp.where(head_block, mn[:, 0:LANES], 0.0)
        n0 = jnp.where(head_block, mn[:, LANES:], 0.0)
        state = state_scr[p]
        op = _mm(jnp.concatenate([r_new, m_mat], axis=0), state, 3)
        o_pairs.append(op[0:c] + o0)
        state_scr[p] = op[c:] + n0

    o = jnp.concatenate(o_pairs, axis=1)
    inv_n = 1.0 / RWKV_HEAD
    mean = _mm_exact_rhs(o, head_ones) * inv_n
    cen = o - mean
    var = _mm_exact_rhs(cen * cen, head_ones) * inv_n
    y = cen * lax.rsqrt(var + LNX_EPS) * lnw_ref[...] + lnb_ref[...]
    bonus = _mm_exact_rhs(r * k * rk_ref[...], head_ones) * v
    o_ref[0] = ((y + bonus) * _silu(g_ref[0])).astype(BF16)

    @pl.when(i == pl.num_programs(1) - 1)
    def _():
        pout_ref[0] = state_scr[...]


def _dot_tri(tri_incl, x):
    tri = tri_incl.astype(BF16)
    out = None
    for part in _split(x, 3):
        d = jnp.dot(tri, part, preferred_element_type=F32)
        out = d if out is None else out + d
    return out


def _rwkv(xc, gate, shift_prev, p0, lp, c, t_valid):
    b, t, _ = xc.shape
    vec = lambda n: pl.BlockSpec((1, n), lambda bi, i: (0, 0))
    n_pairs = N_RWKV_HEADS // 2
    return pl.pallas_call(
        functools.partial(_rwkv_kernel, c=c, t_valid=t_valid),
        grid=(b, t // c),
        in_specs=[
            pl.BlockSpec((1, c, D_SHIFT), lambda bi, i: (bi, i, 0)),
            pl.BlockSpec((1, c, D_RWKV), lambda bi, i: (bi, i, 0)),
            pl.BlockSpec((1, 1, D_SHIFT), lambda bi, i: (bi, 0, 0)),
            pl.BlockSpec((1, n_pairs, LANES, LANES), lambda bi, i: (bi, 0, 0, 0)),
            vec(D_SHIFT), vec(D_RWKV), vec(D_RWKV),
            pl.BlockSpec((2 * LORA, 2 * D_RWKV), lambda bi, i: (0, 0)),
            vec(D_RWKV), vec(D_RWKV), vec(D_RWKV), vec(D_RWKV), vec(D_RWKV),
        ],
        out_specs=[
            pl.BlockSpec((1, c, D_RWKV), lambda bi, i: (bi, i, 0)),
            pl.BlockSpec((1, n_pairs, LANES, LANES), lambda bi, i: (bi, 0, 0, 0)),
        ],
        out_shape=[jax.ShapeDtypeStruct((b, t, D_RWKV), BF16),
                   jax.ShapeDtypeStruct((b, n_pairs, LANES, LANES), F32)],
        scratch_shapes=[pltpu.VMEM((1, D_SHIFT), F32), pltpu.VMEM((n_pairs, LANES, LANES), F32)],
        compiler_params=_cparams(("arbitrary", "arbitrary")),
        name="rwkv",
    )(xc, gate, shift_prev.reshape(b, 1, D_SHIFT), p0,
      lp["mu"], lp["w0"], lp["a0"], lp["lora"], lp["k_k"], lp["k_a"], lp["r_k"], lp["lnx_w"], lp["lnx_b"])


def _state_to_pairs(s):
    b = s.shape[0]
    pt = jnp.swapaxes(s, -1, -2).reshape(b, N_RWKV_HEADS // 2, 2, RWKV_HEAD, RWKV_HEAD)
    z = jnp.zeros_like(pt[:, :, 0])
    top = jnp.concatenate([pt[:, :, 0], z], axis=-1)
    bot = jnp.concatenate([z, pt[:, :, 1]], axis=-1)
    return jnp.concatenate([top, bot], axis=-2)


def _pairs_to_state(pm):
    b = pm.shape[0]
    h0 = pm[:, :, 0:RWKV_HEAD, 0:RWKV_HEAD]
    h1 = pm[:, :, RWKV_HEAD:, RWKV_HEAD:]
    pt = jnp.stack([h0, h1], axis=2).reshape(b, N_RWKV_HEADS, RWKV_HEAD, RWKV_HEAD)
    return jnp.swapaxes(pt, -1, -2)


def _layer(h, lp, att, pool_hist, shift_prev, wkv0, kv_prefix, pos0, final_w, final, cfg):
    b, t, _ = h.shape
    pg, q, kv, ga, xc, gr = _inproj(h.reshape(b * t, D_MODEL), lp["norm_w"], lp["w_in"], cfg["tm_in"])
    shape3 = lambda x: x.reshape(b, t, x.shape[-1])
    pg, q, kv, ga, xc, gr = (shape3(x) for x in (pg, q, kv, ga, xc, gr))

    hist16 = jnp.concatenate([jnp.zeros((b, 1, D_POOL), F32), pool_hist], axis=1)
    mp = _pool(pg, hist16, lp["pool_w"], lp["pool_scale"], pos0, cfg["tt_pool"])
    ma = _attn(q, kv, kv_prefix, ga, att["bias"], _sink_rows(lp["sinks"], cfg["tq"]), cfg["tq"])

    c = RWKV_CHUNK
    if t % c:
        pad = c - t % c
        xc_in = jnp.pad(xc, ((0, 0), (0, pad), (0, 0)))
        gr_in = jnp.pad(gr, ((0, 0), (0, pad), (0, 0)))
        t_valid = t
    else:
        xc_in, gr_in, t_valid = xc, gr, c
    mr, p_new = _rwkv(xc_in, gr_in, shift_prev, _state_to_pairs(wkv0), lp, c, t_valid)
    mr = mr[:, :t]

    flat = lambda x: x.reshape(b * t, x.shape[-1])
    h_new = _outproj(flat(h), flat(mp), flat(ma), flat(mr), lp["w_out"], final_w, final, cfg["tm_out"])
    new_pool = pg[:, -POOL_HIST:, 0:D_POOL]
    kvf = kv[:, -WINDOW:] if t >= WINDOW else jnp.concatenate([kv_prefix[:, t:], kv], axis=1)
    new_k = kvf[:, :, 0:LANES].reshape(b, WINDOW, N_KV_HEADS, HEAD_DIM)
    new_v = kvf[:, :, LANES:].reshape(b, WINDOW, N_KV_HEADS, HEAD_DIM)
    new_shift = xc[:, -1]
    return h_new.reshape(b, t, D_MODEL), (new_pool, new_k, new_v, new_shift, _pairs_to_state(p_new))


def _group_cfg(b, t):
    m = b * t
    tq = min(t, ATT_TILE)
    return {"tm_in": min(m, 256), "tm_out": min(m, 512), "tt_pool": min(t, 512), "tq": tq}


def kernel(x_prompt, x_sample, state_pool, cache_swa_k, cache_swa_v, state_rwkv_shift, state_rwkv_wkv, norm_w, w_in, w_out, pool_w, pool_scale, attn_sinks, rel_bias_table, rwkv_mu, rwkv_w0, rwkv_w_up, rwkv_a0, rwkv_a_up, rwkv_k_k, rwkv_k_a, rwkv_r_k, rwkv_lnx_w, rwkv_lnx_b, final_norm_w):
    bp, tp, _ = x_prompt.shape
    bs, ts, _ = x_sample.shape
    cfg_p = _group_cfg(bp, tp)
    cfg_s = _group_cfg(bs, ts)
    att_p = {"bias": jnp.stack([_attn_bias(rel_bias_table, cfg_p["tq"], WINDOW + cfg_p["tq"], False),
                                _attn_bias(rel_bias_table, cfg_p["tq"], WINDOW + cfg_p["tq"], True)])}
    att_s = {"bias": _attn_bias(rel_bias_table, cfg_s["tq"], WINDOW + cfg_s["tq"], True)[None]}

    hp, hs = x_prompt, x_sample
    prompt_states, sample_states = [], []
    row = lambda x: x.astype(F32).reshape(1, -1)
    zero_lora = jnp.zeros((LORA, D_RWKV), F32)
    for l in range(DEPTH):
        lp = {
            "norm_w": norm_w[l], "w_in": w_in[l].astype(BF16), "w_out": w_out[l].astype(BF16),
            "pool_w": pool_w[l].astype(BF16), "pool_scale": pool_scale[l], "sinks": attn_sinks[l],
            "mu": row(rwkv_mu[l]), "w0": row(rwkv_w0[l]), "a0": row(rwkv_a0[l]),
            "lora": jnp.concatenate([jnp.concatenate([rwkv_w_up[l].astype(F32), zero_lora], axis=1),
                                     jnp.concatenate([zero_lora, rwkv_a_up[l].astype(F32)], axis=1)], axis=0),
            "k_k": row(rwkv_k_k[l]), "k_a": row(rwkv_k_a[l]), "r_k": row(rwkv_r_k[l]),
            "lnx_w": row(rwkv_lnx_w[l]), "lnx_b": row(rwkv_lnx_b[l]),
        }
        final = l == DEPTH - 1
        hp, sp = _layer(hp, lp, att_p, jnp.zeros((bp, POOL_HIST, D_POOL), F32), jnp.zeros((bp, D_SHIFT), F32),
                        jnp.zeros((bp, N_RWKV_HEADS, RWKV_HEAD, RWKV_HEAD), F32),
                        jnp.zeros((bp, WINDOW, 4 * HEAD_DIM), F32), 0, final_norm_w, final, cfg_p)
        prefix_s = jnp.concatenate([cache_swa_k[l].reshape(bs, WINDOW, LANES), cache_swa_v[l].reshape(bs, WINDOW, LANES)], axis=-1)
        hs, ss = _layer(hs, lp, att_s, state_pool[l], state_rwkv_shift[l], state_rwkv_wkv[l], prefix_s, PAST_LEN,
                        final_norm_w, final, cfg_s)
        prompt_states.append(sp)
        sample_states.append(ss)
    outs_p = [jnp.stack(x) for x in zip(*prompt_states)]
    outs_s = [jnp.stack(x) for x in zip(*sample_states)]
    return (hp, hs, *outs_p, *outs_s)
```

```python
import functools
import math

import jax
import jax.numpy as jnp
from jax import lax
from jax.experimental import pallas as pl
from jax.experimental.pallas import tpu as pltpu

F32 = jnp.float32
BF16 = jnp.bfloat16

D_MODEL = 2048
DEPTH = 4
PAST_LEN = 1024
CHUNK = 64
D_POOL = 512
POOL_WINDOWS = (2, 4, 8, 16)
POOL_GROUP = 128
POOL_HIST = 15
HEAD_DIM = 64
D_ATTN = 1024
N_Q_HEADS = 16
N_KV_HEADS = 2
WINDOW = 128
WIN_CHUNKS = 2
NUM_BUCKETS = 32
MAX_DISTANCE = 128
NEG = -1e30
D_RWKV = 512
RWKV_HEAD = 64
N_RWKV_HEADS = 8
LORA = 64
D_SHIFT = 3 * D_RWKV + 2 * LORA
NORM_EPS = 1e-5
LNX_EPS = 1e-5 * RWKV_HEAD
SEGMENTS = (2 * D_POOL, D_ATTN, 2 * N_KV_HEADS * HEAD_DIM, D_ATTN, D_SHIFT, D_RWKV)
D_IN = sum(SEGMENTS)

LANES = 128
ATT_TILE = 128
RWKV_CHUNK = 128
INV_BASE = 16
VMEM_LIMIT = 56 * 1024 * 1024


def _cparams(sem):
    return pltpu.CompilerParams(dimension_semantics=sem, vmem_limit_bytes=VMEM_LIMIT)


def _sigmoid(x):
    return 1.0 / (1.0 + jnp.exp(-x))


def _silu(x):
    return x * _sigmoid(x)


def _split(x, terms):
    parts = []
    rem = x
    for _ in range(terms):
        p = rem.astype(BF16)
        parts.append(p)
        rem = rem - p.astype(F32)
    return parts


def _dot(a, b, dims=None):
    if dims is None:
        return jnp.dot(a, b, preferred_element_type=F32)
    return lax.dot_general(a, b, (dims, ((), ())), preferred_element_type=F32)


def _mm(a, b, passes=1, dims=None):
    if passes == 1:
        return _dot(a.astype(BF16), b.astype(BF16), dims)
    ah, al = _split(a, 2)
    bh, bl = _split(b, 2)
    return _dot(ah, bh, dims) + (_dot(ah, bl, dims) + _dot(al, bh, dims))


def _mm_exact_rhs(a, b_bf16, terms=3):
    out = None
    for p in _split(a, terms):
        d = _dot(p, b_bf16)
        out = d if out is None else out + d
    return out


HEAD_SHIFT = RWKV_HEAD.bit_length() - 1


def _shr(x, bits):
    return lax.shift_right_logical(x, jnp.full_like(x, bits))


NT = ((1,), (1,))
TN = ((0,), (0,))


def _inproj_kernel(h_ref, nw_ref, w_ref, *o_refs):
    x = h_ref[...]
    ms = jnp.mean(x * x, axis=-1, keepdims=True)
    xn = (x * lax.rsqrt(ms + NORM_EPS) * nw_ref[...]).astype(BF16)
    off = 0
    for o_ref in o_refs:
        width = o_ref.shape[-1]
        o_ref[...] = jnp.dot(xn, w_ref[:, off:off + width], preferred_element_type=F32)
        off += width


def _inproj(h2d, norm_w, w_in_bf16, tm):
    m = h2d.shape[0]
    return pl.pallas_call(
        _inproj_kernel,
        grid=(m // tm,),
        in_specs=[
            pl.BlockSpec((tm, D_MODEL), lambda i: (i, 0)),
            pl.BlockSpec((1, D_MODEL), lambda i: (0, 0)),
            pl.BlockSpec(memory_space=pltpu.VMEM),
        ],
        out_specs=[pl.BlockSpec((tm, s), lambda i: (i, 0)) for s in SEGMENTS],
        out_shape=[jax.ShapeDtypeStruct((m, s), F32) for s in SEGMENTS],
        compiler_params=_cparams(("arbitrary",)),
        name="inproj",
    )(h2d, norm_w.reshape(1, D_MODEL), w_in_bf16)


def _outproj_kernel(h_ref, mp_ref, ma_ref, mr_ref, w_ref, fw_ref, o_ref, *, final):
    acc = jnp.dot(mp_ref[...], w_ref[0:D_POOL, :], preferred_element_type=F32)
    acc += jnp.dot(ma_ref[...], w_ref[D_POOL:D_POOL + D_ATTN, :], preferred_element_type=F32)
    acc += jnp.dot(mr_ref[...], w_ref[D_POOL + D_ATTN:, :], preferred_element_type=F32)
    hn = h_ref[...] + acc
    if final:
        ms = jnp.mean(hn * hn, axis=-1, keepdims=True)
        hn = hn * lax.rsqrt(ms + NORM_EPS) * fw_ref[...]
    o_ref[...] = hn


def _outproj(h2d, mp, ma, mr, w_out_bf16, final_w, final, tm):
    m = h2d.shape[0]
    row = lambda width: pl.BlockSpec((tm, width), lambda i: (i, 0))
    return pl.pallas_call(
        functools.partial(_outproj_kernel, final=final),
        grid=(m // tm,),
        in_specs=[row(D_MODEL), row(D_POOL), row(D_ATTN), row(D_RWKV),
                  pl.BlockSpec(memory_space=pltpu.VMEM),
                  pl.BlockSpec((1, D_MODEL), lambda i: (0, 0))],
        out_specs=row(D_MODEL),
        out_shape=jax.ShapeDtypeStruct((m, D_MODEL), F32),
        compiler_params=_cparams(("arbitrary",)),
        name="outproj",
    )(h2d, mp, ma, mr, w_out_bf16, final_w.reshape(1, D_MODEL))


def _pool_kernel(pg_ref, halo_ref, hist_ref, pw_ref, ps_ref, o_ref, *, tt, pos0):
    i = pl.program_id(1)
    p = pg_ref[0, :, 0:D_POOL]
    gate = pg_ref[0, :, D_POOL:2 * D_POOL]
    halo = jnp.where(i == 0, hist_ref[0], halo_ref[0, :, 0:D_POOL])
    ext = jnp.concatenate([halo, p], axis=0)
    pos = pos0 + i * tt + lax.broadcasted_iota(jnp.int32, (tt, 1), 0)
    outs = []
    for g, w in enumerate(POOL_WINDOWS):
        s = ext[:, g * POOL_GROUP:(g + 1) * POOL_GROUP]
        span = 1
        while span < w:
            n = s.shape[0]
            s = s[span:n] + s[0:n - span]
            span *= 2
        win = s[16 - (w - 1):16 - (w - 1) + tt]
        cnt = jnp.minimum(pos + 1, w).astype(F32)
        d = win / cnt - p[:, g * POOL_GROUP:(g + 1) * POOL_GROUP]
        outs.append(jnp.dot(d.astype(BF16), pw_ref[g], preferred_element_type=F32))
    y = jnp.concatenate(outs, axis=1) * ps_ref[...]
    o_ref[0] = (y * _silu(gate)).astype(BF16)


def _pool(pg, hist16, pool_w_bf16, pool_scale, pos0, tt):
    b, t, _ = pg.shape
    nh = tt // 16
    return pl.pallas_call(
        functools.partial(_pool_kernel, tt=tt, pos0=pos0),
        grid=(b, t // tt),
        in_specs=[
            pl.BlockSpec((1, tt, 2 * D_POOL), lambda bi, i: (bi, i, 0)),
            pl.BlockSpec((1, 16, 2 * D_POOL), lambda bi, i: (bi, jnp.maximum(i * nh - 1, 0), 0)),
            pl.BlockSpec((1, 16, D_POOL), lambda bi, i: (bi, 0, 0)),
            pl.BlockSpec((4, POOL_GROUP, POOL_GROUP), lambda bi, i: (0, 0, 0)),
            pl.BlockSpec((1, D_POOL), lambda bi, i: (0, 0)),
        ],
        out_specs=pl.BlockSpec((1, tt, D_POOL), lambda bi, i: (bi, i, 0)),
        out_shape=jax.ShapeDtypeStruct((b, t, D_POOL), BF16),
        compiler_params=_cparams(("arbitrary", "arbitrary")),
        name="pool",
    )(pg, pg, hist16, pool_w_bf16, pool_scale.reshape(1, D_POOL))


def _t5_bucket(rel):
    nb = NUM_BUCKETS // 2
    max_exact = nb // 2
    ret = jnp.where(rel > 0, nb, 0)
    n = jnp.abs(rel)
    nf = jnp.maximum(n, 1).astype(F32)
    large = max_exact + (jnp.log(nf / max_exact) / math.log(MAX_DISTANCE / max_exact) * (nb - max_exact)).astype(jnp.int32)
    large = jnp.minimum(large, nb - 1)
    return ret + jnp.where(n < max_exact, n, large)


def _bias_kernel(bucket_ref, tab_ref, o_ref):
    bucket = bucket_ref[...]
    tab = tab_ref[...]
    out = jnp.full(bucket.shape, NEG, F32)
    for b in range(NUM_BUCKETS):
        out = jnp.where(bucket == b, tab[:, b:b + 1], out)
    o_ref[...] = out


def _attn_bias(table, tq, nk, prefix_valid):
    qi = jnp.arange(tq)
    kj = jnp.arange(nk)
    rel = kj[None, :] - WINDOW - qi[:, None]
    bucket = _t5_bucket(rel)
    qc = qi // CHUNK
    kc = (kj - WINDOW) // CHUNK
    valid = (kc[None, :] <= qc[:, None]) & (kc[None, :] >= qc[:, None] - WIN_CHUNKS)
    valid &= (kj < WINDOW + tq)[None, :]
    if not prefix_valid:
        valid &= (kj >= WINDOW)[None, :]
    bucket = jnp.where(valid, bucket, -1).astype(jnp.int32)
    heads = jnp.array([[[8 * g + 2 * p + par for p in range(4)] for par in range(2)] for g in range(2)]).reshape(-1)
    rows = heads.shape[0] * tq
    bucket_rows = jnp.broadcast_to(bucket[None], (heads.shape[0], tq, nk)).reshape(rows, nk)
    tab_rows = jnp.broadcast_to(table.astype(F32).T[heads][:, None, :], (heads.shape[0], tq, NUM_BUCKETS)).reshape(rows, NUM_BUCKETS)
    tr = 4 * tq
    return pl.pallas_call(
        _bias_kernel,
        grid=(rows // tr,),
        in_specs=[pl.BlockSpec((tr, nk), lambda i: (i, 0)), pl.BlockSpec((tr, NUM_BUCKETS), lambda i: (i, 0))],
        out_specs=pl.BlockSpec((tr, nk), lambda i: (i, 0)),
        out_shape=jax.ShapeDtypeStruct((rows, nk), F32),
        name="attn_bias",
    )(bucket_rows, tab_rows)


def _sink_rows(sinks, tq):
    heads = jnp.array([[[8 * g + 2 * p + par for p in range(4)] for par in range(2)] for g in range(2)]).reshape(-1)
    return jnp.broadcast_to(sinks.astype(F32)[heads][:, None], (heads.shape[0], tq)).reshape(-1, 1)


def _attn_kernel(*refs, tq, has_prev):
    if has_prev:
        q_ref, kvc_ref, kvp_ref, pre_ref, g_ref, bias_ref, sink_ref, o_ref = refs
    else:
        q_ref, kvc_ref, pre_ref, g_ref, bias_ref, sink_ref, o_ref = refs
    i = pl.program_id(1)
    if has_prev:
        kvp = jnp.where(i == 0, pre_ref[0], kvp_ref[0])
    else:
        kvp = pre_ref[0]
    kv = jnp.concatenate([kvp, kvc_ref[0]], axis=0)
    nk = kv.shape[0]
    k = kv[:, 0:LANES] * (HEAD_DIM ** -0.5)
    v = kv[:, LANES:2 * LANES]
    low = lax.broadcasted_iota(jnp.int32, (nk, LANES), 1) < HEAD_DIM
    k_sw = pltpu.roll(k, HEAD_DIM, axis=1)
    v_sw = pltpu.roll(v, HEAD_DIM, axis=1)
    ones = jnp.ones_like(v)

    def place(x, x_sw, g, par):
        src = x if g == par else x_sw
        return jnp.where(low, src, 0.0) if par == 0 else jnp.where(low, 0.0, src)

    blocks = [(g, par) for g in range(N_KV_HEADS) for par in range(2)]
    kx = [place(k, k_sw, g, par).astype(BF16) for g, par in blocks]
    vx = [jnp.concatenate([place(v, v_sw, g, par), ones], axis=1).astype(BF16) for g, par in blocks]
    qs = [jnp.concatenate([q_ref[0, :, (4 * g + p) * LANES:(4 * g + p + 1) * LANES] for p in range(4)], axis=0).astype(BF16)
          for g in range(N_KV_HEADS)]
    rows = [slice(n * 4 * tq, (n + 1) * 4 * tq) for n in range(len(blocks))]
    s = [_dot(qs[g], kx[n], NT) + bias_ref[rows[n], :] for n, (g, par) in enumerate(blocks)]
    sink = [sink_ref[rows[n], :] for n in range(len(blocks))]
    m = [jnp.maximum(jnp.max(s[n], axis=-1, keepdims=True), sink[n]) for n in range(len(blocks))]
    e = [jnp.exp(s[n] - m[n]).astype(BF16) for n in range(len(blocks))]
    pv = [_dot(e[n], vx[n]) for n in range(len(blocks))]
    out = [pv[n][:, 0:LANES] / (pv[n][:, LANES:] + jnp.exp(sink[n] - m[n])) for n in range(len(blocks))]
    for g in range(N_KV_HEADS):
        acc = out[2 * g] + out[2 * g + 1]
        for p in range(4):
            cols = slice((4 * g + p) * LANES, (4 * g + p + 1) * LANES)
            o_ref[0, :, cols] = (acc[p * tq:(p + 1) * tq] * _silu(g_ref[0, :, cols])).astype(BF16)


def _attn(q, kv, prefix, gate, bias, sink_rows, tq):
    b, t, _ = q.shape
    n_tiles = t // tq
    has_prev = n_tiles > 1
    nk = WINDOW + tq
    rows = bias.shape[-2]
    in_specs = [
        pl.BlockSpec((1, tq, D_ATTN), lambda bi, i: (bi, i, 0)),
        pl.BlockSpec((1, tq, 4 * HEAD_DIM), lambda bi, i: (bi, i, 0)),
    ]
    args = [q, kv]
    if has_prev:
        in_specs.append(pl.BlockSpec((1, WINDOW, 4 * HEAD_DIM), lambda bi, i: (bi, jnp.maximum(i - 1, 0), 0)))
        args.append(kv)
    in_specs += [
        pl.BlockSpec((1, WINDOW, 4 * HEAD_DIM), lambda bi, i: (bi, 0, 0)),
        pl.BlockSpec((1, tq, D_ATTN), lambda bi, i: (bi, i, 0)),
        pl.BlockSpec((None, rows, nk), lambda bi, i: (jnp.minimum(i, bias.shape[0] - 1), 0, 0)),
        pl.BlockSpec((rows, 1), lambda bi, i: (0, 0)),
    ]
    args += [prefix, gate, bias, sink_rows]
    return pl.pallas_call(
        functools.partial(_attn_kernel, tq=tq, has_prev=has_prev),
        grid=(b, n_tiles),
        in_specs=in_specs,
        out_specs=pl.BlockSpec((1, tq, D_ATTN), lambda bi, i: (bi, i, 0)),
        out_shape=jax.ShapeDtypeStruct((b, t, D_ATTN), BF16),
        compiler_params=_cparams(("arbitrary", "arbitrary")),
        name="attn",
    )(*args)


def _mm3(a, b):
    return _dot(a[0], b[0]) + (_dot(a[0], b[1]) + _dot(a[1], b[0]))


def _tri_inverse_all(mats, blk_mask, merge_masks, eye):
    n = len(mats)
    power = [jnp.where(blk_mask, a, 0.0) for a in mats]
    t = [eye + d for d in power]
    span = 1
    while 2 * span < INV_BASE:
        pb = [x.astype(BF16) for x in power]
        power = [_dot(pb[i], pb[i]) for i in range(n)]
        t = [t[i] + _dot(power[i].astype(BF16), t[i].astype(BF16)) for i in range(n)]
        span *= 2
    for mask in merge_masks:
        a21 = [jnp.where(mask, a, 0.0).astype(BF16) for a in mats]
        tb = [x.astype(BF16) for x in t]
        low = [_dot(tb[i], a21[i]).astype(BF16) for i in range(n)]
        t = [t[i] + _dot(low[i], tb[i]) for i in range(n)]
    ts = [_split(x, 2) for x in t]
    resid = [(eye - t[i]) + _mm3(_split(mats[i], 2), ts[i]) for i in range(n)]
    return [t[i] + _dot(ts[i][0], resid[i].astype(BF16)) for i in range(n)]


def _rwkv_kernel(xc_ref, g_ref, shift_ref, p0_ref, mu_ref, w0_ref, a0_ref, lora_ref, kk_ref, ka_ref, rk_ref,
                 lnw_ref, lnb_ref, o_ref, pout_ref, prev_scr, state_scr, *, c, t_valid):
    i = pl.program_id(1)
    n_pairs = N_RWKV_HEADS // 2

    @pl.when(i == 0)
    def _():
        prev_scr[...] = shift_ref[0]
        state_scr[...] = p0_ref[0]

    xc = xc_ref[0]
    row = lax.broadcasted_iota(jnp.int32, (c, 1), 0)
    prev = jnp.where(row == 0, prev_scr[...], pltpu.roll(xc, 1, axis=0))
    prev_scr[...] = xc[t_valid - 1:t_valid] if t_valid < c else xc[c - 1:c]
    xs = xc + (prev - xc) * mu_ref[...]
    r = xs[:, 0:D_RWKV]
    k = xs[:, D_RWKV:2 * D_RWKV]
    v = xs[:, 2 * D_RWKV:3 * D_RWKV]
    lo = xs[:, 3 * D_RWKV:3 * D_RWKV + 2 * LORA]
    lane = lax.broadcasted_iota(jnp.int32, (c, LANES), 1)
    first_half = lane < RWKV_HEAD
    lora_in = jnp.where(first_half, jnp.tanh(lo), lo)
    lora = _mm(lora_in, lora_ref[...], 3)
    z = -(w0_ref[...] + lora[:, 0:D_RWKV])
    softplus = jnp.maximum(z, 0.0) + jnp.log(1.0 + jnp.exp(-jnp.abs(z)))
    logw = -jnp.exp(-softplus - 0.5)
    a = _sigmoid(a0_ref[...] + lora[:, D_RWKV:2 * D_RWKV])

    pi = lax.broadcasted_iota(jnp.int32, (LANES, LANES), 0)
    pj = lax.broadcasted_iota(jnp.int32, (LANES, LANES), 1)
    head_block = _shr(pi, HEAD_SHIFT) == _shr(pj, HEAD_SHIFT)
    head_ones = head_block.astype(BF16)

    def head_sum(x):
        stacked = jnp.concatenate([x[:, p * LANES:(p + 1) * LANES] for p in range(n_pairs)], axis=0)
        s = _mm_exact_rhs(stacked, head_ones, 2)
        return jnp.concatenate([s[p * c:(p + 1) * c] for p in range(n_pairs)], axis=1)

    kk = k * kk_ref[...]
    kk = kk / jnp.maximum(jnp.sqrt(head_sum(kk * kk)), 1e-12)
    k = k * (1.0 + (a - 1.0) * ka_ref[...])
    alpha = -kk
    beta = kk * a
    if t_valid < c:
        live = row < t_valid
        logw = jnp.where(live, logw, 0.0)
        alpha = jnp.where(live, alpha, 0.0)
        beta = jnp.where(live, beta, 0.0)
        k = jnp.where(live, k, 0.0)

    ti = lax.broadcasted_iota(jnp.int32, (c, c), 0)
    tj = lax.broadcasted_iota(jnp.int32, (c, c), 1)
    tri_incl = ti >= tj
    tri_strict = ti > tj
    cum = _dot_tri(tri_incl, logw)
    cum_last = cum[c - 1:c]
    g_incl = jnp.exp(cum)
    g_prev = jnp.exp(cum - logw)
    g_inv = jnp.exp(-cum)
    g_rest = jnp.exp(cum_last - cum)
    g_all = jnp.exp(cum_last)

    r_bar = r * g_incl
    a_bar = alpha * g_prev
    b_bar = beta * g_inv
    k_bar = k * g_inv
    b_til = beta * g_rest
    k_til = k * g_rest

    eye = (ti == tj).astype(F32)
    base_shift = INV_BASE.bit_length() - 1
    blk_mask = _shr(ti, base_shift) == _shr(tj, base_shift)
    merge_masks = []
    sh = base_shift
    while (1 << sh) < c:
        merge_masks.append((_shr(ti, sh + 1) == _shr(tj, sh + 1)) & ((_shr(ti, sh) & 1) == 1) & ((_shr(tj, sh) & 1) == 0))
        sh += 1
    diag128 = pi == pj
    zeros_c = jnp.zeros((c, LANES), F32)

    pairs = range(n_pairs)
    heads = [(p, h) for p in pairs for h in range(2)]
    col = [slice(p * LANES, (p + 1) * LANES) for p in pairs]
    rb, ab, bb, kb, bt, kt, vp = ([x[:, col[p]] for p in pairs] for x in (r_bar, a_bar, b_bar, k_bar, b_til, k_til, v))
    pick = lambda x0, x1: jnp.where(first_half, x0, x1)
    gm = []
    for p in pairs:
        lhs = jnp.concatenate([pick(ab[p], 0.0), pick(0.0, ab[p]), pick(rb[p], 0.0), pick(0.0, rb[p])], axis=0)
        gm.append(_mm(lhs, jnp.concatenate([bb[p], kb[p]], axis=0), 3, NT))
    a_ab = [jnp.where(tri_strict, gm[p][h * c:(h + 1) * c, 0:c], 0.0) for p, h in heads]
    a_ak = [jnp.where(tri_strict, gm[p][h * c:(h + 1) * c, c:2 * c], 0.0) for p, h in heads]
    lr = [jnp.where(jnp.concatenate([tri_incl, tri_incl], axis=1), gm[p][(2 + h) * c:(3 + h) * c, :], 0.0) for p, h in heads]
    t_inv = _tri_inverse_all(a_ab, blk_mask, merge_masks, eye)
    akv = [_mm(a_ak[n], vp[p]) for n, (p, h) in enumerate(heads)]
    xh = [_mm(t_inv[n], jnp.concatenate([ab[p], akv[n]], axis=1)) for n, (p, h) in enumerate(heads)]
    a_new = [pick(xh[2 * p][:, 0:LANES], xh[2 * p + 1][:, 0:LANES]) for p in pairs]
    u0 = [pick(xh[2 * p][:, LANES:], xh[2 * p + 1][:, LANES:]) for p in pairs]
    zmat = [jnp.concatenate([jnp.concatenate([a_new[p], u0[p]], axis=1),
                             jnp.concatenate([zeros_c, vp[p]], axis=1)], axis=0).astype(BF16) for p in pairs]
    yh = [_dot(lr[n].astype(BF16), zmat[p]) for n, (p, h) in enumerate(heads)]
    mn = [_dot(jnp.concatenate([bt[p], kt[p]], axis=0).astype(BF16), zmat[p], TN) for p in pairs]
    o_pairs = []
    for p in pairs:
        r_new = rb[p] + pick(yh[2 * p][:, 0:LANES], yh[2 * p + 1][:, 0:LANES])
        o0 = pick(yh[2 * p][:, LANES:], yh[2 * p + 1][:, LANES:])
        m_mat = jnp.where(diag128, g_all[:, col[p]], 0.0) + jnp.where(head_block, mn[p][:, 0:LANES], 0.0)
        n0 = jnp.where(head_block, mn[p][:, LANES:], 0.0)
        op = _mm(jnp.concatenate([r_new, m_mat], axis=0), state_scr[p], 3)
        o_pairs.append(op[0:c] + o0)
        state_scr[p] = op[c:] + n0

    o = jnp.concatenate(o_pairs, axis=1)
    inv_n = 1.0 / RWKV_HEAD
    mean = head_sum(o) * inv_n
    cen = o - mean
    var = head_sum(cen * cen) * inv_n
    y = cen * lax.rsqrt(var + LNX_EPS) * lnw_ref[...] + lnb_ref[...]
    bonus = head_sum(r * k * rk_ref[...]) * v
    o_ref[0] = ((y + bonus) * _silu(g_ref[0])).astype(BF16)

    @pl.when(i == pl.num_programs(1) - 1)
    def _():
        pout_ref[0] = state_scr[...]


def _dot_tri(tri_incl, x):
    tri = tri_incl.astype(BF16)
    out = None
    for part in _split(x, 3):
        d = jnp.dot(tri, part, preferred_element_type=F32)
        out = d if out is None else out + d
    return out


def _rwkv(xc, gate, shift_prev, p0, lp, c, t_valid):
    b, t, _ = xc.shape
    vec = lambda n: pl.BlockSpec((1, n), lambda bi, i: (0, 0))
    n_pairs = N_RWKV_HEADS // 2
    return pl.pallas_call(
        functools.partial(_rwkv_kernel, c=c, t_valid=t_valid),
        grid=(b, t // c),
        in_specs=[
            pl.BlockSpec((1, c, D_SHIFT), lambda bi, i: (bi, i, 0)),
            pl.BlockSpec((1, c, D_RWKV), lambda bi, i: (bi, i, 0)),
            pl.BlockSpec((1, 1, D_SHIFT), lambda bi, i: (bi, 0, 0)),
            pl.BlockSpec((1, n_pairs, LANES, LANES), lambda bi, i: (bi, 0, 0, 0)),
            vec(D_SHIFT), vec(D_RWKV), vec(D_RWKV),
            pl.BlockSpec((2 * LORA, 2 * D_RWKV), lambda bi, i: (0, 0)),
            vec(D_RWKV), vec(D_RWKV), vec(D_RWKV), vec(D_RWKV), vec(D_RWKV),
        ],
        out_specs=[
            pl.BlockSpec((1, c, D_RWKV), lambda bi, i: (bi, i, 0)),
            pl.BlockSpec((1, n_pairs, LANES, LANES), lambda bi, i: (bi, 0, 0, 0)),
        ],
        out_shape=[jax.ShapeDtypeStruct((b, t, D_RWKV), BF16),
                   jax.ShapeDtypeStruct((b, n_pairs, LANES, LANES), F32)],
        scratch_shapes=[pltpu.VMEM((1, D_SHIFT), F32), pltpu.VMEM((n_pairs, LANES, LANES), F32)],
        compiler_params=_cparams(("arbitrary", "arbitrary")),
        name="rwkv",
    )(xc, gate, shift_prev.reshape(b, 1, D_SHIFT), p0,
      lp["mu"], lp["w0"], lp["a0"], lp["lora"], lp["k_k"], lp["k_a"], lp["r_k"], lp["lnx_w"], lp["lnx_b"])


def _state_to_pairs(s):
    b = s.shape[0]
    pt = jnp.swapaxes(s, -1, -2).reshape(b, N_RWKV_HEADS // 2, 2, RWKV_HEAD, RWKV_HEAD)
    z = jnp.zeros_like(pt[:, :, 0])
    top = jnp.concatenate([pt[:, :, 0], z], axis=-1)
    bot = jnp.concatenate([z, pt[:, :, 1]], axis=-1)
    return jnp.concatenate([top, bot], axis=-2)


def _pairs_to_state(pm):
    b = pm.shape[0]
    h0 = pm[:, :, 0:RWKV_HEAD, 0:RWKV_HEAD]
    h1 = pm[:, :, RWKV_HEAD:, RWKV_HEAD:]
    pt = jnp.stack([h0, h1], axis=2).reshape(b, N_RWKV_HEADS, RWKV_HEAD, RWKV_HEAD)
    return jnp.swapaxes(pt, -1, -2)


def _layer(h, lp, att, pool_hist, shift_prev, wkv0, kv_prefix, pos0, final_w, final, cfg):
    b, t, _ = h.shape
    pg, q, kv, ga, xc, gr = _inproj(h.reshape(b * t, D_MODEL), lp["norm_w"], lp["w_in"], cfg["tm_in"])
    shape3 = lambda x: x.reshape(b, t, x.shape[-1])
    pg, q, kv, ga, xc, gr = (shape3(x) for x in (pg, q, kv, ga, xc, gr))

    hist16 = jnp.concatenate([jnp.zeros((b, 1, D_POOL), F32), pool_hist], axis=1)
    mp = _pool(pg, hist16, lp["pool_w"], lp["pool_scale"], pos0, cfg["tt_pool"])
    ma = _attn(q, kv, kv_prefix, ga, att["bias"], _sink_rows(lp["sinks"], cfg["tq"]), cfg["tq"])

    c = RWKV_CHUNK
    if t % c:
        pad = c - t % c
        xc_in = jnp.pad(xc, ((0, 0), (0, pad), (0, 0)))
        gr_in = jnp.pad(gr, ((0, 0), (0, pad), (0, 0)))
        t_valid = t
    else:
        xc_in, gr_in, t_valid = xc, gr, c
    mr, p_new = _rwkv(xc_in, gr_in, shift_prev, _state_to_pairs(wkv0), lp, c, t_valid)
    mr = mr[:, :t]

    flat = lambda x: x.reshape(b * t, x.shape[-1])
    h_new = _outproj(flat(h), flat(mp), flat(ma), flat(mr), lp["w_out"], final_w, final, cfg["tm_out"])
    new_pool = pg[:, -POOL_HIST:, 0:D_POOL]
    kvf = kv[:, -WINDOW:] if t >= WINDOW else jnp.concatenate([kv_prefix[:, t:], kv], axis=1)
    new_k = kvf[:, :, 0:LANES].reshape(b, WINDOW, N_KV_HEADS, HEAD_DIM)
    new_v = kvf[:, :, LANES:].reshape(b, WINDOW, N_KV_HEADS, HEAD_DIM)
    new_shift = xc[:, -1]
    return h_new.reshape(b, t, D_MODEL), (new_pool, new_k, new_v, new_shift, _pairs_to_state(p_new))


def _group_cfg(b, t):
    m = b * t
    tq = min(t, ATT_TILE)
    return {"tm_in": min(m, 256), "tm_out": min(m, 512), "tt_pool": min(t, 512), "tq": tq}


def kernel(x_prompt, x_sample, state_pool, cache_swa_k, cache_swa_v, state_rwkv_shift, state_rwkv_wkv, norm_w, w_in, w_out, pool_w, pool_scale, attn_sinks, rel_bias_table, rwkv_mu, rwkv_w0, rwkv_w_up, rwkv_a0, rwkv_a_up, rwkv_k_k, rwkv_k_a, rwkv_r_k, rwkv_lnx_w, rwkv_lnx_b, final_norm_w):
    bp, tp, _ = x_prompt.shape
    bs, ts, _ = x_sample.shape
    cfg_p = _group_cfg(bp, tp)
    cfg_s = _group_cfg(bs, ts)
    att_p = {"bias": jnp.stack([_attn_bias(rel_bias_table, cfg_p["tq"], WINDOW + cfg_p["tq"], False),
                                _attn_bias(rel_bias_table, cfg_p["tq"], WINDOW + cfg_p["tq"], True)])}
    att_s = {"bias": _attn_bias(rel_bias_table, cfg_s["tq"], WINDOW + cfg_s["tq"], True)[None]}

    hp, hs = x_prompt, x_sample
    prompt_states, sample_states = [], []
    row = lambda x: x.astype(F32).reshape(1, -1)
    zero_lora = jnp.zeros((LORA, D_RWKV), F32)
    for l in range(DEPTH):
        lp = {
            "norm_w": norm_w[l], "w_in": w_in[l].astype(BF16), "w_out": w_out[l].astype(BF16),
            "pool_w": pool_w[l].astype(BF16), "pool_scale": pool_scale[l], "sinks": attn_sinks[l],
            "mu": row(rwkv_mu[l]), "w0": row(rwkv_w0[l]), "a0": row(rwkv_a0[l]),
            "lora": jnp.concatenate([jnp.concatenate([rwkv_w_up[l].astype(F32), zero_lora], axis=1),
                                     jnp.concatenate([zero_lora, rwkv_a_up[l].astype(F32)], axis=1)], axis=0),
            "k_k": row(rwkv_k_k[l]), "k_a": row(rwkv_k_a[l]), "r_k": row(rwkv_r_k[l]),
            "lnx_w": row(rwkv_lnx_w[l]), "lnx_b": row(rwkv_lnx_b[l]),
        }
        final = l == DEPTH - 1
        hp, sp = _layer(hp, lp, att_p, jnp.zeros((bp, POOL_HIST, D_POOL), F32), jnp.zeros((bp, D_SHIFT), F32),
                        jnp.zeros((bp, N_RWKV_HEADS, RWKV_HEAD, RWKV_HEAD), F32),
                        jnp.zeros((bp, WINDOW, 4 * HEAD_DIM), F32), 0, final_norm_w, final, cfg_p)
        prefix_s = jnp.concatenate([cache_swa_k[l].reshape(bs, WINDOW, LANES), cache_swa_v[l].reshape(bs, WINDOW, LANES)], axis=-1)
        hs, ss = _layer(hs, lp, att_s, state_pool[l], state_rwkv_shift[l], state_rwkv_wkv[l], prefix_s, PAST_LEN,
                        final_norm_w, final, cfg_s)
        prompt_states.append(sp)
        sample_states.append(ss)
    outs_p = [jnp.stack(x) for x in zip(*prompt_states)]
    outs_s = [jnp.stack(x) for x in zip(*sample_states)]
    return (hp, hs, *outs_p, *outs_s)
```

```python
import functools
import math

import jax
import jax.numpy as jnp
from jax import lax
from jax.experimental import pallas as pl
from jax.experimental.pallas import tpu as pltpu

F32 = jnp.float32
BF16 = jnp.bfloat16

D_MODEL = 2048
DEPTH = 4
PAST_LEN = 1024
CHUNK = 64
D_POOL = 512
POOL_WINDOWS = (2, 4, 8, 16)
POOL_GROUP = 128
POOL_HIST = 15
HEAD_DIM = 64
D_ATTN = 1024
N_Q_HEADS = 16
N_KV_HEADS = 2
WINDOW = 128
WIN_CHUNKS = 2
NUM_BUCKETS = 32
MAX_DISTANCE = 128
NEG = -1e30
D_RWKV = 512
RWKV_HEAD = 64
N_RWKV_HEADS = 8
LORA = 64
D_SHIFT = 3 * D_RWKV + 2 * LORA
NORM_EPS = 1e-5
LNX_EPS = 1e-5 * RWKV_HEAD
SEGMENTS = (2 * D_POOL, D_ATTN, 2 * N_KV_HEADS * HEAD_DIM, D_ATTN, D_SHIFT, D_RWKV)
D_IN = sum(SEGMENTS)

LANES = 128
ATT_TILE = 128
RWKV_CHUNK = 128
INV_BASE = 16
VMEM_LIMIT = 56 * 1024 * 1024


def _cparams(sem):
    return pltpu.CompilerParams(dimension_semantics=sem, vmem_limit_bytes=VMEM_LIMIT)


def _sigmoid(x):
    return 0.5 * jnp.tanh(0.5 * x) + 0.5


def _silu(x):
    return x * _sigmoid(x)


def _split(x, terms):
    parts = []
    rem = x
    for _ in range(terms):
        p = rem.astype(BF16)
        parts.append(p)
        rem = rem - p.astype(F32)
    return parts


def _dot(a, b, dims=None):
    if dims is None:
        return jnp.dot(a, b, preferred_element_type=F32)
    return lax.dot_general(a, b, (dims, ((), ())), preferred_element_type=F32)


def _mm(a, b, passes=1, dims=None):
    if passes == 1:
        return _dot(a.astype(BF16), b.astype(BF16), dims)
    ah, al = _split(a, 2)
    bh, bl = _split(b, 2)
    return _dot(ah, bh, dims) + (_dot(ah, bl, dims) + _dot(al, bh, dims))


def _mm_exact_rhs(a, b_bf16, terms=3):
    out = None
    for p in _split(a, terms):
        d = _dot(p, b_bf16)
        out = d if out is None else out + d
    return out


HEAD_SHIFT = RWKV_HEAD.bit_length() - 1


def _shr(x, bits):
    return lax.shift_right_logical(x, jnp.full_like(x, bits))


NT = ((1,), (1,))
TN = ((0,), (0,))


def _inproj_kernel(h_ref, nw_ref, w_ref, *o_refs):
    x = h_ref[...]
    ms = jnp.mean(x * x, axis=-1, keepdims=True)
    xn = (x * lax.rsqrt(ms + NORM_EPS) * nw_ref[...]).astype(BF16)
    off = 0
    for o_ref in o_refs:
        width = o_ref.shape[-1]
        o_ref[...] = jnp.dot(xn, w_ref[:, off:off + width], preferred_element_type=F32)
        off += width


def _inproj(h2d, norm_w, w_in_bf16, tm):
    m = h2d.shape[0]
    return pl.pallas_call(
        _inproj_kernel,
        grid=(m // tm,),
        in_specs=[
            pl.BlockSpec((tm, D_MODEL), lambda i: (i, 0)),
            pl.BlockSpec((1, D_MODEL), lambda i: (0, 0)),
            pl.BlockSpec(memory_space=pltpu.VMEM),
        ],
        out_specs=[pl.BlockSpec((tm, s), lambda i: (i, 0)) for s in SEGMENTS],
        out_shape=[jax.ShapeDtypeStruct((m, s), F32) for s in SEGMENTS],
        compiler_params=_cparams(("arbitrary",)),
        name="inproj",
    )(h2d, norm_w.reshape(1, D_MODEL), w_in_bf16)


def _outproj_kernel(h_ref, mp_ref, ma_ref, mr_ref, w_ref, fw_ref, o_ref, *, final):
    acc = jnp.dot(mp_ref[...], w_ref[0:D_POOL, :], preferred_element_type=F32)
    acc += jnp.dot(ma_ref[...], w_ref[D_POOL:D_POOL + D_ATTN, :], preferred_element_type=F32)
    acc += jnp.dot(mr_ref[...], w_ref[D_POOL + D_ATTN:, :], preferred_element_type=F32)
    hn = h_ref[...] + acc
    if final:
        ms = jnp.mean(hn * hn, axis=-1, keepdims=True)
        hn = hn * lax.rsqrt(ms + NORM_EPS) * fw_ref[...]
    o_ref[...] = hn


def _outproj(h2d, mp, ma, mr, w_out_bf16, final_w, final, tm):
    m = h2d.shape[0]
    row = lambda width: pl.BlockSpec((tm, width), lambda i: (i, 0))
    return pl.pallas_call(
        functools.partial(_outproj_kernel, final=final),
        grid=(m // tm,),
        in_specs=[row(D_MODEL), row(D_POOL), row(D_ATTN), row(D_RWKV),
                  pl.BlockSpec(memory_space=pltpu.VMEM),
                  pl.BlockSpec((1, D_MODEL), lambda i: (0, 0))],
        out_specs=row(D_MODEL),
        out_shape=jax.ShapeDtypeStruct((m, D_MODEL), F32),
        compiler_params=_cparams(("arbitrary",)),
        name="outproj",
    )(h2d, mp, ma, mr, w_out_bf16, final_w.reshape(1, D_MODEL))


def _pool_kernel(pg_ref, halo_ref, hist_ref, pw_ref, ps_ref, o_ref, *, tt, pos0):
    i = pl.program_id(1)
    p = pg_ref[0, :, 0:D_POOL]
    gate = pg_ref[0, :, D_POOL:2 * D_POOL]
    halo = jnp.where(i == 0, hist_ref[0], halo_ref[0, :, 0:D_POOL])
    ext = jnp.concatenate([halo, p], axis=0)
    pos = pos0 + i * tt + lax.broadcasted_iota(jnp.int32, (tt, 1), 0)
    outs = []
    for g, w in enumerate(POOL_WINDOWS):
        s = ext[:, g * POOL_GROUP:(g + 1) * POOL_GROUP]
        span = 1
        while span < w:
            n = s.shape[0]
            s = s[span:n] + s[0:n - span]
            span *= 2
        win = s[16 - (w - 1):16 - (w - 1) + tt]
        cnt = jnp.minimum(pos + 1, w).astype(F32)
        d = win / cnt - p[:, g * POOL_GROUP:(g + 1) * POOL_GROUP]
        outs.append(jnp.dot(d.astype(BF16), pw_ref[g], preferred_element_type=F32))
    y = jnp.concatenate(outs, axis=1) * ps_ref[...]
    o_ref[0] = (y * _silu(gate)).astype(BF16)


def _pool(pg, hist16, pool_w_bf16, pool_scale, pos0, tt):
    b, t, _ = pg.shape
    nh = tt // 16
    return pl.pallas_call(
        functools.partial(_pool_kernel, tt=tt, pos0=pos0),
        grid=(b, t // tt),
        in_specs=[
            pl.BlockSpec((1, tt, 2 * D_POOL), lambda bi, i: (bi, i, 0)),
            pl.BlockSpec((1, 16, 2 * D_POOL), lambda bi, i: (bi, jnp.maximum(i * nh - 1, 0), 0)),
            pl.BlockSpec((1, 16, D_POOL), lambda bi, i: (bi, 0, 0)),
            pl.BlockSpec((4, POOL_GROUP, POOL_GROUP), lambda bi, i: (0, 0, 0)),
            pl.BlockSpec((1, D_POOL), lambda bi, i: (0, 0)),
        ],
        out_specs=pl.BlockSpec((1, tt, D_POOL), lambda bi, i: (bi, i, 0)),
        out_shape=jax.ShapeDtypeStruct((b, t, D_POOL), BF16),
        compiler_params=_cparams(("arbitrary", "arbitrary")),
        name="pool",
    )(pg, pg, hist16, pool_w_bf16, pool_scale.reshape(1, D_POOL))


def _t5_bucket(rel):
    nb = NUM_BUCKETS // 2
    max_exact = nb // 2
    ret = jnp.where(rel > 0, nb, 0)
    n = jnp.abs(rel)
    nf = jnp.maximum(n, 1).astype(F32)
    large = max_exact + (jnp.log(nf / max_exact) / math.log(MAX_DISTANCE / max_exact) * (nb - max_exact)).astype(jnp.int32)
    large = jnp.minimum(large, nb - 1)
    return ret + jnp.where(n < max_exact, n, large)


def _bias_kernel(bucket_ref, tab_ref, o_ref):
    bucket = bucket_ref[...]
    tab = tab_ref[...]
    out = jnp.full(bucket.shape, NEG, F32)
    for b in range(NUM_BUCKETS):
        out = jnp.where(bucket == b, tab[:, b:b + 1], out)
    o_ref[...] = out


def _attn_bias(table, tq, nk, prefix_valid):
    qi = jnp.arange(tq)
    kj = jnp.arange(nk)
    rel = kj[None, :] - WINDOW - qi[:, None]
    bucket = _t5_bucket(rel)
    qc = qi // CHUNK
    kc = (kj - WINDOW) // CHUNK
    valid = (kc[None, :] <= qc[:, None]) & (kc[None, :] >= qc[:, None] - WIN_CHUNKS)
    valid &= (kj < WINDOW + tq)[None, :]
    if not prefix_valid:
        valid &= (kj >= WINDOW)[None, :]
    bucket = jnp.where(valid, bucket, -1).astype(jnp.int32)
    heads = jnp.array([[[8 * g + 2 * p + par for p in range(4)] for par in range(2)] for g in range(2)]).reshape(-1)
    rows = heads.shape[0] * tq
    bucket_rows = jnp.broadcast_to(bucket[None], (heads.shape[0], tq, nk)).reshape(rows, nk)
    tab_rows = jnp.broadcast_to(table.astype(F32).T[heads][:, None, :], (heads.shape[0], tq, NUM_BUCKETS)).reshape(rows, NUM_BUCKETS)
    tr = 4 * tq
    return pl.pallas_call(
        _bias_kernel,
        grid=(rows // tr,),
        in_specs=[pl.BlockSpec((tr, nk), lambda i: (i, 0)), pl.BlockSpec((tr, NUM_BUCKETS), lambda i: (i, 0))],
        out_specs=pl.BlockSpec((tr, nk), lambda i: (i, 0)),
        out_shape=jax.ShapeDtypeStruct((rows, nk), F32),
        name="attn_bias",
    )(bucket_rows, tab_rows)


def _sink_rows(sinks, tq):
    heads = jnp.array([[[8 * g + 2 * p + par for p in range(4)] for par in range(2)] for g in range(2)]).reshape(-1)
    return jnp.broadcast_to(sinks.astype(F32)[heads][:, None, None], (heads.shape[0], tq, LANES)).reshape(-1, LANES)


def _attn_kernel(*refs, tq, has_prev):
    if has_prev:
        q_ref, kvc_ref, kvp_ref, pre_ref, g_ref, bias_ref, sink_ref, o_ref = refs
    else:
        q_ref, kvc_ref, pre_ref, g_ref, bias_ref, sink_ref, o_ref = refs
    i = pl.program_id(1)
    if has_prev:
        kvp = jnp.where(i == 0, pre_ref[0], kvp_ref[0])
    else:
        kvp = pre_ref[0]
    kv = jnp.concatenate([kvp, kvc_ref[0]], axis=0)
    nk = kv.shape[0]
    k = kv[:, 0:LANES] * (HEAD_DIM ** -0.5)
    v = kv[:, LANES:2 * LANES]
    low = lax.broadcasted_iota(jnp.int32, (nk, LANES), 1) < HEAD_DIM
    k_sw = pltpu.roll(k, HEAD_DIM, axis=1)
    v_sw = pltpu.roll(v, HEAD_DIM, axis=1)
    ones = jnp.ones_like(v)

    def place(x, x_sw, g, par):
        src = x if g == par else x_sw
        return jnp.where(low, src, 0.0) if par == 0 else jnp.where(low, 0.0, src)

    blocks = [(g, par) for g in range(N_KV_HEADS) for par in range(2)]
    kx = [place(k, k_sw, g, par).astype(BF16) for g, par in blocks]
    vx = [jnp.concatenate([place(v, v_sw, g, par), ones], axis=1).astype(BF16) for g, par in blocks]
    qs = [jnp.concatenate([q_ref[0, :, (4 * g + p) * LANES:(4 * g + p + 1) * LANES] for p in range(4)], axis=0).astype(BF16)
          for g in range(N_KV_HEADS)]
    rows = [slice(n * 4 * tq, (n + 1) * 4 * tq) for n in range(len(blocks))]
    s = [_dot(qs[g], kx[n], NT) + bias_ref[rows[n], :] for n, (g, par) in enumerate(blocks)]
    sink = [sink_ref[rows[n], :] for n in range(len(blocks))]
    m = [jnp.maximum(jnp.broadcast_to(jnp.max(s[n], axis=-1, keepdims=True), (4 * tq, LANES)), sink[n])
         for n in range(len(blocks))]
    widen = lambda x: jnp.concatenate([x] * (nk // LANES) + ([x[:, 0:nk % LANES]] if nk % LANES else []), axis=1)
    e = [jnp.exp(s[n] - widen(m[n])).astype(BF16) for n in range(len(blocks))]
    pv = [_dot(e[n], vx[n]) for n in range(len(blocks))]
    out = [pv[n][:, 0:LANES] / (pv[n][:, LANES:] + jnp.exp(sink[n] - m[n])) for n in range(len(blocks))]
    for g in range(N_KV_HEADS):
        acc = out[2 * g] + out[2 * g + 1]
        for p in range(4):
            cols = slice((4 * g + p) * LANES, (4 * g + p + 1) * LANES)
            o_ref[0, :, cols] = (acc[p * tq:(p + 1) * tq] * _silu(g_ref[0, :, cols])).astype(BF16)


def _attn(q, kv, prefix, gate, bias, sink_rows, tq):
    b, t, _ = q.shape
    n_tiles = t // tq
    has_prev = n_tiles > 1
    nk = WINDOW + tq
    rows = bias.shape[-2]
    in_specs = [
        pl.BlockSpec((1, tq, D_ATTN), lambda bi, i: (bi, i, 0)),
        pl.BlockSpec((1, tq, 4 * HEAD_DIM), lambda bi, i: (bi, i, 0)),
    ]
    args = [q, kv]
    if has_prev:
        in_specs.append(pl.BlockSpec((1, WINDOW, 4 * HEAD_DIM), lambda bi, i: (bi, jnp.maximum(i - 1, 0), 0)))
        args.append(kv)
    in_specs += [
        pl.BlockSpec((1, WINDOW, 4 * HEAD_DIM), lambda bi, i: (bi, 0, 0)),
        pl.BlockSpec((1, tq, D_ATTN), lambda bi, i: (bi, i, 0)),
        pl.BlockSpec((None, rows, nk), lambda bi, i: (jnp.minimum(i, bias.shape[0] - 1), 0, 0)),
        pl.BlockSpec((rows, LANES), lambda bi, i: (0, 0)),
    ]
    args += [prefix, gate, bias, sink_rows]
    return pl.pallas_call(
        functools.partial(_attn_kernel, tq=tq, has_prev=has_prev),
        grid=(b, n_tiles),
        in_specs=in_specs,
        out_specs=pl.BlockSpec((1, tq, D_ATTN), lambda bi, i: (bi, i, 0)),
        out_shape=jax.ShapeDtypeStruct((b, t, D_ATTN), BF16),
        compiler_params=_cparams(("arbitrary", "arbitrary")),
        name="attn",
    )(*args)


def _mm3(a, b):
    return _dot(a[0], b[0]) + (_dot(a[0], b[1]) + _dot(a[1], b[0]))


def _tri_inverse_all(mats, blk_mask, merge_masks, eye):
    n = len(mats)
    c = eye.shape[0]
    power = [jnp.where(blk_mask, a, 0.0) for a in mats]
    t = [eye + d for d in power]
    pb = [x.astype(BF16) for x in power]
    power = [_dot(pb[i], pb[i]) for i in range(n)]
    span = 2
    while 2 * span < INV_BASE:
        pb = [x.astype(BF16) for x in power]
        both = [_dot(pb[i], jnp.concatenate([pb[i], t[i].astype(BF16)], axis=1)) for i in range(n)]
        power = [x[:, 0:c] for x in both]
        t = [t[i] + both[i][:, c:] for i in range(n)]
        span *= 2
    t = [t[i] + _dot(power[i].astype(BF16), t[i].astype(BF16)) for i in range(n)]
    for mask in merge_masks:
        a21 = [jnp.where(mask, a, 0.0).astype(BF16) for a in mats]
        tb = [x.astype(BF16) for x in t]
        low = [_dot(tb[i], a21[i]).astype(BF16) for i in range(n)]
        t = [t[i] + _dot(low[i], tb[i]) for i in range(n)]
    ts = [_split(x, 2) for x in t]
    resid = [(eye - t[i]) + _mm3(_split(mats[i], 2), ts[i]) for i in range(n)]
    return [t[i] + _dot(ts[i][0], resid[i].astype(BF16)) for i in range(n)]


def _rwkv_kernel(xc_ref, g_ref, shift_ref, p0_ref, mu_ref, w0_ref, a0_ref, lora_ref, kk_ref, ka_ref, rk_ref,
                 lnw_ref, lnb_ref, o_ref, pout_ref, prev_scr, state_scr, *, c, t_valid):
    i = pl.program_id(1)
    n_pairs = N_RWKV_HEADS // 2

    @pl.when(i == 0)
    def _():
        prev_scr[...] = shift_ref[0]
        state_scr[...] = p0_ref[0]

    xc = xc_ref[0]
    row = lax.broadcasted_iota(jnp.int32, (c, 1), 0)
    prev = jnp.where(row == 0, prev_scr[...], pltpu.roll(xc, 1, axis=0))
    prev_scr[...] = xc[t_valid - 1:t_valid] if t_valid < c else xc[c - 1:c]
    xs = xc + (prev - xc) * mu_ref[...]
    r = xs[:, 0:D_RWKV]
    k = xs[:, D_RWKV:2 * D_RWKV]
    v = xs[:, 2 * D_RWKV:3 * D_RWKV]
    lo = xs[:, 3 * D_RWKV:3 * D_RWKV + 2 * LORA]
    lane = lax.broadcasted_iota(jnp.int32, (c, LANES), 1)
    first_half = lane < RWKV_HEAD
    lora_in = jnp.where(first_half, jnp.tanh(lo), lo)
    lora = _mm(lora_in, lora_ref[...], 3)
    z = -(w0_ref[...] + lora[:, 0:D_RWKV])
    softplus = jnp.maximum(z, 0.0) + jnp.log(1.0 + jnp.exp(-jnp.abs(z)))
    logw = -jnp.exp(-softplus - 0.5)
    a = _sigmoid(a0_ref[...] + lora[:, D_RWKV:2 * D_RWKV])

    pi = lax.broadcasted_iota(jnp.int32, (LANES, LANES), 0)
    pj = lax.broadcasted_iota(jnp.int32, (LANES, LANES), 1)
    head_block = _shr(pi, HEAD_SHIFT) == _shr(pj, HEAD_SHIFT)
    head_ones = head_block.astype(BF16)

    def head_sum(x):
        stacked = jnp.concatenate([x[:, p * LANES:(p + 1) * LANES] for p in range(n_pairs)], axis=0)
        s = _dot(stacked.astype(BF16), head_ones)
        return jnp.concatenate([s[p * c:(p + 1) * c] for p in range(n_pairs)], axis=1)

    kk = k * kk_ref[...]
    kk = kk * jnp.minimum(lax.rsqrt(head_sum(kk * kk)), 1e12)
    k = k * (1.0 + (a - 1.0) * ka_ref[...])
    alpha = -kk
    beta = kk * a
    if t_valid < c:
        live = row < t_valid
        logw = jnp.where(live, logw, 0.0)
        alpha = jnp.where(live, alpha, 0.0)
        beta = jnp.where(live, beta, 0.0)
        k = jnp.where(live, k, 0.0)

    ti = lax.broadcasted_iota(jnp.int32, (c, c), 0)
    tj = lax.broadcasted_iota(jnp.int32, (c, c), 1)
    tri_incl = ti >= tj
    tri_strict = ti > tj
    cum = _dot_tri(tri_incl, logw)
    cum_last = cum[c - 1:c]
    g_incl = jnp.exp(cum)
    g_prev = jnp.exp(cum - logw)
    g_inv = jnp.exp(-cum)
    g_rest = jnp.exp(cum_last - cum)
    g_all = jnp.exp(cum_last)

    r_bar = r * g_incl
    a_bar = alpha * g_prev
    b_bar = beta * g_inv
    k_bar = k * g_inv
    b_til = beta * g_rest
    k_til = k * g_rest

    eye = (ti == tj).astype(F32)
    base_shift = INV_BASE.bit_length() - 1
    blk_mask = _shr(ti, base_shift) == _shr(tj, base_shift)
    merge_masks = []
    sh = base_shift
    while (1 << sh) < c:
        merge_masks.append((_shr(ti, sh + 1) == _shr(tj, sh + 1)) & ((_shr(ti, sh) & 1) == 1) & ((_shr(tj, sh) & 1) == 0))
        sh += 1
    diag128 = pi == pj
    zeros_c = jnp.zeros((c, LANES), F32)

    pairs = range(n_pairs)
    heads = [(p, h) for p in pairs for h in range(2)]
    col = [slice(p * LANES, (p + 1) * LANES) for p in pairs]
    rb, ab, bb, kb, bt, kt, vp = ([x[:, col[p]] for p in pairs] for x in (r_bar, a_bar, b_bar, k_bar, b_til, k_til, v))
    pick = lambda x0, x1: jnp.where(first_half, x0, x1)
    bk = [jnp.concatenate([bb[p], kb[p]], axis=0) for p in pairs]
    ga = [_mm(jnp.concatenate([pick(ab[p], 0.0), pick(0.0, ab[p])], axis=0), bk[p], 3, NT) for p in pairs]
    gr = [_mm(jnp.concatenate([pick(rb[p], 0.0), pick(0.0, rb[p])], axis=0), bk[p], 1, NT) for p in pairs]
    a_ab = [jnp.where(tri_strict, ga[p][h * c:(h + 1) * c, 0:c], 0.0) for p, h in heads]
    a_ak = [jnp.where(tri_strict, ga[p][h * c:(h + 1) * c, c:2 * c], 0.0) for p, h in heads]
    lr = [jnp.where(jnp.concatenate([tri_incl, tri_incl], axis=1), gr[p][h * c:(h + 1) * c, :], 0.0) for p, h in heads]
    t_inv = _tri_inverse_all(a_ab, blk_mask, merge_masks, eye)
    akv = [_mm(a_ak[n], vp[p]) for n, (p, h) in enumerate(heads)]
    xh = [_mm(t_inv[n], jnp.concatenate([ab[p], akv[n]], axis=1)) for n, (p, h) in enumerate(heads)]
    a_new = [pick(xh[2 * p][:, 0:LANES], xh[2 * p + 1][:, 0:LANES]) for p in pairs]
    u0 = [pick(xh[2 * p][:, LANES:], xh[2 * p + 1][:, LANES:]) for p in pairs]
    zmat = [jnp.concatenate([jnp.concatenate([a_new[p], u0[p]], axis=1),
                             jnp.concatenate([zeros_c, vp[p]], axis=1)], axis=0).astype(BF16) for p in pairs]
    yh = [_dot(lr[n].astype(BF16), zmat[p]) for n, (p, h) in enumerate(heads)]
    mn = [_dot(jnp.concatenate([bt[p], kt[p]], axis=0).astype(BF16), zmat[p], TN) for p in pairs]
    o_pairs = []
    for p in pairs:
        r_new = rb[p] + pick(yh[2 * p][:, 0:LANES], yh[2 * p + 1][:, 0:LANES])
        o0 = pick(yh[2 * p][:, LANES:], yh[2 * p + 1][:, LANES:])
        m_mat = jnp.where(diag128, g_all[:, col[p]], 0.0) + jnp.where(head_block, mn[p][:, 0:LANES], 0.0)
        n0 = jnp.where(head_block, mn[p][:, LANES:], 0.0)
        state = _split(state_scr[p], 2)
        o_pairs.append(_dot(r_new.astype(BF16), state[0]) + o0)
        state_scr[p] = _mm3(_split(m_mat, 2), state) + n0

    o = jnp.concatenate(o_pairs, axis=1)
    inv_n = 1.0 / RWKV_HEAD
    mean = head_sum(o) * inv_n
    cen = o - mean
    var = head_sum(cen * cen) * inv_n
    y = cen * lax.rsqrt(var + LNX_EPS) * lnw_ref[...] + lnb_ref[...]
    bonus = head_sum(r * k * rk_ref[...]) * v
    o_ref[0] = ((y + bonus) * _silu(g_ref[0])).astype(BF16)

    @pl.when(i == pl.num_programs(1) - 1)
    def _():
        pout_ref[0] = state_scr[...]


def _dot_tri(tri_incl, x):
    tri = tri_incl.astype(BF16)
    out = None
    for part in _split(x, 3):
        d = jnp.dot(tri, part, preferred_element_type=F32)
        out = d if out is None else out + d
    return out


def _rwkv(xc, gate, shift_prev, p0, lp, c, t_valid):
    b, t, _ = xc.shape
    vec = lambda n: pl.BlockSpec((1, n), lambda bi, i: (0, 0))
    n_pairs = N_RWKV_HEADS // 2
    return pl.pallas_call(
        functools.partial(_rwkv_kernel, c=c, t_valid=t_valid),
        grid=(b, t // c),
        in_specs=[
            pl.BlockSpec((1, c, D_SHIFT), lambda bi, i: (bi, i, 0)),
            pl.BlockSpec((1, c, D_RWKV), lambda bi, i: (bi, i, 0)),
            pl.BlockSpec((1, 1, D_SHIFT), lambda bi, i: (bi, 0, 0)),
            pl.BlockSpec((1, n_pairs, LANES, LANES), lambda bi, i: (bi, 0, 0, 0)),
            vec(D_SHIFT), vec(D_RWKV), vec(D_RWKV),
            pl.BlockSpec((2 * LORA, 2 * D_RWKV), lambda bi, i: (0, 0)),
            vec(D_RWKV), vec(D_RWKV), vec(D_RWKV), vec(D_RWKV), vec(D_RWKV),
        ],
        out_specs=[
            pl.BlockSpec((1, c, D_RWKV), lambda bi, i: (bi, i, 0)),
            pl.BlockSpec((1, n_pairs, LANES, LANES), lambda bi, i: (bi, 0, 0, 0)),
        ],
        out_shape=[jax.ShapeDtypeStruct((b, t, D_RWKV), BF16),
                   jax.ShapeDtypeStruct((b, n_pairs, LANES, LANES), F32)],
        scratch_shapes=[pltpu.VMEM((1, D_SHIFT), F32), pltpu.VMEM((n_pairs, LANES, LANES), F32)],
        compiler_params=_cparams(("arbitrary", "arbitrary")),
        name="rwkv",
    )(xc, gate, shift_prev.reshape(b, 1, D_SHIFT), p0,
      lp["mu"], lp["w0"], lp["a0"], lp["lora"], lp["k_k"], lp["k_a"], lp["r_k"], lp["lnx_w"], lp["lnx_b"])


def _state_to_pairs(s):
    b = s.shape[0]
    pt = jnp.swapaxes(s, -1, -2).reshape(b, N_RWKV_HEADS // 2, 2, RWKV_HEAD, RWKV_HEAD)
    z = jnp.zeros_like(pt[:, :, 0])
    top = jnp.concatenate([pt[:, :, 0], z], axis=-1)
    bot = jnp.concatenate([z, pt[:, :, 1]], axis=-1)
    return jnp.concatenate([top, bot], axis=-2)


def _pairs_to_state(pm):
    b = pm.shape[0]
    h0 = pm[:, :, 0:RWKV_HEAD, 0:RWKV_HEAD]
    h1 = pm[:, :, RWKV_HEAD:, RWKV_HEAD:]
    pt = jnp.stack([h0, h1], axis=2).reshape(b, N_RWKV_HEADS, RWKV_HEAD, RWKV_HEAD)
    return jnp.swapaxes(pt, -1, -2)


def _layer(h, lp, att, pool_hist, shift_prev, wkv0, kv_prefix, pos0, final_w, final, cfg):
    b, t, _ = h.shape
    pg, q, kv, ga, xc, gr = _inproj(h.reshape(b * t, D_MODEL), lp["norm_w"], lp["w_in"], cfg["tm_in"])
    shape3 = lambda x: x.reshape(b, t, x.shape[-1])
    pg, q, kv, ga, xc, gr = (shape3(x) for x in (pg, q, kv, ga, xc, gr))

    hist16 = jnp.concatenate([jnp.zeros((b, 1, D_POOL), F32), pool_hist], axis=1)
    mp = _pool(pg, hist16, lp["pool_w"], lp["pool_scale"], pos0, cfg["tt_pool"])
    ma = _attn(q, kv, kv_prefix, ga, att["bias"], _sink_rows(lp["sinks"], cfg["tq"]), cfg["tq"])

    c = RWKV_CHUNK
    if t % c:
        pad = c - t % c
        xc_in = jnp.pad(xc, ((0, 0), (0, pad), (0, 0)))
        gr_in = jnp.pad(gr, ((0, 0), (0, pad), (0, 0)))
        t_valid = t
    else:
        xc_in, gr_in, t_valid = xc, gr, c
    mr, p_new = _rwkv(xc_in, gr_in, shift_prev, _state_to_pairs(wkv0), lp, c, t_valid)
    mr = mr[:, :t]

    flat = lambda x: x.reshape(b * t, x.shape[-1])
    h_new = _outproj(flat(h), flat(mp), flat(ma), flat(mr), lp["w_out"], final_w, final, cfg["tm_out"])
    new_pool = pg[:, -POOL_HIST:, 0:D_POOL]
    kvf = kv[:, -WINDOW:] if t >= WINDOW else jnp.concatenate([kv_prefix[:, t:], kv], axis=1)
    new_k = kvf[:, :, 0:LANES].reshape(b, WINDOW, N_KV_HEADS, HEAD_DIM)
    new_v = kvf[:, :, LANES:].reshape(b, WINDOW, N_KV_HEADS, HEAD_DIM)
    new_shift = xc[:, -1]
    return h_new.reshape(b, t, D_MODEL), (new_pool, new_k, new_v, new_shift, _pairs_to_state(p_new))


def _group_cfg(b, t):
    m = b * t
    tq = min(t, ATT_TILE)
    return {"tm_in": min(m, 256), "tm_out": min(m, 512), "tt_pool": min(t, 512), "tq": tq}


def kernel(x_prompt, x_sample, state_pool, cache_swa_k, cache_swa_v, state_rwkv_shift, state_rwkv_wkv, norm_w, w_in, w_out, pool_w, pool_scale, attn_sinks, rel_bias_table, rwkv_mu, rwkv_w0, rwkv_w_up, rwkv_a0, rwkv_a_up, rwkv_k_k, rwkv_k_a, rwkv_r_k, rwkv_lnx_w, rwkv_lnx_b, final_norm_w):
    bp, tp, _ = x_prompt.shape
    bs, ts, _ = x_sample.shape
    cfg_p = _group_cfg(bp, tp)
    cfg_s = _group_cfg(bs, ts)
    att_p = {"bias": jnp.stack([_attn_bias(rel_bias_table, cfg_p["tq"], WINDOW + cfg_p["tq"], False),
                                _attn_bias(rel_bias_table, cfg_p["tq"], WINDOW + cfg_p["tq"], True)])}
    att_s = {"bias": _attn_bias(rel_bias_table, cfg_s["tq"], WINDOW + cfg_s["tq"], True)[None]}

    hp, hs = x_prompt, x_sample
    prompt_states, sample_states = [], []
    row = lambda x: x.astype(F32).reshape(1, -1)
    zero_lora = jnp.zeros((LORA, D_RWKV), F32)
    for l in range(DEPTH):
        lp = {
            "norm_w": norm_w[l], "w_in": w_in[l].astype(BF16), "w_out": w_out[l].astype(BF16),
            "pool_w": pool_w[l].astype(BF16), "pool_scale": pool_scale[l], "sinks": attn_sinks[l],
            "mu": row(rwkv_mu[l]), "w0": row(rwkv_w0[l]), "a0": row(rwkv_a0[l]),
            "lora": jnp.concatenate([jnp.concatenate([rwkv_w_up[l].astype(F32), zero_lora], axis=1),
                                     jnp.concatenate([zero_lora, rwkv_a_up[l].astype(F32)], axis=1)], axis=0),
            "k_k": row(rwkv_k_k[l]), "k_a": row(rwkv_k_a[l]), "r_k": row(rwkv_r_k[l]),
            "lnx_w": row(rwkv_lnx_w[l]), "lnx_b": row(rwkv_lnx_b[l]),
        }
        final = l == DEPTH - 1
        hp, sp = _layer(hp, lp, att_p, jnp.zeros((bp, POOL_HIST, D_POOL), F32), jnp.zeros((bp, D_SHIFT), F32),
                        jnp.zeros((bp, N_RWKV_HEADS, RWKV_HEAD, RWKV_HEAD), F32),
                        jnp.zeros((bp, WINDOW, 4 * HEAD_DIM), F32), 0, final_norm_w, final, cfg_p)
        prefix_s = jnp.concatenate([cache_swa_k[l].reshape(bs, WINDOW, LANES), cache_swa_v[l].reshape(bs, WINDOW, LANES)], axis=-1)
        hs, ss = _layer(hs, lp, att_s, state_pool[l], state_rwkv_shift[l], state_rwkv_wkv[l], prefix_s, PAST_LEN,
                        final_norm_w, final, cfg_s)
        prompt_states.append(sp)
        sample_states.append(ss)
    outs_p = [jnp.stack(x) for x in zip(*prompt_states)]
    outs_s = [jnp.stack(x) for x in zip(*sample_states)]
    return (hp, hs, *outs_p, *outs_s)
```

```python
import functools
import math

import jax
import jax.numpy as jnp
from jax import lax
from jax.experimental import pallas as pl
from jax.experimental.pallas import tpu as pltpu

F32 = jnp.float32
BF16 = jnp.bfloat16

D_MODEL = 2048
DEPTH = 4
PAST_LEN = 1024
CHUNK = 64
D_POOL = 512
POOL_WINDOWS = (2, 4, 8, 16)
POOL_GROUP = 128
POOL_HIST = 15
HEAD_DIM = 64
D_ATTN = 1024
N_Q_HEADS = 16
N_KV_HEADS = 2
WINDOW = 128
WIN_CHUNKS = 2
NUM_BUCKETS = 32
MAX_DISTANCE = 128
NEG = -1e30
D_RWKV = 512
RWKV_HEAD = 64
N_RWKV_HEADS = 8
LORA = 64
D_SHIFT = 3 * D_RWKV + 2 * LORA
NORM_EPS = 1e-5
LNX_EPS = 1e-5 * RWKV_HEAD
SEGMENTS = (2 * D_POOL, D_ATTN, 2 * N_KV_HEADS * HEAD_DIM, D_ATTN, D_SHIFT, D_RWKV)
D_IN = sum(SEGMENTS)

LANES = 128
ATT_TILE = 128
RWKV_CHUNK = 128
INV_BASE = 16
VMEM_LIMIT = 56 * 1024 * 1024


def _cparams(sem):
    return pltpu.CompilerParams(dimension_semantics=sem, vmem_limit_bytes=VMEM_LIMIT)


def _sigmoid(x):
    return 0.5 * jnp.tanh(0.5 * x) + 0.5


def _silu(x):
    return x * _sigmoid(x)


def _split(x, terms):
    parts = []
    rem = x
    for _ in range(terms):
        p = rem.astype(BF16)
        parts.append(p)
        rem = rem - p.astype(F32)
    return parts


def _dot(a, b, dims=None):
    if dims is None:
        return jnp.dot(a, b, preferred_element_type=F32)
    return lax.dot_general(a, b, (dims, ((), ())), preferred_element_type=F32)


def _mm(a, b, passes=1, dims=None):
    if passes == 1:
        return _dot(a.astype(BF16), b.astype(BF16), dims)
    ah, al = _split(a, 2)
    bh, bl = _split(b, 2)
    return _dot(ah, bh, dims) + (_dot(ah, bl, dims) + _dot(al, bh, dims))


def _mm_exact_rhs(a, b_bf16, terms=3):
    out = None
    for p in _split(a, terms):
        d = _dot(p, b_bf16)
        out = d if out is None else out + d
    return out


HEAD_SHIFT = RWKV_HEAD.bit_length() - 1


def _shr(x, bits):
    return lax.shift_right_logical(x, jnp.full_like(x, bits))


NT = ((1,), (1,))
TN = ((0,), (0,))


SEG_OFFSETS = tuple(sum(SEGMENTS[:n]) for n in range(len(SEGMENTS)))
FUSE_COLS = 512


class _ChunkView:
    def __init__(self, x, c):
        self.x, self.c = x, c

    def __getitem__(self, idx):
        q, _, cols = idx
        return self.x[q * self.c:(q + 1) * self.c, cols]


class _PrevRowView:
    def __init__(self, x, first, c):
        self.x, self.first, self.c = x, first, c

    def __getitem__(self, idx):
        q, _, cols = idx
        return self.first[:, cols] if q == 0 else self.x[q * self.c - 1:q * self.c, cols]


def _inproj_kernel(h_ref, nw_ref, w_ref, *refs, seq_tiles, pos0, c):
    x = h_ref[...]
    ms = jnp.mean(x * x, axis=-1, keepdims=True)
    xn = (x * lax.rsqrt(ms + NORM_EPS) * nw_ref[...]).astype(BF16)
    tm = x.shape[0]
    fused = seq_tiles > 0
    if fused:
        hist_ref, pw_ref, ps_ref, shift_ref, mu_ref, w0_ref, a0_ref, lora_ref, kk_ref, ka_ref, rk_ref, *refs = refs
        mp_ref, st_ref, gall_ref, halo_scr, prevrow_scr = refs[len(SEGMENTS):]
        it = lax.rem(pl.program_id(0), seq_tiles)
    o_refs = refs[:len(SEGMENTS)]

    def project(n, lo=0, width=None):
        width = SEGMENTS[n] if width is None else width
        seg = jnp.dot(xn, w_ref[:, SEG_OFFSETS[n] + lo:SEG_OFFSETS[n] + lo + width], preferred_element_type=F32)
        o_refs[n][:, lo:lo + width] = seg
        return seg

    if not fused:
        for n in range(len(SEGMENTS)):
            project(n)
        return

    xc = project(4)
    first = jnp.where(it == 0, shift_ref[0], prevrow_scr[...])
    prevrow_scr[...] = xc[tm - 1:tm]
    pi = lax.broadcasted_iota(jnp.int32, (LANES, LANES), 0)
    pj = lax.broadcasted_iota(jnp.int32, (LANES, LANES), 1)
    head_ones = (_shr(pi, HEAD_SHIFT) == _shr(pj, HEAD_SHIFT)).astype(BF16)

    def emit(q, s, p, x):
        if s == N_STAGE:
            gall_ref[q, :, p * LANES:(p + 1) * LANES] = jnp.broadcast_to(x, (8, LANES))
        else:
            st_ref[q * c:(q + 1) * c, s * D_RWKV + p * LANES:s * D_RWKV + (p + 1) * LANES] = x

    def mixer_slices():
        for q in range(tm // c):
            yield from _rwkv_elementwise(q, _ChunkView(xc, c), _PrevRowView(xc, first, c), mu_ref, w0_ref, a0_ref,
                                         lora_ref, kk_ref, ka_ref, rk_ref, head_ones, functools.partial(emit, q), c, c)

    slices = mixer_slices()
    pg_parts = []
    for n in (0, 1, 2, 3, 5):
        for lo in range(0, SEGMENTS[n], FUSE_COLS):
            part = project(n, lo, min(FUSE_COLS, SEGMENTS[n] - lo))
            if n == 0:
                pg_parts.append(part)
            next(slices, None)
        if n == 0:
            pg = jnp.concatenate(pg_parts, axis=1)
            p = pg[:, 0:D_POOL]
            halo = jnp.where(it == 0, hist_ref[0], halo_scr[...])
            halo_scr[...] = p[tm - 16:tm]
            mp_ref[...] = _pool_math(p, pg[:, D_POOL:2 * D_POOL], halo, pos0 + it * tm, pw_ref, ps_ref)
    for _ in slices:
        pass


def _inproj(h2d, norm_w, w_in_bf16, tm, fused=None):
    m = h2d.shape[0]
    c = RWKV_CHUNK
    in_specs = [
        pl.BlockSpec((tm, D_MODEL), lambda i: (i, 0)),
        pl.BlockSpec((1, D_MODEL), lambda i: (0, 0)),
        pl.BlockSpec(memory_space=pltpu.VMEM),
    ]
    out_specs = [pl.BlockSpec((tm, s), lambda i: (i, 0)) for s in SEGMENTS]
    out_shape = [jax.ShapeDtypeStruct((m, s), F32) for s in SEGMENTS]
    args = [h2d, norm_w.reshape(1, D_MODEL), w_in_bf16]
    scratch = []
    seq_tiles = pos0 = 0
    if fused is not None:
        seq_tiles, pos0, lp = fused["t"] // tm, fused["pos0"], fused["lp"]
        b = m // fused["t"]
        vec = lambda n: pl.BlockSpec((1, n), lambda i: (0, 0))
        in_specs += [pl.BlockSpec((1, 16, D_POOL), lambda i: (i // seq_tiles, 0, 0)),
                     pl.BlockSpec((4, POOL_GROUP, POOL_GROUP), lambda i: (0, 0, 0)),
                     vec(D_POOL),
                     pl.BlockSpec((1, 1, D_SHIFT), lambda i: (i // seq_tiles, 0, 0)),
                     vec(D_SHIFT), vec(D_RWKV), vec(D_RWKV),
                     pl.BlockSpec((2 * LORA, 2 * D_RWKV), lambda i: (0, 0)),
                     vec(D_RWKV), vec(D_RWKV), vec(D_RWKV)]
        args += [fused["hist16"], fused["pool_w"], fused["pool_scale"].reshape(1, D_POOL),
                 fused["shift_prev"].reshape(b, 1, D_SHIFT), lp["mu"], lp["w0"], lp["a0"], lp["lora"],
                 lp["k_k"], lp["k_a"], lp["r_k"]]
        out_specs += [pl.BlockSpec((tm, D_POOL), lambda i: (i, 0)),
                      pl.BlockSpec((tm, N_STAGE * D_RWKV), lambda i: (i, 0)),
                      pl.BlockSpec((tm // c, 8, D_RWKV), lambda i: (i, 0, 0))]
        out_shape += [jax.ShapeDtypeStruct((m, D_POOL), BF16),
                      jax.ShapeDtypeStruct((m, N_STAGE * D_RWKV), F32),
                      jax.ShapeDtypeStruct((m // c, 8, D_RWKV), F32)]
        scratch = [pltpu.VMEM((16, D_POOL), F32), pltpu.VMEM((1, D_SHIFT), F32)]
    return pl.pallas_call(
        functools.partial(_inproj_kernel, seq_tiles=seq_tiles, pos0=pos0, c=c),
        grid=(m // tm,),
        in_specs=in_specs,
        out_specs=out_specs,
        out_shape=out_shape,
        scratch_shapes=scratch,
        compiler_params=_cparams(("arbitrary",)),
        name="inproj",
    )(*args)


def _outproj_kernel(h_ref, mp_ref, ma_ref, mr_ref, w_ref, fw_ref, o_ref, *, final):
    acc = jnp.dot(mp_ref[...], w_ref[0:D_POOL, :], preferred_element_type=F32)
    acc += jnp.dot(ma_ref[...], w_ref[D_POOL:D_POOL + D_ATTN, :], preferred_element_type=F32)
    acc += jnp.dot(mr_ref[...], w_ref[D_POOL + D_ATTN:, :], preferred_element_type=F32)
    hn = h_ref[...] + acc
    if final:
        ms = jnp.mean(hn * hn, axis=-1, keepdims=True)
        hn = hn * lax.rsqrt(ms + NORM_EPS) * fw_ref[...]
    o_ref[...] = hn


def _outproj(h2d, mp, ma, mr, w_out_bf16, final_w, final, tm):
    m = h2d.shape[0]
    row = lambda width: pl.BlockSpec((tm, width), lambda i: (i, 0))
    return pl.pallas_call(
        functools.partial(_outproj_kernel, final=final),
        grid=(m // tm,),
        in_specs=[row(D_MODEL), row(D_POOL), row(D_ATTN), row(D_RWKV),
                  pl.BlockSpec(memory_space=pltpu.VMEM),
                  pl.BlockSpec((1, D_MODEL), lambda i: (0, 0))],
        out_specs=row(D_MODEL),
        out_shape=jax.ShapeDtypeStruct((m, D_MODEL), F32),
        compiler_params=_cparams(("arbitrary",)),
        name="outproj",
    )(h2d, mp, ma, mr, w_out_bf16, final_w.reshape(1, D_MODEL))


def _pool_math(p, gate, halo, pos_first, pw_ref, ps_ref):
    tt = p.shape[0]
    ext = jnp.concatenate([halo, p], axis=0)
    pos = pos_first + lax.broadcasted_iota(jnp.int32, (tt, 1), 0)
    outs = []
    for g, w in enumerate(POOL_WINDOWS):
        s = ext[:, g * POOL_GROUP:(g + 1) * POOL_GROUP]
        span = 1
        while span < w:
            n = s.shape[0]
            s = s[span:n] + s[0:n - span]
            span *= 2
        win = s[16 - (w - 1):16 - (w - 1) + tt]
        cnt = jnp.minimum(pos + 1, w).astype(F32)
        d = win / cnt - p[:, g * POOL_GROUP:(g + 1) * POOL_GROUP]
        outs.append(jnp.dot(d.astype(BF16), pw_ref[g], preferred_element_type=F32))
    y = jnp.concatenate(outs, axis=1) * ps_ref[...]
    return (y * _silu(gate)).astype(BF16)


def _pool_kernel(pg_ref, halo_ref, hist_ref, pw_ref, ps_ref, o_ref, *, tt, pos0):
    i = pl.program_id(1)
    halo = jnp.where(i == 0, hist_ref[0], halo_ref[0, :, 0:D_POOL])
    o_ref[0] = _pool_math(pg_ref[0, :, 0:D_POOL], pg_ref[0, :, D_POOL:2 * D_POOL], halo, pos0 + i * tt, pw_ref, ps_ref)


def _pool(pg, hist16, pool_w_bf16, pool_scale, pos0, tt):
    b, t, _ = pg.shape
    nh = tt // 16
    return pl.pallas_call(
        functools.partial(_pool_kernel, tt=tt, pos0=pos0),
        grid=(b, t // tt),
        in_specs=[
            pl.BlockSpec((1, tt, 2 * D_POOL), lambda bi, i: (bi, i, 0)),
            pl.BlockSpec((1, 16, 2 * D_POOL), lambda bi, i: (bi, jnp.maximum(i * nh - 1, 0), 0)),
            pl.BlockSpec((1, 16, D_POOL), lambda bi, i: (bi, 0, 0)),
            pl.BlockSpec((4, POOL_GROUP, POOL_GROUP), lambda bi, i: (0, 0, 0)),
            pl.BlockSpec((1, D_POOL), lambda bi, i: (0, 0)),
        ],
        out_specs=pl.BlockSpec((1, tt, D_POOL), lambda bi, i: (bi, i, 0)),
        out_shape=jax.ShapeDtypeStruct((b, t, D_POOL), BF16),
        compiler_params=_cparams(("arbitrary", "arbitrary")),
        name="pool",
    )(pg, pg, hist16, pool_w_bf16, pool_scale.reshape(1, D_POOL))


def _t5_bucket(rel):
    nb = NUM_BUCKETS // 2
    max_exact = nb // 2
    ret = jnp.where(rel > 0, nb, 0)
    n = jnp.abs(rel)
    nf = jnp.maximum(n, 1).astype(F32)
    large = max_exact + (jnp.log(nf / max_exact) / math.log(MAX_DISTANCE / max_exact) * (nb - max_exact)).astype(jnp.int32)
    large = jnp.minimum(large, nb - 1)
    return ret + jnp.where(n < max_exact, n, large)


def _bias_kernel(bucket_ref, tab_ref, o_ref):
    bucket = bucket_ref[...]
    tab = tab_ref[...]
    out = jnp.full(bucket.shape, NEG, F32)
    for b in range(NUM_BUCKETS):
        out = jnp.where(bucket == b, tab[:, b:b + 1], out)
    o_ref[...] = out


def _attn_bias(table, tq, nk, prefix_valid):
    qi = jnp.arange(tq)
    kj = jnp.arange(nk)
    rel = kj[None, :] - WINDOW - qi[:, None]
    bucket = _t5_bucket(rel)
    qc = qi // CHUNK
    kc = (kj - WINDOW) // CHUNK
    valid = (kc[None, :] <= qc[:, None]) & (kc[None, :] >= qc[:, None] - WIN_CHUNKS)
    valid &= (kj < WINDOW + tq)[None, :]
    if not prefix_valid:
        valid &= (kj >= WINDOW)[None, :]
    bucket = jnp.where(valid, bucket, -1).astype(jnp.int32)
    heads = jnp.array([[[8 * g + 2 * p + par for p in range(4)] for par in range(2)] for g in range(2)]).reshape(-1)
    rows = heads.shape[0] * tq
    bucket_rows = jnp.broadcast_to(bucket[None], (heads.shape[0], tq, nk)).reshape(rows, nk)
    tab_rows = jnp.broadcast_to(table.astype(F32).T[heads][:, None, :], (heads.shape[0], tq, NUM_BUCKETS)).reshape(rows, NUM_BUCKETS)
    tr = 4 * tq
    return pl.pallas_call(
        _bias_kernel,
        grid=(rows // tr,),
        in_specs=[pl.BlockSpec((tr, nk), lambda i: (i, 0)), pl.BlockSpec((tr, NUM_BUCKETS), lambda i: (i, 0))],
        out_specs=pl.BlockSpec((tr, nk), lambda i: (i, 0)),
        out_shape=jax.ShapeDtypeStruct((rows, nk), F32),
        name="attn_bias",
    )(bucket_rows, tab_rows)


def _sink_rows(sinks, tq):
    heads = jnp.array([[[8 * g + 2 * p + par for p in range(4)] for par in range(2)] for g in range(2)]).reshape(-1)
    return jnp.broadcast_to(sinks.astype(F32)[heads][:, None, None], (heads.shape[0], tq, LANES)).reshape(-1, LANES)


def _attn_kernel(*refs, tq, has_prev):
    if has_prev:
        q_ref, kvc_ref, kvp_ref, pre_ref, g_ref, bias_ref, sink_ref, o_ref = refs
    else:
        q_ref, kvc_ref, pre_ref, g_ref, bias_ref, sink_ref, o_ref = refs
    i = pl.program_id(1)
    if has_prev:
        kvp = jnp.where(i == 0, pre_ref[0], kvp_ref[0])
    else:
        kvp = pre_ref[0]
    kv = jnp.concatenate([kvp, kvc_ref[0]], axis=0)
    nk = kv.shape[0]
    k = kv[:, 0:LANES] * (HEAD_DIM ** -0.5)
    v = kv[:, LANES:2 * LANES]
    low = lax.broadcasted_iota(jnp.int32, (nk, LANES), 1) < HEAD_DIM
    k_sw = pltpu.roll(k, HEAD_DIM, axis=1)
    v_sw = pltpu.roll(v, HEAD_DIM, axis=1)
    ones = jnp.ones_like(v)

    def place(x, x_sw, g, par):
        src = x if g == par else x_sw
        return jnp.where(low, src, 0.0) if par == 0 else jnp.where(low, 0.0, src)

    blocks = [(g, par) for g in range(N_KV_HEADS) for par in range(2)]
    kx = [place(k, k_sw, g, par).astype(BF16) for g, par in blocks]
    vx = [jnp.concatenate([place(v, v_sw, g, par), ones], axis=1).astype(BF16) for g, par in blocks]
    qs = [jnp.concatenate([q_ref[0, :, (4 * g + p) * LANES:(4 * g + p + 1) * LANES] for p in range(4)], axis=0).astype(BF16)
          for g in range(N_KV_HEADS)]
    rows = [slice(n * 4 * tq, (n + 1) * 4 * tq) for n in range(len(blocks))]
    s = [_dot(qs[g], kx[n], NT) + bias_ref[rows[n], :] for n, (g, par) in enumerate(blocks)]
    sink = [sink_ref[rows[n], :] for n in range(len(blocks))]
    m = [jnp.maximum(jnp.broadcast_to(jnp.max(s[n], axis=-1, keepdims=True), (4 * tq, LANES)), sink[n])
         for n in range(len(blocks))]
    widen = lambda x: jnp.concatenate([x] * (nk // LANES) + ([x[:, 0:nk % LANES]] if nk % LANES else []), axis=1)
    e = [jnp.exp(s[n] - widen(m[n])).astype(BF16) for n in range(len(blocks))]
    pv = [_dot(e[n], vx[n]) for n in range(len(blocks))]
    out = [pv[n][:, 0:LANES] / (pv[n][:, LANES:] + jnp.exp(sink[n] - m[n])) for n in range(len(blocks))]
    for g in range(N_KV_HEADS):
        acc = out[2 * g] + out[2 * g + 1]
        for p in range(4):
            cols = slice((4 * g + p) * LANES, (4 * g + p + 1) * LANES)
            o_ref[0, :, cols] = (acc[p * tq:(p + 1) * tq] * _silu(g_ref[0, :, cols])).astype(BF16)


def _attn(q, kv, prefix, gate, bias, sink_rows, tq):
    b, t, _ = q.shape
    n_tiles = t // tq
    has_prev = n_tiles > 1
    nk = WINDOW + tq
    rows = bias.shape[-2]
    in_specs = [
        pl.BlockSpec((1, tq, D_ATTN), lambda bi, i: (bi, i, 0)),
        pl.BlockSpec((1, tq, 4 * HEAD_DIM), lambda bi, i: (bi, i, 0)),
    ]
    args = [q, kv]
    if has_prev:
        in_specs.append(pl.BlockSpec((1, WINDOW, 4 * HEAD_DIM), lambda bi, i: (bi, jnp.maximum(i - 1, 0), 0)))
        args.append(kv)
    in_specs += [
        pl.BlockSpec((1, WINDOW, 4 * HEAD_DIM), lambda bi, i: (bi, 0, 0)),
        pl.BlockSpec((1, tq, D_ATTN), lambda bi, i: (bi, i, 0)),
        pl.BlockSpec((None, rows, nk), lambda bi, i: (jnp.minimum(i, bias.shape[0] - 1), 0, 0)),
        pl.BlockSpec((rows, LANES), lambda bi, i: (0, 0)),
    ]
    args += [prefix, gate, bias, sink_rows]
    return pl.pallas_call(
        functools.partial(_attn_kernel, tq=tq, has_prev=has_prev),
        grid=(b, n_tiles),
        in_specs=in_specs,
        out_specs=pl.BlockSpec((1, tq, D_ATTN), lambda bi, i: (bi, i, 0)),
        out_shape=jax.ShapeDtypeStruct((b, t, D_ATTN), BF16),
        compiler_params=_cparams(("arbitrary", "arbitrary")),
        name="attn",
    )(*args)


def _mm3(a, b):
    return _dot(a[0], b[0]) + (_dot(a[0], b[1]) + _dot(a[1], b[0]))


def _tri_inverse_all(mats, blk_mask, merge_masks, eye, each):
    idx = range(len(mats))
    c = eye.shape[0]
    power = [jnp.where(blk_mask, a, 0.0) for a in mats]
    t = [eye + d for d in power]
    power = each(lambda i: _dot(power[i].astype(BF16), power[i].astype(BF16)), idx)
    span = 2
    while 2 * span < INV_BASE:
        both = each(lambda i: _dot(power[i].astype(BF16),
                                   jnp.concatenate([power[i], t[i]], axis=1).astype(BF16)), idx)
        power = [x[:, 0:c] for x in both]
        t = [t[i] + both[i][:, c:] for i in idx]
        span *= 2
    t = each(lambda i: t[i] + _dot(power[i].astype(BF16), t[i].astype(BF16)), idx)
    for mask in merge_masks:
        tb = [x.astype(BF16) for x in t]
        low = each(lambda i: _dot(tb[i], jnp.where(mask, mats[i], 0.0).astype(BF16)).astype(BF16), idx)
        t = each(lambda i: t[i] + _dot(low[i], tb[i]), idx)
    ts = [_split(x, 2) for x in t]
    resid = each(lambda i: ((eye - t[i]) + _mm3(_split(mats[i], 2), ts[i])).astype(BF16), idx)
    return each(lambda i: t[i] + _dot(ts[i][0], resid[i]), idx)


N_STAGE = 8


def _rwkv_elementwise(q, xc_ref, prev_scr, mu_ref, w0_ref, a0_ref, lora_ref, kk_ref, ka_ref, rk_ref, head_ones, emit, c, t_valid):
    row = lax.broadcasted_iota(jnp.int32, (c, 1), 0)
    live = row < t_valid
    mask = (lambda x: jnp.where(live, x, 0.0)) if t_valid < c else (lambda x: x)

    def shifted(lo_col, width):
        x = xc_ref[q, :, lo_col:lo_col + width]
        prev = jnp.where(row == 0, prev_scr[q, :, lo_col:lo_col + width], pltpu.roll(x, 1, axis=0))
        return x + (prev - x) * mu_ref[:, lo_col:lo_col + width]

    lo = shifted(3 * D_RWKV, 2 * LORA)
    first_half = lax.broadcasted_iota(jnp.int32, (c, LANES), 1) < RWKV_HEAD
    lora = _mm(jnp.where(first_half, jnp.tanh(lo), lo), lora_ref[...], 3)
    tri = (lax.broadcasted_iota(jnp.int32, (c, c), 0) >= lax.broadcasted_iota(jnp.int32, (c, c), 1)).astype(BF16)
    yield
    for p in range(N_RWKV_HEADS // 2):
        cols = slice(p * LANES, (p + 1) * LANES)
        r = shifted(p * LANES, LANES)
        k = shifted(D_RWKV + p * LANES, LANES)
        v = shifted(2 * D_RWKV + p * LANES, LANES)
        emit(6, p, v)
        z = -(w0_ref[:, cols] + lora[:, cols])
        softplus = jnp.maximum(z, 0.0) + jnp.log(1.0 + jnp.exp(-jnp.abs(z)))
        logw = mask(-jnp.exp(-softplus - 0.5))
        cum = None
        for part in _split(logw, 3):
            d = _dot(tri, part)
            cum = d if cum is None else cum + d
        cum_last = cum[c - 1:c]
        a = _sigmoid(a0_ref[:, cols] + lora[:, D_RWKV + p * LANES:D_RWKV + (p + 1) * LANES])
        kk = k * kk_ref[:, cols]
        kk = kk * jnp.minimum(lax.rsqrt(_dot((kk * kk).astype(BF16), head_ones)), 1e12)
        k = k * (1.0 + (a - 1.0) * ka_ref[:, cols])
        emit(7, p, _dot((r * k * rk_ref[:, cols]).astype(BF16), head_ones) * v)
        k = mask(k)
        alpha = mask(-kk)
        beta = mask(kk * a)
        emit(0, p, r * jnp.exp(cum))
        emit(1, p, alpha * jnp.exp(cum - logw))
        g_inv = jnp.exp(-cum)
        emit(2, p, beta * g_inv)
        emit(3, p, k * g_inv)
        g_rest = jnp.exp(cum_last - cum)
        emit(4, p, beta * g_rest)
        emit(5, p, k * g_rest)
        emit(N_STAGE, p, jnp.exp(cum_last))
        yield


def _rwkv_kernel(*refs, c, t_valid, staged_input):
    if staged_input:
        st_ref, gall_ref, g_ref, p0_ref, lnw_ref, lnb_ref, o_ref, pout_ref, state_scr = refs
    else:
        (xc_ref, g_ref, shift_ref, p0_ref, mu_ref, w0_ref, a0_ref, lora_ref, kk_ref, ka_ref, rk_ref,
         lnw_ref, lnb_ref, o_ref, pout_ref, prev_scr, state_scr) = refs
    i = pl.program_id(1)
    nb = g_ref.shape[0]
    n_pairs = N_RWKV_HEADS // 2

    @pl.when(i == 0)
    def _():
        state_scr[...] = p0_ref[...]
        if not staged_input:
            prev_scr[...] = shift_ref[...]

    units = [(q, p) for q in range(nb) for p in range(n_pairs)]
    nu = range(len(units))
    nh = range(2 * len(units))

    pi = lax.broadcasted_iota(jnp.int32, (LANES, LANES), 0)
    pj = lax.broadcasted_iota(jnp.int32, (LANES, LANES), 1)
    head_block = _shr(pi, HEAD_SHIFT) == _shr(pj, HEAD_SHIFT)
    head_ones = head_block.astype(BF16)

    def head_sum(x):
        stacked = jnp.concatenate([x[:, p * LANES:(p + 1) * LANES] for p in range(n_pairs)], axis=0)
        s = _dot(stacked.astype(BF16), head_ones)
        return jnp.concatenate([s[p * c:(p + 1) * c] for p in range(n_pairs)], axis=1)

    staged = {}

    def emit(q, s, p, x):
        staged[s, q, p] = x

    last_row = t_valid - 1 if t_valid < c else c - 1
    for q in range(nb):
        if staged_input:
            for p in range(n_pairs):
                for s in range(N_STAGE):
                    emit(q, s, p, st_ref[q, :, s * D_RWKV + p * LANES:s * D_RWKV + (p + 1) * LANES])
                emit(q, N_STAGE, p, gall_ref[q, 0, 0:1, p * LANES:(p + 1) * LANES])
        else:
            for _ in _rwkv_elementwise(q, xc_ref, prev_scr, mu_ref, w0_ref, a0_ref, lora_ref, kk_ref, ka_ref, rk_ref,
                                       head_ones, functools.partial(emit, q), c, t_valid):
                pass
            prev_scr[q] = xc_ref[q, last_row:last_row + 1, :]
    rb, ab, bb, kb, bt, kt, vp, bonus, g_all = ([staged[s, q, p] for q, p in units] for s in range(N_STAGE + 1))
    each = lambda fn, items: [fn(x) for x in items]

    first_half = lax.broadcasted_iota(jnp.int32, (c, LANES), 1) < RWKV_HEAD
    ti = lax.broadcasted_iota(jnp.int32, (c, c), 0)
    tj = lax.broadcasted_iota(jnp.int32, (c, c), 1)
    tri_incl = ti >= tj
    tri_strict = ti > tj
    tri_incl2 = jnp.concatenate([tri_incl, tri_incl], axis=1)
    eye = (ti == tj).astype(F32)
    base_shift = INV_BASE.bit_length() - 1
    blk_mask = _shr(ti, base_shift) == _shr(tj, base_shift)
    merge_masks = []
    sh = base_shift
    while (1 << sh) < c:
        merge_masks.append((_shr(ti, sh + 1) == _shr(tj, sh + 1)) & ((_shr(ti, sh) & 1) == 1) & ((_shr(tj, sh) & 1) == 0))
        sh += 1
    diag128 = pi == pj
    zeros_c = jnp.zeros((c, LANES), F32)
    pick = lambda x0, x1: jnp.where(first_half, x0, x1)
    half = lambda x, n: x[(n % 2) * c:(n % 2 + 1) * c]

    bk = [jnp.concatenate([bb[u], kb[u]], axis=0) for u in nu]
    ga = each(lambda u: _mm(jnp.concatenate([pick(ab[u], 0.0), pick(0.0, ab[u])], axis=0), bk[u], 3, NT), nu)
    gr = each(lambda u: _mm(jnp.concatenate([pick(rb[u], 0.0), pick(0.0, rb[u])], axis=0), bk[u], 1, NT), nu)
    a_ab = [jnp.where(tri_strict, half(ga[n // 2], n)[:, 0:c], 0.0) for n in nh]
    a_ak = [jnp.where(tri_strict, half(ga[n // 2], n)[:, c:2 * c], 0.0) for n in nh]
    lr = [jnp.where(tri_incl2, half(gr[n // 2], n), 0.0) for n in nh]
    t_inv = _tri_inverse_all(a_ab, blk_mask, merge_masks, eye, each)
    akv = each(lambda n: _mm(a_ak[n], vp[n // 2]), nh)
    xh = each(lambda n: _mm(t_inv[n], jnp.concatenate([ab[n // 2], akv[n]], axis=1)), nh)
    a_new = [pick(xh[2 * u][:, 0:LANES], xh[2 * u + 1][:, 0:LANES]) for u in nu]
    u0 = [pick(xh[2 * u][:, LANES:], xh[2 * u + 1][:, LANES:]) for u in nu]
    zmat = [jnp.concatenate([jnp.concatenate([a_new[u], u0[u]], axis=1),
                             jnp.concatenate([zeros_c, vp[u]], axis=1)], axis=0).astype(BF16) for u in nu]
    yh = each(lambda n: _dot(lr[n].astype(BF16), zmat[n // 2]), nh)
    mn = each(lambda u: _dot(jnp.concatenate([bt[u], kt[u]], axis=0).astype(BF16), zmat[u], TN), nu)
    o_units = []
    for u, (q, p) in enumerate(units):
        r_new = rb[u] + pick(yh[2 * u][:, 0:LANES], yh[2 * u + 1][:, 0:LANES])
        o0 = pick(yh[2 * u][:, LANES:], yh[2 * u + 1][:, LANES:])
        m_mat = jnp.where(diag128, g_all[u], 0.0) + jnp.where(head_block, mn[u][:, 0:LANES], 0.0)
        n0 = jnp.where(head_block, mn[u][:, LANES:], 0.0)
        state = _split(state_scr[q, p], 2)
        o_units.append(_dot(r_new.astype(BF16), state[0]) + o0)
        state_scr[q, p] = _mm3(_split(m_mat, 2), state) + n0

    inv_n = 1.0 / RWKV_HEAD
    for q in range(nb):
        sl = slice(q * n_pairs, (q + 1) * n_pairs)
        o = jnp.concatenate(o_units[sl], axis=1)
        mean = head_sum(o) * inv_n
        cen = o - mean
        var = head_sum(cen * cen) * inv_n
        y = cen * lax.rsqrt(var + LNX_EPS) * lnw_ref[...] + lnb_ref[...]
        o_ref[q] = ((y + jnp.concatenate(bonus[sl], axis=1)) * _silu(g_ref[q])).astype(BF16)

    @pl.when(i == pl.num_programs(1) - 1)
    def _():
        pout_ref[...] = state_scr[...]


RWKV_SEQS = 2


def _rwkv(gate, p0, lp, c, t_valid, xc=None, shift_prev=None, staged=None):
    b, t, _ = gate.shape
    vec = lambda n: pl.BlockSpec((1, n), lambda bi, i: (0, 0))
    n_pairs = N_RWKV_HEADS // 2
    nb = RWKV_SEQS if b % RWKV_SEQS == 0 else 1
    seq_block = lambda width: pl.BlockSpec((nb, c, width), lambda bi, i: (bi, i, 0))
    state_block = pl.BlockSpec((nb, n_pairs, LANES, LANES), lambda bi, i: (bi, 0, 0, 0))
    scratch = [pltpu.VMEM((nb, n_pairs, LANES, LANES), F32)]
    if staged is not None:
        st, gall = staged
        in_specs = [seq_block(N_STAGE * D_RWKV), pl.BlockSpec((nb, 1, 8, D_RWKV), lambda bi, i: (bi, i, 0, 0)),
                    seq_block(D_RWKV), state_block, vec(D_RWKV), vec(D_RWKV)]
        args = [st.reshape(b, t, N_STAGE * D_RWKV), gall.reshape(b, t // c, 8, D_RWKV), gate, p0, lp["lnx_w"], lp["lnx_b"]]
    else:
        in_specs = [seq_block(D_SHIFT), seq_block(D_RWKV), pl.BlockSpec((nb, 1, D_SHIFT), lambda bi, i: (bi, 0, 0)),
                    state_block, vec(D_SHIFT), vec(D_RWKV), vec(D_RWKV),
                    pl.BlockSpec((2 * LORA, 2 * D_RWKV), lambda bi, i: (0, 0)),
                    vec(D_RWKV), vec(D_RWKV), vec(D_RWKV), vec(D_RWKV), vec(D_RWKV)]
        args = [xc, gate, shift_prev.reshape(b, 1, D_SHIFT), p0, lp["mu"], lp["w0"], lp["a0"], lp["lora"],
                lp["k_k"], lp["k_a"], lp["r_k"], lp["lnx_w"], lp["lnx_b"]]
        scratch = [pltpu.VMEM((nb, 1, D_SHIFT), F32)] + scratch
    return pl.pallas_call(
        functools.partial(_rwkv_kernel, c=c, t_valid=t_valid, staged_input=staged is not None),
        grid=(b // nb, t // c),
        in_specs=in_specs,
        out_specs=[seq_block(D_RWKV), state_block],
        out_shape=[jax.ShapeDtypeStruct((b, t, D_RWKV), BF16),
                   jax.ShapeDtypeStruct((b, n_pairs, LANES, LANES), F32)],
        scratch_shapes=scratch,
        compiler_params=_cparams(("arbitrary", "arbitrary")),
        name="rwkv",
    )(*args)


def _state_to_pairs(s):
    b = s.shape[0]
    pt = jnp.swapaxes(s, -1, -2).reshape(b, N_RWKV_HEADS // 2, 2, RWKV_HEAD, RWKV_HEAD)
    z = jnp.zeros_like(pt[:, :, 0])
    top = jnp.concatenate([pt[:, :, 0], z], axis=-1)
    bot = jnp.concatenate([z, pt[:, :, 1]], axis=-1)
    return jnp.concatenate([top, bot], axis=-2)


def _pairs_to_state(pm):
    b = pm.shape[0]
    h0 = pm[:, :, 0:RWKV_HEAD, 0:RWKV_HEAD]
    h1 = pm[:, :, RWKV_HEAD:, RWKV_HEAD:]
    pt = jnp.stack([h0, h1], axis=2).reshape(b, N_RWKV_HEADS, RWKV_HEAD, RWKV_HEAD)
    return jnp.swapaxes(pt, -1, -2)


def _layer(h, lp, att, pool_hist, shift_prev, wkv0, kv_prefix, pos0, final_w, final, cfg):
    b, t, _ = h.shape
    hist16 = jnp.concatenate([jnp.zeros((b, 1, D_POOL), F32), pool_hist], axis=1)
    c = RWKV_CHUNK
    tm = cfg["tm_in"]
    fuse = t % tm == 0 and tm % c == 0
    fused = dict(t=t, pos0=pos0, hist16=hist16, pool_w=lp["pool_w"], pool_scale=lp["pool_scale"],
                 shift_prev=shift_prev, lp=lp) if fuse else None
    outs = _inproj(h.reshape(b * t, D_MODEL), lp["norm_w"], lp["w_in"], tm, fused)
    shape3 = lambda x: x.reshape(b, t, x.shape[-1])
    pg, q, kv, ga, xc, gr = (shape3(x) for x in outs[:len(SEGMENTS)])
    ma = _attn(q, kv, kv_prefix, ga, att["bias"], _sink_rows(lp["sinks"], cfg["tq"]), cfg["tq"])
    p0 = _state_to_pairs(wkv0)
    if fuse:
        mp = shape3(outs[len(SEGMENTS)])
        mr, p_new = _rwkv(gr, p0, lp, c, c, staged=outs[len(SEGMENTS) + 1:])
    else:
        mp = _pool(pg, hist16, lp["pool_w"], lp["pool_scale"], pos0, cfg["tt_pool"])
        pad = (-t) % c
        xc_in = jnp.pad(xc, ((0, 0), (0, pad), (0, 0)))
        gr_in = jnp.pad(gr, ((0, 0), (0, pad), (0, 0)))
        mr, p_new = _rwkv(gr_in, p0, lp, c, t if pad else c, xc=xc_in, shift_prev=shift_prev)
        mr = mr[:, :t]

    flat = lambda x: x.reshape(b * t, x.shape[-1])
    h_new = _outproj(flat(h), flat(mp), flat(ma), flat(mr), lp["w_out"], final_w, final, cfg["tm_out"])
    new_pool = pg[:, -POOL_HIST:, 0:D_POOL]
    kvf = kv[:, -WINDOW:] if t >= WINDOW else jnp.concatenate([kv_prefix[:, t:], kv], axis=1)
    new_k = kvf[:, :, 0:LANES].reshape(b, WINDOW, N_KV_HEADS, HEAD_DIM)
    new_v = kvf[:, :, LANES:].reshape(b, WINDOW, N_KV_HEADS, HEAD_DIM)
    new_shift = xc[:, -1]
    return h_new.reshape(b, t, D_MODEL), (new_pool, new_k, new_v, new_shift, _pairs_to_state(p_new))


def _group_cfg(b, t):
    m = b * t
    tq = min(t, ATT_TILE)
    return {"tm_in": min(m, 256), "tm_out": min(m, 512), "tt_pool": min(t, 512), "tq": tq}


def kernel(x_prompt, x_sample, state_pool, cache_swa_k, cache_swa_v, state_rwkv_shift, state_rwkv_wkv, norm_w, w_in, w_out, pool_w, pool_scale, attn_sinks, rel_bias_table, rwkv_mu, rwkv_w0, rwkv_w_up, rwkv_a0, rwkv_a_up, rwkv_k_k, rwkv_k_a, rwkv_r_k, rwkv_lnx_w, rwkv_lnx_b, final_norm_w):
    bp, tp, _ = x_prompt.shape
    bs, ts, _ = x_sample.shape
    cfg_p = _group_cfg(bp, tp)
    cfg_s = _group_cfg(bs, ts)
    att_p = {"bias": jnp.stack([_attn_bias(rel_bias_table, cfg_p["tq"], WINDOW + cfg_p["tq"], False),
                                _attn_bias(rel_bias_table, cfg_p["tq"], WINDOW + cfg_p["tq"], True)])}
    att_s = {"bias": _attn_bias(rel_bias_table, cfg_s["tq"], WINDOW + cfg_s["tq"], True)[None]}

    hp, hs = x_prompt, x_sample
    prompt_states, sample_states = [], []
    row = lambda x: x.astype(F32).reshape(1, -1)
    zero_lora = jnp.zeros((LORA, D_RWKV), F32)
    for l in range(DEPTH):
        lp = {
            "norm_w": norm_w[l], "w_in": w_in[l].astype(BF16), "w_out": w_out[l].astype(BF16),
            "pool_w": pool_w[l].astype(BF16), "pool_scale": pool_scale[l], "sinks": attn_sinks[l],
            "mu": row(rwkv_mu[l]), "w0": row(rwkv_w0[l]), "a0": row(rwkv_a0[l]),
            "lora": jnp.concatenate([jnp.concatenate([rwkv_w_up[l].astype(F32), zero_lora], axis=1),
                                     jnp.concatenate([zero_lora, rwkv_a_up[l].astype(F32)], axis=1)], axis=0),
            "k_k": row(rwkv_k_k[l]), "k_a": row(rwkv_k_a[l]), "r_k": row(rwkv_r_k[l]),
            "lnx_w": row(rwkv_lnx_w[l]), "lnx_b": row(rwkv_lnx_b[l]),
        }
        final = l == DEPTH - 1
        hp, sp = _layer(hp, lp, att_p, jnp.zeros((bp, POOL_HIST, D_POOL), F32), jnp.zeros((bp, D_SHIFT), F32),
                        jnp.zeros((bp, N_RWKV_HEADS, RWKV_HEAD, RWKV_HEAD), F32),
                        jnp.zeros((bp, WINDOW, 4 * HEAD_DIM), F32), 0, final_norm_w, final, cfg_p)
        prefix_s = jnp.concatenate([cache_swa_k[l].reshape(bs, WINDOW, LANES), cache_swa_v[l].reshape(bs, WINDOW, LANES)], axis=-1)
        hs, ss = _layer(hs, lp, att_s, state_pool[l], state_rwkv_shift[l], state_rwkv_wkv[l], prefix_s, PAST_LEN,
                        final_norm_w, final, cfg_s)
        prompt_states.append(sp)
        sample_states.append(ss)
    outs_p = [jnp.stack(x) for x in zip(*prompt_states)]
    outs_s = [jnp.stack(x) for x in zip(*sample_states)]
    return (hp, hs, *outs_p, *outs_s)
```

```python
import functools
import math

import jax
import jax.numpy as jnp
from jax import lax
from jax.experimental import pallas as pl
from jax.experimental.pallas import tpu as pltpu

F32 = jnp.float32
BF16 = jnp.bfloat16

D_MODEL = 2048
DEPTH = 4
PAST_LEN = 1024
CHUNK = 64
D_POOL = 512
POOL_WINDOWS = (2, 4, 8, 16)
POOL_GROUP = 128
POOL_HIST = 15
HEAD_DIM = 64
D_ATTN = 1024
N_Q_HEADS = 16
N_KV_HEADS = 2
WINDOW = 128
WIN_CHUNKS = 2
NUM_BUCKETS = 32
MAX_DISTANCE = 128
NEG = -1e30
LOG2E = 1.0 / math.log(2.0)
D_RWKV = 512
RWKV_HEAD = 64
N_RWKV_HEADS = 8
LORA = 64
D_SHIFT = 3 * D_RWKV + 2 * LORA
NORM_EPS = 1e-5
LNX_EPS = 1e-5 * RWKV_HEAD
SEGMENTS = (2 * D_POOL, D_ATTN, 2 * N_KV_HEADS * HEAD_DIM, D_ATTN, D_SHIFT, D_RWKV)
D_IN = sum(SEGMENTS)

LANES = 128
ATT_TILE = 128
RWKV_CHUNK = 128
INV_BASE = 16
VMEM_LIMIT = 56 * 1024 * 1024


def _cparams(sem):
    return pltpu.CompilerParams(dimension_semantics=sem, vmem_limit_bytes=VMEM_LIMIT)


def _sigmoid(x):
    return 0.5 * jnp.tanh(0.5 * x) + 0.5


def _silu(x):
    return x * _sigmoid(x)


def _split(x, terms):
    parts = []
    rem = x
    for _ in range(terms):
        p = rem.astype(BF16)
        parts.append(p)
        rem = rem - p.astype(F32)
    return parts


def _dot(a, b, dims=None):
    if dims is None:
        return jnp.dot(a, b, preferred_element_type=F32)
    return lax.dot_general(a, b, (dims, ((), ())), preferred_element_type=F32)


def _mm(a, b, passes=1, dims=None):
    if passes == 1:
        return _dot(a.astype(BF16), b.astype(BF16), dims)
    ah, al = _split(a, 2)
    bh, bl = _split(b, 2)
    return _dot(ah, bh, dims) + (_dot(ah, bl, dims) + _dot(al, bh, dims))


def _mm_exact_rhs(a, b_bf16, terms=3):
    out = None
    for p in _split(a, terms):
        d = _dot(p, b_bf16)
        out = d if out is None else out + d
    return out


HEAD_SHIFT = RWKV_HEAD.bit_length() - 1


def _shr(x, bits):
    return lax.shift_right_logical(x, jnp.full_like(x, bits))


NT = ((1,), (1,))
TN = ((0,), (0,))


CAST_ROWS = 256


def _cast_kernel(w_ref, o_ref):
    o_ref[...] = w_ref[...].astype(BF16)


def _layer_bf16(w, l):
    _, rows, cols = w.shape
    return pl.pallas_call(
        _cast_kernel,
        grid=(rows // CAST_ROWS,),
        in_specs=[pl.BlockSpec((None, CAST_ROWS, cols), lambda i: (l, i, 0))],
        out_specs=pl.BlockSpec((CAST_ROWS, cols), lambda i: (i, 0)),
        out_shape=jax.ShapeDtypeStruct((rows, cols), BF16),
        compiler_params=_cparams(("arbitrary",)),
        name="cast_bf16",
    )(w)


SEG_OFFSETS = tuple(sum(SEGMENTS[:n]) for n in range(len(SEGMENTS)))
FUSE_COLS = 512


class _ChunkView:
    def __init__(self, x, c):
        self.x, self.c = x, c

    def __getitem__(self, idx):
        q, _, cols = idx
        return self.x[q * self.c:(q + 1) * self.c, cols]


class _PrevRowView:
    def __init__(self, x, first, c):
        self.x, self.first, self.c = x, first, c

    def __getitem__(self, idx):
        q, _, cols = idx
        return self.first[:, cols] if q == 0 else self.x[q * self.c - 1:q * self.c, cols]


def _inproj_kernel(h_ref, nw_ref, w_ref, *refs, seq_tiles, pos0, c):
    x = h_ref[...]
    ms = jnp.mean(x * x, axis=-1, keepdims=True)
    xn = (x * lax.rsqrt(ms + NORM_EPS) * nw_ref[...]).astype(BF16)
    tm = x.shape[0]
    fused = seq_tiles > 0
    if fused:
        hist_ref, pw_ref, ps_ref, shift_ref, mu_ref, w0_ref, a0_ref, lora_ref, kk_ref, ka_ref, rk_ref, *refs = refs
        mp_ref, st_ref, gall_ref, halo_scr, prevrow_scr = refs[len(SEGMENTS):]
        it = lax.rem(pl.program_id(0), seq_tiles)
    o_refs = refs[:len(SEGMENTS)]

    def project(n, lo=0, width=None):
        width = SEGMENTS[n] if width is None else width
        seg = jnp.dot(xn, w_ref[:, SEG_OFFSETS[n] + lo:SEG_OFFSETS[n] + lo + width], preferred_element_type=F32)
        o_refs[n][:, lo:lo + width] = seg
        return seg

    if not fused:
        for n in range(len(SEGMENTS)):
            project(n)
        return

    xc = project(4)
    first = jnp.where(it == 0, shift_ref[0], prevrow_scr[...])
    prevrow_scr[...] = xc[tm - 1:tm]
    pi = lax.broadcasted_iota(jnp.int32, (LANES, LANES), 0)
    pj = lax.broadcasted_iota(jnp.int32, (LANES, LANES), 1)
    head_ones = (_shr(pi, HEAD_SHIFT) == _shr(pj, HEAD_SHIFT)).astype(BF16)

    def emit(q, s, p, x):
        if s == N_STAGE:
            gall_ref[q, :, p * LANES:(p + 1) * LANES] = jnp.broadcast_to(x, (8, LANES))
        else:
            st_ref[q * c:(q + 1) * c, s * D_RWKV + p * LANES:s * D_RWKV + (p + 1) * LANES] = x

    def mixer_slices():
        for q in range(tm // c):
            yield from _rwkv_elementwise(q, _ChunkView(xc, c), _PrevRowView(xc, first, c), mu_ref, w0_ref, a0_ref,
                                         lora_ref, kk_ref, ka_ref, rk_ref, head_ones, functools.partial(emit, q), c, c)

    slices = mixer_slices()
    pg_parts = []
    for n in (0, 1, 2, 3, 5):
        for lo in range(0, SEGMENTS[n], FUSE_COLS):
            part = project(n, lo, min(FUSE_COLS, SEGMENTS[n] - lo))
            if n == 0:
                pg_parts.append(part)
            next(slices, None)
        if n == 0:
            pg = jnp.concatenate(pg_parts, axis=1)
            p = pg[:, 0:D_POOL]
            halo = jnp.where(it == 0, hist_ref[0], halo_scr[...])
            halo_scr[...] = p[tm - 16:tm]
            mp_ref[...] = _pool_math(p, pg[:, D_POOL:2 * D_POOL], halo, pos0 + it * tm, pw_ref, ps_ref)
    for _ in slices:
        pass


def _inproj(h2d, norm_w, w_in_bf16, tm, fused=None):
    m = h2d.shape[0]
    c = RWKV_CHUNK
    in_specs = [
        pl.BlockSpec((tm, D_MODEL), lambda i: (i, 0)),
        pl.BlockSpec((1, D_MODEL), lambda i: (0, 0)),
        pl.BlockSpec(memory_space=pltpu.VMEM),
    ]
    out_specs = [pl.BlockSpec((tm, s), lambda i: (i, 0)) for s in SEGMENTS]
    out_shape = [jax.ShapeDtypeStruct((m, s), F32) for s in SEGMENTS]
    args = [h2d, norm_w.reshape(1, D_MODEL), w_in_bf16]
    scratch = []
    seq_tiles = pos0 = 0
    if fused is not None:
        seq_tiles, pos0, lp = fused["t"] // tm, fused["pos0"], fused["lp"]
        b = m // fused["t"]
        vec = lambda n: pl.BlockSpec((1, n), lambda i: (0, 0))
        in_specs += [pl.BlockSpec((1, 16, D_POOL), lambda i: (i // seq_tiles, 0, 0)),
                     pl.BlockSpec((4, POOL_GROUP, POOL_GROUP), lambda i: (0, 0, 0)),
                     vec(D_POOL),
                     pl.BlockSpec((1, 1, D_SHIFT), lambda i: (i // seq_tiles, 0, 0)),
                     vec(D_SHIFT), vec(D_RWKV), vec(D_RWKV),
                     pl.BlockSpec((2 * LORA, 2 * D_RWKV), lambda i: (0, 0)),
                     vec(D_RWKV), vec(D_RWKV), vec(D_RWKV)]
        args += [fused["hist16"], fused["pool_w"], fused["pool_scale"].reshape(1, D_POOL),
                 fused["shift_prev"].reshape(b, 1, D_SHIFT), lp["mu"], lp["w0"], lp["a0"], lp["lora"],
                 lp["k_k"], lp["k_a"], lp["r_k"]]
        out_specs += [pl.BlockSpec((tm, D_POOL), lambda i: (i, 0)),
                      pl.BlockSpec((tm, N_STAGE * D_RWKV), lambda i: (i, 0)),
                      pl.BlockSpec((tm // c, 8, D_RWKV), lambda i: (i, 0, 0))]
        out_shape += [jax.ShapeDtypeStruct((m, D_POOL), BF16),
                      jax.ShapeDtypeStruct((m, N_STAGE * D_RWKV), F32),
                      jax.ShapeDtypeStruct((m // c, 8, D_RWKV), F32)]
        scratch = [pltpu.VMEM((16, D_POOL), F32), pltpu.VMEM((1, D_SHIFT), F32)]
    return pl.pallas_call(
        functools.partial(_inproj_kernel, seq_tiles=seq_tiles, pos0=pos0, c=c),
        grid=(m // tm,),
        in_specs=in_specs,
        out_specs=out_specs,
        out_shape=out_shape,
        scratch_shapes=scratch,
        compiler_params=_cparams(("arbitrary",)),
        name="inproj",
    )(*args)


def _outproj_kernel(h_ref, mp_ref, ma_ref, mr_ref, w_ref, fw_ref, o_ref, *, final):
    acc = jnp.dot(mp_ref[...], w_ref[0:D_POOL, :], preferred_element_type=F32)
    acc += jnp.dot(ma_ref[...], w_ref[D_POOL:D_POOL + D_ATTN, :], preferred_element_type=F32)
    acc += jnp.dot(mr_ref[...], w_ref[D_POOL + D_ATTN:, :], preferred_element_type=F32)
    hn = h_ref[...] + acc
    if final:
        ms = jnp.mean(hn * hn, axis=-1, keepdims=True)
        hn = hn * lax.rsqrt(ms + NORM_EPS) * fw_ref[...]
    o_ref[...] = hn


def _outproj(h2d, mp, ma, mr, w_out_bf16, final_w, final, tm):
    m = h2d.shape[0]
    row = lambda width: pl.BlockSpec((tm, width), lambda i: (i, 0))
    return pl.pallas_call(
        functools.partial(_outproj_kernel, final=final),
        grid=(m // tm,),
        in_specs=[row(D_MODEL), row(D_POOL), row(D_ATTN), row(D_RWKV),
                  pl.BlockSpec(memory_space=pltpu.VMEM),
                  pl.BlockSpec((1, D_MODEL), lambda i: (0, 0))],
        out_specs=row(D_MODEL),
        out_shape=jax.ShapeDtypeStruct((m, D_MODEL), F32),
        compiler_params=_cparams(("arbitrary",)),
        name="outproj",
    )(h2d, mp, ma, mr, w_out_bf16, final_w.reshape(1, D_MODEL))


def _pool_math(p, gate, halo, pos_first, pw_ref, ps_ref):
    tt = p.shape[0]
    ext = jnp.concatenate([halo, p], axis=0)
    pos = pos_first + lax.broadcasted_iota(jnp.int32, (tt, 1), 0)
    outs = []
    for g, w in enumerate(POOL_WINDOWS):
        s = ext[:, g * POOL_GROUP:(g + 1) * POOL_GROUP]
        span = 1
        while span < w:
            n = s.shape[0]
            s = s[span:n] + s[0:n - span]
            span *= 2
        win = s[16 - (w - 1):16 - (w - 1) + tt]
        cnt = jnp.minimum(pos + 1, w).astype(F32)
        d = win / cnt - p[:, g * POOL_GROUP:(g + 1) * POOL_GROUP]
        outs.append(jnp.dot(d.astype(BF16), pw_ref[g], preferred_element_type=F32))
    y = jnp.concatenate(outs, axis=1) * ps_ref[...]
    return (y * _silu(gate)).astype(BF16)


def _pool_kernel(pg_ref, halo_ref, hist_ref, pw_ref, ps_ref, o_ref, *, tt, pos0):
    i = pl.program_id(1)
    halo = jnp.where(i == 0, hist_ref[0], halo_ref[0, :, 0:D_POOL])
    o_ref[0] = _pool_math(pg_ref[0, :, 0:D_POOL], pg_ref[0, :, D_POOL:2 * D_POOL], halo, pos0 + i * tt, pw_ref, ps_ref)


def _pool(pg, hist16, pool_w_bf16, pool_scale, pos0, tt):
    b, t, _ = pg.shape
    nh = tt // 16
    return pl.pallas_call(
        functools.partial(_pool_kernel, tt=tt, pos0=pos0),
        grid=(b, t // tt),
        in_specs=[
            pl.BlockSpec((1, tt, 2 * D_POOL), lambda bi, i: (bi, i, 0)),
            pl.BlockSpec((1, 16, 2 * D_POOL), lambda bi, i: (bi, jnp.maximum(i * nh - 1, 0), 0)),
            pl.BlockSpec((1, 16, D_POOL), lambda bi, i: (bi, 0, 0)),
            pl.BlockSpec((4, POOL_GROUP, POOL_GROUP), lambda bi, i: (0, 0, 0)),
            pl.BlockSpec((1, D_POOL), lambda bi, i: (0, 0)),
        ],
        out_specs=pl.BlockSpec((1, tt, D_POOL), lambda bi, i: (bi, i, 0)),
        out_shape=jax.ShapeDtypeStruct((b, t, D_POOL), BF16),
        compiler_params=_cparams(("arbitrary", "arbitrary")),
        name="pool",
    )(pg, pg, hist16, pool_w_bf16, pool_scale.reshape(1, D_POOL))


def _t5_bucket(rel):
    nb = NUM_BUCKETS // 2
    max_exact = nb // 2
    ret = jnp.where(rel > 0, nb, 0)
    n = jnp.abs(rel)
    nf = jnp.maximum(n, 1).astype(F32)
    large = max_exact + (jnp.log(nf / max_exact) / math.log(MAX_DISTANCE / max_exact) * (nb - max_exact)).astype(jnp.int32)
    large = jnp.minimum(large, nb - 1)
    return ret + jnp.where(n < max_exact, n, large)


def _bias_kernel(bucket_ref, tab_ref, o_ref):
    bucket = bucket_ref[...]
    tab = tab_ref[...]
    out = jnp.full(bucket.shape, NEG, F32)
    for b in range(NUM_BUCKETS):
        out = jnp.where(bucket == b, tab[:, b:b + 1] * LOG2E, out)
    o_ref[...] = out


def _attn_bias(table, tq, nk, prefix_valid):
    qi = jnp.arange(tq)
    kj = jnp.arange(nk)
    rel = kj[None, :] - WINDOW - qi[:, None]
    bucket = _t5_bucket(rel)
    qc = qi // CHUNK
    kc = (kj - WINDOW) // CHUNK
    valid = (kc[None, :] <= qc[:, None]) & (kc[None, :] >= qc[:, None] - WIN_CHUNKS)
    valid &= (kj < WINDOW + tq)[None, :]
    if not prefix_valid:
        valid &= (kj >= WINDOW)[None, :]
    bucket = jnp.where(valid, bucket, -1).astype(jnp.int32)
    heads = jnp.array([[[8 * g + 2 * p + par for p in range(4)] for par in range(2)] for g in range(2)]).reshape(-1)
    rows = heads.shape[0] * tq
    bucket_rows = jnp.broadcast_to(bucket[None], (heads.shape[0], tq, nk)).reshape(rows, nk)
    tab_rows = jnp.broadcast_to(table.astype(F32).T[heads][:, None, :], (heads.shape[0], tq, NUM_BUCKETS)).reshape(rows, NUM_BUCKETS)
    tr = 4 * tq
    return pl.pallas_call(
        _bias_kernel,
        grid=(rows // tr,),
        in_specs=[pl.BlockSpec((tr, nk), lambda i: (i, 0)), pl.BlockSpec((tr, NUM_BUCKETS), lambda i: (i, 0))],
        out_specs=pl.BlockSpec((tr, nk), lambda i: (i, 0)),
        out_shape=jax.ShapeDtypeStruct((rows, nk), F32),
        name="attn_bias",
    )(bucket_rows, tab_rows)


def _sink_rows(sinks, tq):
    heads = jnp.array([[[8 * g + 2 * p + par for p in range(4)] for par in range(2)] for g in range(2)]).reshape(-1)
    return jnp.broadcast_to(sinks.astype(F32)[heads][:, None, None] * LOG2E, (heads.shape[0], tq, LANES)).reshape(-1, LANES)


def _attn_kernel(*refs, tq, has_prev):
    if has_prev:
        q_ref, kvc_ref, kvp_ref, pre_ref, g_ref, bias_ref, sink_ref, o_ref = refs
    else:
        q_ref, kvc_ref, pre_ref, g_ref, bias_ref, sink_ref, o_ref = refs
    i = pl.program_id(1)
    if has_prev:
        kvp = jnp.where(i == 0, pre_ref[0], kvp_ref[0])
    else:
        kvp = pre_ref[0]
    kv = jnp.concatenate([kvp, kvc_ref[0]], axis=0)
    nk = kv.shape[0]
    k = kv[:, 0:LANES] * (HEAD_DIM ** -0.5 * LOG2E)
    v = kv[:, LANES:2 * LANES]
    low = lax.broadcasted_iota(jnp.int32, (nk, LANES), 1) < HEAD_DIM
    k_sw = pltpu.roll(k, HEAD_DIM, axis=1)
    v_sw = pltpu.roll(v, HEAD_DIM, axis=1)
    ones = jnp.ones_like(v)

    def place(x, x_sw, g, par):
        src = x if g == par else x_sw
        return jnp.where(low, src, 0.0) if par == 0 else jnp.where(low, 0.0, src)

    blocks = [(g, par) for g in range(N_KV_HEADS) for par in range(2)]
    kx = [place(k, k_sw, g, par).astype(BF16) for g, par in blocks]
    vx = [jnp.concatenate([place(v, v_sw, g, par), ones], axis=1).astype(BF16) for g, par in blocks]
    qs = [jnp.concatenate([q_ref[0, :, (4 * g + p) * LANES:(4 * g + p + 1) * LANES] for p in range(4)], axis=0).astype(BF16)
          for g in range(N_KV_HEADS)]
    rows = [slice(n * 4 * tq, (n + 1) * 4 * tq) for n in range(len(blocks))]
    s = [_dot(qs[g], kx[n], NT) + bias_ref[rows[n], :] for n, (g, par) in enumerate(blocks)]
    sink = [sink_ref[rows[n], :] for n in range(len(blocks))]
    m = [jnp.maximum(jnp.broadcast_to(jnp.max(s[n], axis=-1, keepdims=True), (4 * tq, LANES)), sink[n])
         for n in range(len(blocks))]
    widen = lambda x: jnp.concatenate([x] * (nk // LANES) + ([x[:, 0:nk % LANES]] if nk % LANES else []), axis=1)
    e = [jnp.exp2(s[n] - widen(m[n])).astype(BF16) for n in range(len(blocks))]
    pv = [_dot(e[n], vx[n]) for n in range(len(blocks))]
    den = [pv[n][:, LANES:] + jnp.exp2(sink[n] - m[n]) for n in range(len(blocks))]
    even_lanes = lax.broadcasted_iota(jnp.int32, (4 * tq, LANES), 1) < HEAD_DIM
    for g in range(N_KV_HEADS):
        acc = (pv[2 * g][:, 0:LANES] + pv[2 * g + 1][:, 0:LANES]) / jnp.where(even_lanes, den[2 * g], den[2 * g + 1])
        for p in range(4):
            cols = slice((4 * g + p) * LANES, (4 * g + p + 1) * LANES)
            o_ref[0, :, cols] = (acc[p * tq:(p + 1) * tq] * _silu(g_ref[0, :, cols])).astype(BF16)


def _attn(q, kv, prefix, gate, bias, sink_rows, tq):
    b, t, _ = q.shape
    n_tiles = t // tq
    has_prev = n_tiles > 1
    nk = WINDOW + tq
    rows = bias.shape[-2]
    in_specs = [
        pl.BlockSpec((1, tq, D_ATTN), lambda bi, i: (bi, i, 0)),
        pl.BlockSpec((1, tq, 4 * HEAD_DIM), lambda bi, i: (bi, i, 0)),
    ]
    args = [q, kv]
    if has_prev:
        in_specs.append(pl.BlockSpec((1, WINDOW, 4 * HEAD_DIM), lambda bi, i: (bi, jnp.maximum(i - 1, 0), 0)))
        args.append(kv)
    in_specs += [
        pl.BlockSpec((1, WINDOW, 4 * HEAD_DIM), lambda bi, i: (bi, 0, 0)),
        pl.BlockSpec((1, tq, D_ATTN), lambda bi, i: (bi, i, 0)),
        pl.BlockSpec((None, rows, nk), lambda bi, i: (jnp.minimum(i, bias.shape[0] - 1), 0, 0)),
        pl.BlockSpec((rows, LANES), lambda bi, i: (0, 0)),
    ]
    args += [prefix, gate, bias, sink_rows]
    return pl.pallas_call(
        functools.partial(_attn_kernel, tq=tq, has_prev=has_prev),
        grid=(b, n_tiles),
        in_specs=in_specs,
        out_specs=pl.BlockSpec((1, tq, D_ATTN), lambda bi, i: (bi, i, 0)),
        out_shape=jax.ShapeDtypeStruct((b, t, D_ATTN), BF16),
        compiler_params=_cparams(("arbitrary", "arbitrary")),
        name="attn",
    )(*args)


def _mm3(a, b):
    return _dot(a[0], b[0]) + (_dot(a[0], b[1]) + _dot(a[1], b[0]))


def _tri_inverse_all(mats, blk_mask, merge_masks, eye, each):
    idx = range(len(mats))
    c = eye.shape[0]
    power = [jnp.where(blk_mask, a, 0.0) for a in mats]
    t = [eye + d for d in power]
    power = each(lambda i: _dot(power[i].astype(BF16), power[i].astype(BF16)), idx)
    span = 2
    while 2 * span < INV_BASE:
        both = each(lambda i: _dot(power[i].astype(BF16),
                                   jnp.concatenate([power[i], t[i]], axis=1).astype(BF16)), idx)
        power = [x[:, 0:c] for x in both]
        t = [t[i] + both[i][:, c:] for i in idx]
        span *= 2
    t = each(lambda i: t[i] + _dot(power[i].astype(BF16), t[i].astype(BF16)), idx)
    for mask in merge_masks:
        tb = [x.astype(BF16) for x in t]
        low = each(lambda i: _dot(tb[i], jnp.where(mask, mats[i], 0.0).astype(BF16)).astype(BF16), idx)
        t = each(lambda i: t[i] + _dot(low[i], tb[i]), idx)
    tb = [x.astype(BF16) for x in t]
    t = [x.astype(F32) for x in tb]

    def residual(i):
        a_hi, a_lo = _split(mats[i], 2)
        return ((eye - t[i]) + (_dot(a_hi, tb[i]) + _dot(a_lo, tb[i]))).astype(BF16)

    resid = each(residual, idx)
    return each(lambda i: t[i] + _dot(tb[i], resid[i]), idx)


N_STAGE = 8


def _rwkv_elementwise(q, xc_ref, prev_scr, mu_ref, w0_ref, a0_ref, lora_ref, kk_ref, ka_ref, rk_ref, head_ones, emit, c, t_valid):
    row = lax.broadcasted_iota(jnp.int32, (c, 1), 0)
    live = row < t_valid
    mask = (lambda x: jnp.where(live, x, 0.0)) if t_valid < c else (lambda x: x)

    def shifted(lo_col, width):
        x = xc_ref[q, :, lo_col:lo_col + width]
        prev = jnp.where(row == 0, prev_scr[q, :, lo_col:lo_col + width], pltpu.roll(x, 1, axis=0))
        return x + (prev - x) * mu_ref[:, lo_col:lo_col + width]

    lo = shifted(3 * D_RWKV, 2 * LORA)
    first_half = lax.broadcasted_iota(jnp.int32, (c, LANES), 1) < RWKV_HEAD
    lora = _mm(jnp.where(first_half, jnp.tanh(lo), lo), lora_ref[...], 3)
    tri = (lax.broadcasted_iota(jnp.int32, (c, c), 0) >= lax.broadcasted_iota(jnp.int32, (c, c), 1)).astype(BF16)
    yield
    for p in range(N_RWKV_HEADS // 2):
        cols = slice(p * LANES, (p + 1) * LANES)
        r = shifted(p * LANES, LANES)
        k = shifted(D_RWKV + p * LANES, LANES)
        v = shifted(2 * D_RWKV + p * LANES, LANES)
        emit(6, p, v)
        z = -(w0_ref[:, cols] + lora[:, cols])
        softplus = jnp.maximum(z, 0.0) + jnp.log(1.0 + jnp.exp(-jnp.abs(z)))
        logw = mask(-jnp.exp(-softplus - 0.5))
        cum = None
        for part in _split(logw, 3):
            d = _dot(tri, part)
            cum = d if cum is None else cum + d
        cum_last = cum[c - 1:c]
        a = _sigmoid(a0_ref[:, cols] + lora[:, D_RWKV + p * LANES:D_RWKV + (p + 1) * LANES])
        kk = k * kk_ref[:, cols]
        kk = kk * jnp.minimum(lax.rsqrt(_dot((kk * kk).astype(BF16), head_ones)), 1e12)
        k = k * (1.0 + (a - 1.0) * ka_ref[:, cols])
        emit(7, p, _dot((r * k * rk_ref[:, cols]).astype(BF16), head_ones) * v)
        k = mask(k)
        alpha = mask(-kk)
        beta = mask(kk * a)
        emit(0, p, r * jnp.exp(cum))
        emit(1, p, alpha * jnp.exp(cum - logw))
        g_inv = jnp.exp(-cum)
        emit(2, p, beta * g_inv)
        emit(3, p, k * g_inv)
        g_rest = jnp.exp(cum_last - cum)
        emit(4, p, beta * g_rest)
        emit(5, p, k * g_rest)
        emit(N_STAGE, p, jnp.exp(cum_last))
        yield


def _rwkv_kernel(*refs, c, t_valid, staged_input):
    if staged_input:
        st_ref, gall_ref, g_ref, p0_ref, lnw_ref, lnb_ref, o_ref, pout_ref, state_scr = refs
    else:
        (xc_ref, g_ref, shift_ref, p0_ref, mu_ref, w0_ref, a0_ref, lora_ref, kk_ref, ka_ref, rk_ref,
         lnw_ref, lnb_ref, o_ref, pout_ref, prev_scr, state_scr) = refs
    i = pl.program_id(1)
    nb = g_ref.shape[0]
    n_pairs = N_RWKV_HEADS // 2

    @pl.when(i == 0)
    def _():
        state_scr[...] = p0_ref[...]
        if not staged_input:
            prev_scr[...] = shift_ref[...]

    units = [(q, p) for q in range(nb) for p in range(n_pairs)]
    nu = range(len(units))
    nh = range(2 * len(units))

    pi = lax.broadcasted_iota(jnp.int32, (LANES, LANES), 0)
    pj = lax.broadcasted_iota(jnp.int32, (LANES, LANES), 1)
    head_block = _shr(pi, HEAD_SHIFT) == _shr(pj, HEAD_SHIFT)
    head_ones = head_block.astype(BF16)

    def head_sum(x):
        stacked = jnp.concatenate([x[:, p * LANES:(p + 1) * LANES] for p in range(n_pairs)], axis=0)
        s = _dot(stacked.astype(BF16), head_ones)
        return jnp.concatenate([s[p * c:(p + 1) * c] for p in range(n_pairs)], axis=1)

    staged = {}

    def emit(q, s, p, x):
        staged[s, q, p] = x

    last_row = t_valid - 1 if t_valid < c else c - 1
    for q in range(nb):
        if staged_input:
            for p in range(n_pairs):
                for s in range(N_STAGE):
                    emit(q, s, p, st_ref[q, :, s * D_RWKV + p * LANES:s * D_RWKV + (p + 1) * LANES])
                emit(q, N_STAGE, p, gall_ref[q, 0, 0:1, p * LANES:(p + 1) * LANES])
        else:
            for _ in _rwkv_elementwise(q, xc_ref, prev_scr, mu_ref, w0_ref, a0_ref, lora_ref, kk_ref, ka_ref, rk_ref,
                                       head_ones, functools.partial(emit, q), c, t_valid):
                pass
            prev_scr[q] = xc_ref[q, last_row:last_row + 1, :]
    rb, ab, bb, kb, bt, kt, vp, bonus, g_all = ([staged[s, q, p] for q, p in units] for s in range(N_STAGE + 1))
    each = lambda fn, items: [fn(x) for x in items]

    first_half = lax.broadcasted_iota(jnp.int32, (c, LANES), 1) < RWKV_HEAD
    ti = lax.broadcasted_iota(jnp.int32, (c, c), 0)
    tj = lax.broadcasted_iota(jnp.int32, (c, c), 1)
    tri_incl = ti >= tj
    tri_strict = ti > tj
    tri_incl2 = jnp.concatenate([tri_incl, tri_incl], axis=1)
    eye = (ti == tj).astype(F32)
    base_shift = INV_BASE.bit_length() - 1
    blk_mask = _shr(ti, base_shift) == _shr(tj, base_shift)
    merge_masks = []
    sh = base_shift
    while (1 << sh) < c:
        merge_masks.append((_shr(ti, sh + 1) == _shr(tj, sh + 1)) & ((_shr(ti, sh) & 1) == 1) & ((_shr(tj, sh) & 1) == 0))
        sh += 1
    diag128 = pi == pj
    zeros_c = jnp.zeros((c, LANES), F32)
    pick = lambda x0, x1: jnp.where(first_half, x0, x1)
    half = lambda x, n: x[(n % 2) * c:(n % 2 + 1) * c]

    bk = [jnp.concatenate([bb[u], kb[u]], axis=0) for u in nu]
    ga = each(lambda u: _mm(jnp.concatenate([pick(ab[u], 0.0), pick(0.0, ab[u])], axis=0), bk[u], 3, NT), nu)
    gr = each(lambda u: _mm(jnp.concatenate([pick(rb[u], 0.0), pick(0.0, rb[u])], axis=0), bk[u], 1, NT), nu)
    a_ab = [jnp.where(tri_strict, half(ga[n // 2], n)[:, 0:c], 0.0) for n in nh]
    a_ak = [jnp.where(tri_strict, half(ga[n // 2], n)[:, c:2 * c], 0.0) for n in nh]
    lr = [jnp.where(tri_incl2, half(gr[n // 2], n), 0.0) for n in nh]
    t_inv = _tri_inverse_all(a_ab, blk_mask, merge_masks, eye, each)
    akv = each(lambda n: _mm(a_ak[n], vp[n // 2]), nh)
    xh = each(lambda n: _mm(t_inv[n], jnp.concatenate([ab[n // 2], akv[n]], axis=1)), nh)
    a_new = [pick(xh[2 * u][:, 0:LANES], xh[2 * u + 1][:, 0:LANES]) for u in nu]
    u0 = [pick(xh[2 * u][:, LANES:], xh[2 * u + 1][:, LANES:]) for u in nu]
    zmat = [jnp.concatenate([jnp.concatenate([a_new[u], u0[u]], axis=1),
                             jnp.concatenate([zeros_c, vp[u]], axis=1)], axis=0).astype(BF16) for u in nu]
    yh = each(lambda n: _dot(lr[n].astype(BF16), zmat[n // 2]), nh)
    mn = each(lambda u: _dot(jnp.concatenate([bt[u], kt[u]], axis=0).astype(BF16), zmat[u], TN), nu)
    o_units = []
    for u, (q, p) in enumerate(units):
        r_new = rb[u] + pick(yh[2 * u][:, 0:LANES], yh[2 * u + 1][:, 0:LANES])
        o0 = pick(yh[2 * u][:, LANES:], yh[2 * u + 1][:, LANES:])
        m_mat = jnp.where(diag128, g_all[u], 0.0) + jnp.where(head_block, mn[u][:, 0:LANES], 0.0)
        n0 = jnp.where(head_block, mn[u][:, LANES:], 0.0)
        state = _split(state_scr[q, p], 2)
        o_units.append(_dot(r_new.astype(BF16), state[0]) + o0)
        state_scr[q, p] = _mm3(_split(m_mat, 2), state) + n0

    inv_n = 1.0 / RWKV_HEAD
    for q in range(nb):
        sl = slice(q * n_pairs, (q + 1) * n_pairs)
        o = jnp.concatenate(o_units[sl], axis=1)
        mean = head_sum(o) * inv_n
        cen = o - mean
        var = head_sum(cen * cen) * inv_n
        y = cen * lax.rsqrt(var + LNX_EPS) * lnw_ref[...] + lnb_ref[...]
        o_ref[q] = ((y + jnp.concatenate(bonus[sl], axis=1)) * _silu(g_ref[q])).astype(BF16)

    @pl.when(i == pl.num_programs(1) - 1)
    def _():
        pout_ref[...] = state_scr[...]


RWKV_SEQS = 2


def _rwkv(gate, p0, lp, c, t_valid, xc=None, shift_prev=None, staged=None):
    b, t, _ = gate.shape
    vec = lambda n: pl.BlockSpec((1, n), lambda bi, i: (0, 0))
    n_pairs = N_RWKV_HEADS // 2
    nb = RWKV_SEQS if b % RWKV_SEQS == 0 else 1
    seq_block = lambda width: pl.BlockSpec((nb, c, width), lambda bi, i: (bi, i, 0))
    state_block = pl.BlockSpec((nb, n_pairs, LANES, LANES), lambda bi, i: (bi, 0, 0, 0))
    scratch = [pltpu.VMEM((nb, n_pairs, LANES, LANES), F32)]
    if staged is not None:
        st, gall = staged
        in_specs = [seq_block(N_STAGE * D_RWKV), pl.BlockSpec((nb, 1, 8, D_RWKV), lambda bi, i: (bi, i, 0, 0)),
                    seq_block(D_RWKV), state_block, vec(D_RWKV), vec(D_RWKV)]
        args = [st.reshape(b, t, N_STAGE * D_RWKV), gall.reshape(b, t // c, 8, D_RWKV), gate, p0, lp["lnx_w"], lp["lnx_b"]]
    else:
        in_specs = [seq_block(D_SHIFT), seq_block(D_RWKV), pl.BlockSpec((nb, 1, D_SHIFT), lambda bi, i: (bi, 0, 0)),
                    state_block, vec(D_SHIFT), vec(D_RWKV), vec(D_RWKV),
                    pl.BlockSpec((2 * LORA, 2 * D_RWKV), lambda bi, i: (0, 0)),
                    vec(D_RWKV), vec(D_RWKV), vec(D_RWKV), vec(D_RWKV), vec(D_RWKV)]
        args = [xc, gate, shift_prev.reshape(b, 1, D_SHIFT), p0, lp["mu"], lp["w0"], lp["a0"], lp["lora"],
                lp["k_k"], lp["k_a"], lp["r_k"], lp["lnx_w"], lp["lnx_b"]]
        scratch = [pltpu.VMEM((nb, 1, D_SHIFT), F32)] + scratch
    return pl.pallas_call(
        functools.partial(_rwkv_kernel, c=c, t_valid=t_valid, staged_input=staged is not None),
        grid=(b // nb, t // c),
        in_specs=in_specs,
        out_specs=[seq_block(D_RWKV), state_block],
        out_shape=[jax.ShapeDtypeStruct((b, t, D_RWKV), BF16),
                   jax.ShapeDtypeStruct((b, n_pairs, LANES, LANES), F32)],
        scratch_shapes=scratch,
        compiler_params=_cparams(("arbitrary", "arbitrary")),
        name="rwkv",
    )(*args)


def _state_to_pairs(s):
    b = s.shape[0]
    pt = jnp.swapaxes(s, -1, -2).reshape(b, N_RWKV_HEADS // 2, 2, RWKV_HEAD, RWKV_HEAD)
    z = jnp.zeros_like(pt[:, :, 0])
    top = jnp.concatenate([pt[:, :, 0], z], axis=-1)
    bot = jnp.concatenate([z, pt[:, :, 1]], axis=-1)
    return jnp.concatenate([top, bot], axis=-2)


def _pairs_to_state(pm):
    b = pm.shape[0]
    h0 = pm[:, :, 0:RWKV_HEAD, 0:RWKV_HEAD]
    h1 = pm[:, :, RWKV_HEAD:, RWKV_HEAD:]
    pt = jnp.stack([h0, h1], axis=2).reshape(b, N_RWKV_HEADS, RWKV_HEAD, RWKV_HEAD)
    return jnp.swapaxes(pt, -1, -2)


def _layer(h, lp, att, pool_hist, shift_prev, wkv0, kv_prefix, pos0, final_w, final, cfg):
    b, t, _ = h.shape
    hist16 = jnp.concatenate([jnp.zeros((b, 1, D_POOL), F32), pool_hist], axis=1)
    c = RWKV_CHUNK
    tm = cfg["tm_in"]
    fuse = t % tm == 0 and tm % c == 0
    fused = dict(t=t, pos0=pos0, hist16=hist16, pool_w=lp["pool_w"], pool_scale=lp["pool_scale"],
                 shift_prev=shift_prev, lp=lp) if fuse else None
    outs = _inproj(h.reshape(b * t, D_MODEL), lp["norm_w"], lp["w_in"], tm, fused)
    shape3 = lambda x: x.reshape(b, t, x.shape[-1])
    pg, q, kv, ga, xc, gr = (shape3(x) for x in outs[:len(SEGMENTS)])
    ma = _attn(q, kv, kv_prefix, ga, att["bias"], _sink_rows(lp["sinks"], cfg["tq"]), cfg["tq"])
    p0 = _state_to_pairs(wkv0)
    if fuse:
        mp = shape3(outs[len(SEGMENTS)])
        mr, p_new = _rwkv(gr, p0, lp, c, c, staged=outs[len(SEGMENTS) + 1:])
    else:
        mp = _pool(pg, hist16, lp["pool_w"], lp["pool_scale"], pos0, cfg["tt_pool"])
        pad = (-t) % c
        xc_in = jnp.pad(xc, ((0, 0), (0, pad), (0, 0)))
        gr_in = jnp.pad(gr, ((0, 0), (0, pad), (0, 0)))
        mr, p_new = _rwkv(gr_in, p0, lp, c, t if pad else c, xc=xc_in, shift_prev=shift_prev)
        mr = mr[:, :t]

    flat = lambda x: x.reshape(b * t, x.shape[-1])
    h_new = _outproj(flat(h), flat(mp), flat(ma), flat(mr), lp["w_out"], final_w, final, cfg["tm_out"])
    new_pool = pg[:, -POOL_HIST:, 0:D_POOL]
    kvf = kv[:, -WINDOW:] if t >= WINDOW else jnp.concatenate([kv_prefix[:, t:], kv], axis=1)
    new_k = kvf[:, :, 0:LANES].reshape(b, WINDOW, N_KV_HEADS, HEAD_DIM)
    new_v = kvf[:, :, LANES:].reshape(b, WINDOW, N_KV_HEADS, HEAD_DIM)
    new_shift = xc[:, -1]
    return h_new.reshape(b, t, D_MODEL), (new_pool, new_k, new_v, new_shift, _pairs_to_state(p_new))


def _group_cfg(b, t):
    m = b * t
    tq = min(t, ATT_TILE)
    return {"tm_in": min(m, 256), "tm_out": min(m, 512), "tt_pool": min(t, 512), "tq": tq}


def kernel(x_prompt, x_sample, state_pool, cache_swa_k, cache_swa_v, state_rwkv_shift, state_rwkv_wkv, norm_w, w_in, w_out, pool_w, pool_scale, attn_sinks, rel_bias_table, rwkv_mu, rwkv_w0, rwkv_w_up, rwkv_a0, rwkv_a_up, rwkv_k_k, rwkv_k_a, rwkv_r_k, rwkv_lnx_w, rwkv_lnx_b, final_norm_w):
    bp, tp, _ = x_prompt.shape
    bs, ts, _ = x_sample.shape
    cfg_p = _group_cfg(bp, tp)
    cfg_s = _group_cfg(bs, ts)
    att_p = {"bias": jnp.stack([_attn_bias(rel_bias_table, cfg_p["tq"], WINDOW + cfg_p["tq"], False),
                                _attn_bias(rel_bias_table, cfg_p["tq"], WINDOW + cfg_p["tq"], True)])}
    att_s = {"bias": _attn_bias(rel_bias_table, cfg_s["tq"], WINDOW + cfg_s["tq"], True)[None]}

    hp, hs = x_prompt, x_sample
    prompt_states, sample_states = [], []
    row = lambda x: x.astype(F32).reshape(1, -1)
    zero_lora = jnp.zeros((LORA, D_RWKV), F32)
    for l in range(DEPTH):
        lp = {
            "norm_w": norm_w[l], "w_in": _layer_bf16(w_in, l), "w_out": _layer_bf16(w_out, l),
            "pool_w": pool_w[l].astype(BF16), "pool_scale": pool_scale[l], "sinks": attn_sinks[l],
            "mu": row(rwkv_mu[l]), "w0": row(rwkv_w0[l]), "a0": row(rwkv_a0[l]),
            "lora": jnp.concatenate([jnp.concatenate([rwkv_w_up[l].astype(F32), zero_lora], axis=1),
                                     jnp.concatenate([zero_lora, rwkv_a_up[l].astype(F32)], axis=1)], axis=0),
            "k_k": row(rwkv_k_k[l]), "k_a": row(rwkv_k_a[l]), "r_k": row(rwkv_r_k[l]),
            "lnx_w": row(rwkv_lnx_w[l]), "lnx_b": row(rwkv_lnx_b[l]),
        }
        final = l == DEPTH - 1
        hp, sp = _layer(hp, lp, att_p, jnp.zeros((bp, POOL_HIST, D_POOL), F32), jnp.zeros((bp, D_SHIFT), F32),
                        jnp.zeros((bp, N_RWKV_HEADS, RWKV_HEAD, RWKV_HEAD), F32),
                        jnp.zeros((bp, WINDOW, 4 * HEAD_DIM), F32), 0, final_norm_w, final, cfg_p)
        prefix_s = jnp.concatenate([cache_swa_k[l].reshape(bs, WINDOW, LANES), cache_swa_v[l].reshape(bs, WINDOW, LANES)], axis=-1)
        hs, ss = _layer(hs, lp, att_s, state_pool[l], state_rwkv_shift[l], state_rwkv_wkv[l], prefix_s, PAST_LEN,
                        final_norm_w, final, cfg_s)
        prompt_states.append(sp)
        sample_states.append(ss)
    outs_p = [jnp.stack(x) for x in zip(*prompt_states)]
    outs_s = [jnp.stack(x) for x in zip(*sample_states)]
    return (hp, hs, *outs_p, *outs_s)
```

```python
import functools
import math

import jax
import jax.numpy as jnp
from jax import lax
from jax.experimental import pallas as pl
from jax.experimental.pallas import tpu as pltpu

F32 = jnp.float32
BF16 = jnp.bfloat16

D_MODEL = 2048
DEPTH = 4
PAST_LEN = 1024
CHUNK = 64
D_POOL = 512
POOL_WINDOWS = (2, 4, 8, 16)
POOL_GROUP = 128
POOL_HIST = 15
HEAD_DIM = 64
D_ATTN = 1024
N_Q_HEADS = 16
N_KV_HEADS = 2
WINDOW = 128
WIN_CHUNKS = 2
NUM_BUCKETS = 32
MAX_DISTANCE = 128
NEG = -1e30
LOG2E = 1.0 / math.log(2.0)
D_RWKV = 512
RWKV_HEAD = 64
N_RWKV_HEADS = 8
LORA = 64
D_SHIFT = 3 * D_RWKV + 2 * LORA
NORM_EPS = 1e-5
LNX_EPS = 1e-5 * RWKV_HEAD
SEGMENTS = (2 * D_POOL, D_ATTN, 2 * N_KV_HEADS * HEAD_DIM, D_ATTN, D_SHIFT, D_RWKV)
D_IN = sum(SEGMENTS)

LANES = 128
ATT_TILE = 128
RWKV_CHUNK = 128
INV_BASE = 16
VMEM_LIMIT = 56 * 1024 * 1024


def _cparams(sem):
    return pltpu.CompilerParams(dimension_semantics=sem, vmem_limit_bytes=VMEM_LIMIT)


def _sigmoid(x):
    return 0.5 * jnp.tanh(0.5 * x) + 0.5


def _silu(x):
    return x * _sigmoid(x)


def _split(x, terms):
    parts = []
    rem = x
    for _ in range(terms):
        p = rem.astype(BF16)
        parts.append(p)
        rem = rem - p.astype(F32)
    return parts


def _dot(a, b, dims=None):
    if dims is None:
        return jnp.dot(a, b, preferred_element_type=F32)
    return lax.dot_general(a, b, (dims, ((), ())), preferred_element_type=F32)


def _mm(a, b, passes=1, dims=None):
    if passes == 1:
        return _dot(a.astype(BF16), b.astype(BF16), dims)
    ah, al = _split(a, 2)
    bh, bl = _split(b, 2)
    return _dot(ah, bh, dims) + (_dot(ah, bl, dims) + _dot(al, bh, dims))


def _mm_exact_rhs(a, b_bf16, terms=3):
    out = None
    for p in _split(a, terms):
        d = _dot(p, b_bf16)
        out = d if out is None else out + d
    return out


HEAD_SHIFT = RWKV_HEAD.bit_length() - 1


def _shr(x, bits):
    return lax.shift_right_logical(x, jnp.full_like(x, bits))


NT = ((1,), (1,))
TN = ((0,), (0,))


CAST_ROWS = 256


def _cast_kernel(w_ref, o_ref):
    o_ref[...] = w_ref[...].astype(BF16)


def _layer_bf16(w, l):
    _, rows, cols = w.shape
    return pl.pallas_call(
        _cast_kernel,
        grid=(rows // CAST_ROWS,),
        in_specs=[pl.BlockSpec((None, CAST_ROWS, cols), lambda i: (l, i, 0))],
        out_specs=pl.BlockSpec((CAST_ROWS, cols), lambda i: (i, 0)),
        out_shape=jax.ShapeDtypeStruct((rows, cols), BF16),
        compiler_params=_cparams(("arbitrary",)),
        name="cast_bf16",
    )(w)


SEG_OFFSETS = tuple(sum(SEGMENTS[:n]) for n in range(len(SEGMENTS)))
FUSE_COLS = 512


class _ChunkView:
    def __init__(self, x, c):
        self.x, self.c = x, c

    def __getitem__(self, idx):
        q, _, cols = idx
        return self.x[q * self.c:(q + 1) * self.c, cols]


class _PrevRowView:
    def __init__(self, x, first, c):
        self.x, self.first, self.c = x, first, c

    def __getitem__(self, idx):
        q, _, cols = idx
        return self.first[:, cols] if q == 0 else self.x[q * self.c - 1:q * self.c, cols]


def _inproj_kernel(h_ref, nw_ref, w_ref, *refs, seq_tiles, pos0, c):
    x = h_ref[...]
    ms = jnp.mean(x * x, axis=-1, keepdims=True)
    xn = (x * lax.rsqrt(ms + NORM_EPS) * nw_ref[...]).astype(BF16)
    tm = x.shape[0]
    fused = seq_tiles > 0
    if fused:
        hist_ref, pw_ref, ps_ref, shift_ref, mu_ref, w0_ref, a0_ref, lora_ref, kk_ref, ka_ref, rk_ref, *refs = refs
        mp_ref, st_ref, gall_ref, halo_scr, prevrow_scr = refs[len(SEGMENTS):]
        it = lax.rem(pl.program_id(0), seq_tiles)
    o_refs = refs[:len(SEGMENTS)]

    def project(n, lo=0, width=None):
        width = SEGMENTS[n] if width is None else width
        seg = jnp.dot(xn, w_ref[:, SEG_OFFSETS[n] + lo:SEG_OFFSETS[n] + lo + width], preferred_element_type=F32)
        o_refs[n][:, lo:lo + width] = seg
        return seg

    if not fused:
        for n in range(len(SEGMENTS)):
            project(n)
        return

    xc = project(4)
    first = jnp.where(it == 0, shift_ref[0], prevrow_scr[...])
    prevrow_scr[...] = xc[tm - 1:tm]
    pi = lax.broadcasted_iota(jnp.int32, (LANES, LANES), 0)
    pj = lax.broadcasted_iota(jnp.int32, (LANES, LANES), 1)
    head_ones = (_shr(pi, HEAD_SHIFT) == _shr(pj, HEAD_SHIFT)).astype(BF16)

    def emit(q, s, p, x):
        if s == N_STAGE:
            gall_ref[q, :, p * LANES:(p + 1) * LANES] = jnp.broadcast_to(x, (8, LANES))
        else:
            st_ref[q * c:(q + 1) * c, s * D_RWKV + p * LANES:s * D_RWKV + (p + 1) * LANES] = x

    def mixer_slices():
        for q in range(tm // c):
            yield from _rwkv_elementwise(q, _ChunkView(xc, c), _PrevRowView(xc, first, c), mu_ref, w0_ref, a0_ref,
                                         lora_ref, kk_ref, ka_ref, rk_ref, head_ones, functools.partial(emit, q), c, c)

    slices = mixer_slices()
    pg_parts = []
    for n in (0, 1, 2, 3, 5):
        for lo in range(0, SEGMENTS[n], FUSE_COLS):
            part = project(n, lo, min(FUSE_COLS, SEGMENTS[n] - lo))
            if n == 0:
                pg_parts.append(part)
            next(slices, None)
        if n == 0:
            pg = jnp.concatenate(pg_parts, axis=1)
            p = pg[:, 0:D_POOL]
            halo = jnp.where(it == 0, hist_ref[0], halo_scr[...])
            halo_scr[...] = p[tm - 16:tm]
            mp_ref[...] = _pool_math(p, pg[:, D_POOL:2 * D_POOL], halo, pos0 + it * tm, pw_ref, ps_ref)
    for _ in slices:
        pass


def _inproj(h2d, norm_w, w_in_bf16, tm, fused=None):
    m = h2d.shape[0]
    c = RWKV_CHUNK
    in_specs = [
        pl.BlockSpec((tm, D_MODEL), lambda i: (i, 0)),
        pl.BlockSpec((1, D_MODEL), lambda i: (0, 0)),
        pl.BlockSpec(memory_space=pltpu.VMEM),
    ]
    out_specs = [pl.BlockSpec((tm, s), lambda i: (i, 0)) for s in SEGMENTS]
    out_shape = [jax.ShapeDtypeStruct((m, s), F32) for s in SEGMENTS]
    args = [h2d, norm_w.reshape(1, D_MODEL), w_in_bf16]
    scratch = []
    seq_tiles = pos0 = 0
    if fused is not None:
        seq_tiles, pos0, lp = fused["t"] // tm, fused["pos0"], fused["lp"]
        b = m // fused["t"]
        vec = lambda n: pl.BlockSpec((1, n), lambda i: (0, 0))
        in_specs += [pl.BlockSpec((1, 16, D_POOL), lambda i: (i // seq_tiles, 0, 0)),
                     pl.BlockSpec((4, POOL_GROUP, POOL_GROUP), lambda i: (0, 0, 0)),
                     vec(D_POOL),
                     pl.BlockSpec((1, 1, D_SHIFT), lambda i: (i // seq_tiles, 0, 0)),
                     vec(D_SHIFT), vec(D_RWKV), vec(D_RWKV),
                     pl.BlockSpec((2 * LORA, 2 * D_RWKV), lambda i: (0, 0)),
                     vec(D_RWKV), vec(D_RWKV), vec(D_RWKV)]
        args += [fused["hist16"], fused["pool_w"], fused["pool_scale"].reshape(1, D_POOL),
                 fused["shift_prev"].reshape(b, 1, D_SHIFT), lp["mu"], lp["w0"], lp["a0"], lp["lora"],
                 lp["k_k"], lp["k_a"], lp["r_k"]]
        out_specs += [pl.BlockSpec((tm, D_POOL), lambda i: (i, 0)),
                      pl.BlockSpec((tm, N_STAGE * D_RWKV), lambda i: (i, 0)),
                      pl.BlockSpec((tm // c, 8, D_RWKV), lambda i: (i, 0, 0))]
        out_shape += [jax.ShapeDtypeStruct((m, D_POOL), BF16),
                      jax.ShapeDtypeStruct((m, N_STAGE * D_RWKV), F32),
                      jax.ShapeDtypeStruct((m // c, 8, D_RWKV), F32)]
        scratch = [pltpu.VMEM((16, D_POOL), F32), pltpu.VMEM((1, D_SHIFT), F32)]
    return pl.pallas_call(
        functools.partial(_inproj_kernel, seq_tiles=seq_tiles, pos0=pos0, c=c),
        grid=(m // tm,),
        in_specs=in_specs,
        out_specs=out_specs,
        out_shape=out_shape,
        scratch_shapes=scratch,
        compiler_params=_cparams(("arbitrary",)),
        name="inproj",
    )(*args)


OUT_COLS = 512
ATTN_SLICES = 7


def _outproj_kernel(*refs, final, attn_tiles, tq):
    if not attn_tiles:
        h_ref, mp_ref, ma_ref, mr_ref, w_ref, fw_ref, o_ref = refs
        acc = jnp.dot(mp_ref[...], w_ref[0:D_POOL, :], preferred_element_type=F32)
        acc += jnp.dot(ma_ref[...], w_ref[D_POOL:D_POOL + D_ATTN, :], preferred_element_type=F32)
        acc += jnp.dot(mr_ref[...], w_ref[D_POOL + D_ATTN:, :], preferred_element_type=F32)
        hn = h_ref[...] + acc
    else:
        h_ref, mp_ref, mr_ref, w_ref, fw_ref, q_ref, kv_ref, kvp_ref, pre_ref, g_ref, bias_ref, sink_ref, o_ref, ma_scr = refs
        tm = h_ref.shape[0]
        it = lax.rem(pl.program_id(0), attn_tiles)

        def attention_slices():
            for j in range(tm // tq):
                r = slice(j * tq, (j + 1) * tq)
                if j == 0:
                    kvp = jnp.where(it == 0, pre_ref[0], kvp_ref[...])
                    variant = jnp.minimum(it, 1)
                    bias_get = lambda rows, variant=variant: bias_ref[variant, rows, :]
                else:
                    kvp = kv_ref[j * tq - WINDOW:j * tq, :]
                    bias_get = lambda rows: bias_ref[1, rows, :]

                def put(cols, x, r=r):
                    ma_scr[r, cols] = x

                yield from _attn_tile(lambda cols, r=r: q_ref[r, cols], kvp, kv_ref[r, :], lambda cols, r=r: g_ref[r, cols],
                                      bias_get, sink_ref, put, tq)

        slices = attention_slices()
        blocks = [slice(j * OUT_COLS, (j + 1) * OUT_COLS) for j in range(D_MODEL // OUT_COLS)]
        per_block = -(-(tm // tq) * ATTN_SLICES // (2 * len(blocks)))
        acc = []
        for cols in blocks:
            part = jnp.dot(mp_ref[...], w_ref[0:D_POOL, cols], preferred_element_type=F32)
            for _ in range(per_block):
                next(slices, None)
            part += jnp.dot(mr_ref[...], w_ref[D_POOL + D_ATTN:, cols], preferred_element_type=F32)
            for _ in range(per_block):
                next(slices, None)
            acc.append(part)
        for _ in slices:
            pass
        ma = ma_scr[...]
        hn = jnp.concatenate([h_ref[:, cols] + acc[j] + jnp.dot(ma, w_ref[D_POOL:D_POOL + D_ATTN, cols], preferred_element_type=F32)
                              for j, cols in enumerate(blocks)], axis=1)
    if final:
        ms = jnp.mean(hn * hn, axis=-1, keepdims=True)
        hn = hn * lax.rsqrt(ms + NORM_EPS) * fw_ref[...]
    o_ref[...] = hn


def _outproj(h2d, mp, ma, mr, w_out_bf16, final_w, final, tm, attn=None):
    m = h2d.shape[0]
    row = lambda width: pl.BlockSpec((tm, width), lambda i: (i, 0))
    whole = pl.BlockSpec(memory_space=pltpu.VMEM)
    fw_spec = pl.BlockSpec((1, D_MODEL), lambda i: (0, 0))
    if attn is None:
        in_specs = [row(D_MODEL), row(D_POOL), row(D_ATTN), row(D_RWKV), whole, fw_spec]
        args = [h2d, mp, ma, mr, w_out_bf16, final_w.reshape(1, D_MODEL)]
        scratch, attn_tiles, tq = [], 0, 0
    else:
        tq, attn_tiles = attn["tq"], attn["t"] // tm
        per = tm // WINDOW
        in_specs = [row(D_MODEL), row(D_POOL), row(D_RWKV), whole, fw_spec, row(D_ATTN), row(4 * HEAD_DIM),
                    pl.BlockSpec((WINDOW, 4 * HEAD_DIM), lambda i: (jnp.maximum(i * per - 1, 0), 0)),
                    pl.BlockSpec((1, WINDOW, 4 * HEAD_DIM), lambda i: (i // attn_tiles, 0, 0)),
                    row(D_ATTN), whole, whole]
        args = [h2d, mp, mr, w_out_bf16, final_w.reshape(1, D_MODEL), attn["q"], attn["kv"], attn["kv"], attn["prefix"],
                attn["gate"], attn["bias"], attn["sink_rows"]]
        scratch = [pltpu.VMEM((tm, D_ATTN), BF16)]
    return pl.pallas_call(
        functools.partial(_outproj_kernel, final=final, attn_tiles=attn_tiles, tq=tq),
        grid=(m // tm,),
        in_specs=in_specs,
        out_specs=row(D_MODEL),
        out_shape=jax.ShapeDtypeStruct((m, D_MODEL), F32),
        scratch_shapes=scratch,
        compiler_params=_cparams(("arbitrary",)),
        name="outproj",
    )(*args)


def _pool_math(p, gate, halo, pos_first, pw_ref, ps_ref):
    tt = p.shape[0]
    ext = jnp.concatenate([halo, p], axis=0)
    pos = pos_first + lax.broadcasted_iota(jnp.int32, (tt, 1), 0)
    outs = []
    for g, w in enumerate(POOL_WINDOWS):
        s = ext[:, g * POOL_GROUP:(g + 1) * POOL_GROUP]
        span = 1
        while span < w:
            n = s.shape[0]
            s = s[span:n] + s[0:n - span]
            span *= 2
        win = s[16 - (w - 1):16 - (w - 1) + tt]
        cnt = jnp.minimum(pos + 1, w).astype(F32)
        d = win / cnt - p[:, g * POOL_GROUP:(g + 1) * POOL_GROUP]
        outs.append(jnp.dot(d.astype(BF16), pw_ref[g], preferred_element_type=F32))
    y = jnp.concatenate(outs, axis=1) * ps_ref[...]
    return (y * _silu(gate)).astype(BF16)


def _pool_kernel(pg_ref, halo_ref, hist_ref, pw_ref, ps_ref, o_ref, *, tt, pos0):
    i = pl.program_id(1)
    halo = jnp.where(i == 0, hist_ref[0], halo_ref[0, :, 0:D_POOL])
    o_ref[0] = _pool_math(pg_ref[0, :, 0:D_POOL], pg_ref[0, :, D_POOL:2 * D_POOL], halo, pos0 + i * tt, pw_ref, ps_ref)


def _pool(pg, hist16, pool_w_bf16, pool_scale, pos0, tt):
    b, t, _ = pg.shape
    nh = tt // 16
    return pl.pallas_call(
        functools.partial(_pool_kernel, tt=tt, pos0=pos0),
        grid=(b, t // tt),
        in_specs=[
            pl.BlockSpec((1, tt, 2 * D_POOL), lambda bi, i: (bi, i, 0)),
            pl.BlockSpec((1, 16, 2 * D_POOL), lambda bi, i: (bi, jnp.maximum(i * nh - 1, 0), 0)),
            pl.BlockSpec((1, 16, D_POOL), lambda bi, i: (bi, 0, 0)),
            pl.BlockSpec((4, POOL_GROUP, POOL_GROUP), lambda bi, i: (0, 0, 0)),
            pl.BlockSpec((1, D_POOL), lambda bi, i: (0, 0)),
        ],
        out_specs=pl.BlockSpec((1, tt, D_POOL), lambda bi, i: (bi, i, 0)),
        out_shape=jax.ShapeDtypeStruct((b, t, D_POOL), BF16),
        compiler_params=_cparams(("arbitrary", "arbitrary")),
        name="pool",
    )(pg, pg, hist16, pool_w_bf16, pool_scale.reshape(1, D_POOL))


def _t5_bucket(rel):
    nb = NUM_BUCKETS // 2
    max_exact = nb // 2
    ret = jnp.where(rel > 0, nb, 0)
    n = jnp.abs(rel)
    nf = jnp.maximum(n, 1).astype(F32)
    large = max_exact + (jnp.log(nf / max_exact) / math.log(MAX_DISTANCE / max_exact) * (nb - max_exact)).astype(jnp.int32)
    large = jnp.minimum(large, nb - 1)
    return ret + jnp.where(n < max_exact, n, large)


def _bias_kernel(bucket_ref, tab_ref, o_ref):
    bucket = bucket_ref[...]
    tab = tab_ref[...]
    out = jnp.full(bucket.shape, NEG, F32)
    for b in range(NUM_BUCKETS):
        out = jnp.where(bucket == b, tab[:, b:b + 1] * LOG2E, out)
    o_ref[...] = out


def _attn_bias(table, tq, nk, prefix_valid):
    qi = jnp.arange(tq)
    kj = jnp.arange(nk)
    rel = kj[None, :] - WINDOW - qi[:, None]
    bucket = _t5_bucket(rel)
    qc = qi // CHUNK
    kc = (kj - WINDOW) // CHUNK
    valid = (kc[None, :] <= qc[:, None]) & (kc[None, :] >= qc[:, None] - WIN_CHUNKS)
    valid &= (kj < WINDOW + tq)[None, :]
    if not prefix_valid:
        valid &= (kj >= WINDOW)[None, :]
    bucket = jnp.where(valid, bucket, -1).astype(jnp.int32)
    heads = jnp.array([[[8 * g + 2 * p + par for p in range(4)] for par in range(2)] for g in range(2)]).reshape(-1)
    rows = heads.shape[0] * tq
    bucket_rows = jnp.broadcast_to(bucket[None], (heads.shape[0], tq, nk)).reshape(rows, nk)
    tab_rows = jnp.broadcast_to(table.astype(F32).T[heads][:, None, :], (heads.shape[0], tq, NUM_BUCKETS)).reshape(rows, NUM_BUCKETS)
    tr = 4 * tq
    return pl.pallas_call(
        _bias_kernel,
        grid=(rows // tr,),
        in_specs=[pl.BlockSpec((tr, nk), lambda i: (i, 0)), pl.BlockSpec((tr, NUM_BUCKETS), lambda i: (i, 0))],
        out_specs=pl.BlockSpec((tr, nk), lambda i: (i, 0)),
        out_shape=jax.ShapeDtypeStruct((rows, nk), F32),
        name="attn_bias",
    )(bucket_rows, tab_rows)


def _sink_rows(sinks, tq):
    heads = jnp.array([[[8 * g + 2 * p + par for p in range(4)] for par in range(2)] for g in range(2)]).reshape(-1)
    return jnp.broadcast_to(sinks.astype(F32)[heads][:, None, None] * LOG2E, (heads.shape[0], tq, LANES)).reshape(-1, LANES)


def _attn_tile(q_get, kvp, kvc, gate_get, bias_get, sink_ref, put, tq):
    kv = jnp.concatenate([kvp, kvc], axis=0)
    nk = kv.shape[0]
    k = kv[:, 0:LANES] * (HEAD_DIM ** -0.5 * LOG2E)
    v = kv[:, LANES:2 * LANES]
    low = lax.broadcasted_iota(jnp.int32, (nk, LANES), 1) < HEAD_DIM
    k_sw = pltpu.roll(k, HEAD_DIM, axis=1)
    v_sw = pltpu.roll(v, HEAD_DIM, axis=1)
    ones = jnp.ones_like(v)

    def place(x, x_sw, g, par):
        src = x if g == par else x_sw
        return jnp.where(low, src, 0.0) if par == 0 else jnp.where(low, 0.0, src)

    blocks = [(g, par) for g in range(N_KV_HEADS) for par in range(2)]
    nblk = range(len(blocks))
    kx = [place(k, k_sw, g, par).astype(BF16) for g, par in blocks]
    vx = [jnp.concatenate([place(v, v_sw, g, par), ones], axis=1).astype(BF16) for g, par in blocks]
    qs = [jnp.concatenate([q_get(slice((4 * g + p) * LANES, (4 * g + p + 1) * LANES)) for p in range(4)], axis=0).astype(BF16)
          for g in range(N_KV_HEADS)]
    rows = [slice(n * 4 * tq, (n + 1) * 4 * tq) for n in nblk]
    yield
    s = [_dot(qs[g], kx[n], NT) + bias_get(rows[n]) for n, (g, par) in enumerate(blocks)]
    yield
    sink = [sink_ref[rows[n], :] for n in nblk]
    m = [jnp.maximum(jnp.broadcast_to(jnp.max(s[n], axis=-1, keepdims=True), (4 * tq, LANES)), sink[n]) for n in nblk]
    yield
    widen = lambda x: jnp.concatenate([x] * (nk // LANES) + ([x[:, 0:nk % LANES]] if nk % LANES else []), axis=1)
    e = [jnp.exp2(s[n] - widen(m[n])).astype(BF16) for n in nblk]
    yield
    pv = [_dot(e[n], vx[n]) for n in nblk]
    den = [pv[n][:, LANES:] + jnp.exp2(sink[n] - m[n]) for n in nblk]
    yield
    even_lanes = lax.broadcasted_iota(jnp.int32, (4 * tq, LANES), 1) < HEAD_DIM
    for g in range(N_KV_HEADS):
        acc = (pv[2 * g][:, 0:LANES] + pv[2 * g + 1][:, 0:LANES]) / jnp.where(even_lanes, den[2 * g], den[2 * g + 1])
        for p in range(4):
            cols = slice((4 * g + p) * LANES, (4 * g + p + 1) * LANES)
            put(cols, (acc[p * tq:(p + 1) * tq] * _silu(gate_get(cols))).astype(BF16))
        yield


def _attn_kernel(*refs, tq, has_prev):
    if has_prev:
        q_ref, kvc_ref, kvp_ref, pre_ref, g_ref, bias_ref, sink_ref, o_ref = refs
    else:
        q_ref, kvc_ref, pre_ref, g_ref, bias_ref, sink_ref, o_ref = refs
    i = pl.program_id(1)
    kvp = jnp.where(i == 0, pre_ref[0], kvp_ref[0]) if has_prev else pre_ref[0]

    def put(cols, x):
        o_ref[0, :, cols] = x

    for _ in _attn_tile(lambda cols: q_ref[0, :, cols], kvp, kvc_ref[0], lambda cols: g_ref[0, :, cols],
                        lambda rows: bias_ref[rows, :], sink_ref, put, tq):
        pass


def _attn(q, kv, prefix, gate, bias, sink_rows, tq):
    b, t, _ = q.shape
    n_tiles = t // tq
    has_prev = n_tiles > 1
    nk = WINDOW + tq
    rows = bias.shape[-2]
    in_specs = [
        pl.BlockSpec((1, tq, D_ATTN), lambda bi, i: (bi, i, 0)),
        pl.BlockSpec((1, tq, 4 * HEAD_DIM), lambda bi, i: (bi, i, 0)),
    ]
    args = [q, kv]
    if has_prev:
        in_specs.append(pl.BlockSpec((1, WINDOW, 4 * HEAD_DIM), lambda bi, i: (bi, jnp.maximum(i - 1, 0), 0)))
        args.append(kv)
    in_specs += [
        pl.BlockSpec((1, WINDOW, 4 * HEAD_DIM), lambda bi, i: (bi, 0, 0)),
        pl.BlockSpec((1, tq, D_ATTN), lambda bi, i: (bi, i, 0)),
        pl.BlockSpec((None, rows, nk), lambda bi, i: (jnp.minimum(i, bias.shape[0] - 1), 0, 0)),
        pl.BlockSpec((rows, LANES), lambda bi, i: (0, 0)),
    ]
    args += [prefix, gate, bias, sink_rows]
    return pl.pallas_call(
        functools.partial(_attn_kernel, tq=tq, has_prev=has_prev),
        grid=(b, n_tiles),
        in_specs=in_specs,
        out_specs=pl.BlockSpec((1, tq, D_ATTN), lambda bi, i: (bi, i, 0)),
        out_shape=jax.ShapeDtypeStruct((b, t, D_ATTN), BF16),
        compiler_params=_cparams(("arbitrary", "arbitrary")),
        name="attn",
    )(*args)


def _mm3(a, b):
    return _dot(a[0], b[0]) + (_dot(a[0], b[1]) + _dot(a[1], b[0]))


def _tri_inverse_all(mats, blk_mask, merge_masks, eye, each):
    idx = range(len(mats))
    c = eye.shape[0]
    power = [jnp.where(blk_mask, a, 0.0) for a in mats]
    t = [eye + d for d in power]
    power = each(lambda i: _dot(power[i].astype(BF16), power[i].astype(BF16)), idx)
    span = 2
    while 2 * span < INV_BASE:
        both = each(lambda i: _dot(power[i].astype(BF16),
                                   jnp.concatenate([power[i], t[i]], axis=1).astype(BF16)), idx)
        power = [x[:, 0:c] for x in both]
        t = [t[i] + both[i][:, c:] for i in idx]
        span *= 2
    t = each(lambda i: t[i] + _dot(power[i].astype(BF16), t[i].astype(BF16)), idx)
    for mask in merge_masks:
        tb = [x.astype(BF16) for x in t]
        low = each(lambda i: _dot(tb[i], jnp.where(mask, mats[i], 0.0).astype(BF16)).astype(BF16), idx)
        t = each(lambda i: t[i] + _dot(low[i], tb[i]), idx)
    tb = [x.astype(BF16) for x in t]
    t = [x.astype(F32) for x in tb]

    def residual(i):
        a_hi, a_lo = _split(mats[i], 2)
        return ((eye - t[i]) + (_dot(a_hi, tb[i]) + _dot(a_lo, tb[i]))).astype(BF16)

    resid = each(residual, idx)
    return each(lambda i: t[i] + _dot(tb[i], resid[i]), idx)


N_STAGE = 8


def _rwkv_elementwise(q, xc_ref, prev_scr, mu_ref, w0_ref, a0_ref, lora_ref, kk_ref, ka_ref, rk_ref, head_ones, emit, c, t_valid):
    row = lax.broadcasted_iota(jnp.int32, (c, 1), 0)
    live = row < t_valid
    mask = (lambda x: jnp.where(live, x, 0.0)) if t_valid < c else (lambda x: x)

    def shifted(lo_col, width):
        x = xc_ref[q, :, lo_col:lo_col + width]
        prev = jnp.where(row == 0, prev_scr[q, :, lo_col:lo_col + width], pltpu.roll(x, 1, axis=0))
        return x + (prev - x) * mu_ref[:, lo_col:lo_col + width]

    lo = shifted(3 * D_RWKV, 2 * LORA)
    first_half = lax.broadcasted_iota(jnp.int32, (c, LANES), 1) < RWKV_HEAD
    lora = _mm(jnp.where(first_half, jnp.tanh(lo), lo), lora_ref[...], 3)
    tri = (lax.broadcasted_iota(jnp.int32, (c, c), 0) >= lax.broadcasted_iota(jnp.int32, (c, c), 1)).astype(BF16)
    yield
    for p in range(N_RWKV_HEADS // 2):
        cols = slice(p * LANES, (p + 1) * LANES)
        r = shifted(p * LANES, LANES)
        k = shifted(D_RWKV + p * LANES, LANES)
        v = shifted(2 * D_RWKV + p * LANES, LANES)
        emit(6, p, v)
        z = -(w0_ref[:, cols] + lora[:, cols])
        softplus = jnp.maximum(z, 0.0) + jnp.log(1.0 + jnp.exp(-jnp.abs(z)))
        logw = mask(-jnp.exp(-softplus - 0.5))
        cum = None
        for part in _split(logw, 3):
            d = _dot(tri, part)
            cum = d if cum is None else cum + d
        cum_last = cum[c - 1:c]
        a = _sigmoid(a0_ref[:, cols] + lora[:, D_RWKV + p * LANES:D_RWKV + (p + 1) * LANES])
        kk = k * kk_ref[:, cols]
        kk = kk * jnp.minimum(lax.rsqrt(_dot((kk * kk).astype(BF16), head_ones)), 1e12)
        k = k * (1.0 + (a - 1.0) * ka_ref[:, cols])
        emit(7, p, _dot((r * k * rk_ref[:, cols]).astype(BF16), head_ones) * v)
        k = mask(k)
        alpha = mask(-kk)
        beta = mask(kk * a)
        emit(0, p, r * jnp.exp(cum))
        emit(1, p, alpha * jnp.exp(cum - logw))
        g_inv = jnp.exp(-cum)
        emit(2, p, beta * g_inv)
        emit(3, p, k * g_inv)
        g_rest = jnp.exp(cum_last - cum)
        emit(4, p, beta * g_rest)
        emit(5, p, k * g_rest)
        emit(N_STAGE, p, jnp.exp(cum_last))
        yield


def _rwkv_kernel(*refs, c, t_valid, staged_input):
    if staged_input:
        st_ref, gall_ref, g_ref, p0_ref, lnw_ref, lnb_ref, o_ref, pout_ref, state_scr = refs
    else:
        (xc_ref, g_ref, shift_ref, p0_ref, mu_ref, w0_ref, a0_ref, lora_ref, kk_ref, ka_ref, rk_ref,
         lnw_ref, lnb_ref, o_ref, pout_ref, prev_scr, state_scr) = refs
    i = pl.program_id(1)
    nb = g_ref.shape[0]
    n_pairs = N_RWKV_HEADS // 2

    @pl.when(i == 0)
    def _():
        state_scr[...] = p0_ref[...]
        if not staged_input:
            prev_scr[...] = shift_ref[...]

    units = [(q, p) for q in range(nb) for p in range(n_pairs)]
    nu = range(len(units))
    nh = range(2 * len(units))

    pi = lax.broadcasted_iota(jnp.int32, (LANES, LANES), 0)
    pj = lax.broadcasted_iota(jnp.int32, (LANES, LANES), 1)
    head_block = _shr(pi, HEAD_SHIFT) == _shr(pj, HEAD_SHIFT)
    head_ones = head_block.astype(BF16)

    def head_sum(x):
        stacked = jnp.concatenate([x[:, p * LANES:(p + 1) * LANES] for p in range(n_pairs)], axis=0)
        s = _dot(stacked.astype(BF16), head_ones)
        return jnp.concatenate([s[p * c:(p + 1) * c] for p in range(n_pairs)], axis=1)

    staged = {}

    def emit(q, s, p, x):
        staged[s, q, p] = x

    last_row = t_valid - 1 if t_valid < c else c - 1
    for q in range(nb):
        if staged_input:
            for p in range(n_pairs):
                for s in range(N_STAGE):
                    emit(q, s, p, st_ref[q, :, s * D_RWKV + p * LANES:s * D_RWKV + (p + 1) * LANES])
                emit(q, N_STAGE, p, gall_ref[q, 0, 0:1, p * LANES:(p + 1) * LANES])
        else:
            for _ in _rwkv_elementwise(q, xc_ref, prev_scr, mu_ref, w0_ref, a0_ref, lora_ref, kk_ref, ka_ref, rk_ref,
                                       head_ones, functools.partial(emit, q), c, t_valid):
                pass
            prev_scr[q] = xc_ref[q, last_row:last_row + 1, :]
    rb, ab, bb, kb, bt, kt, vp, bonus, g_all = ([staged[s, q, p] for q, p in units] for s in range(N_STAGE + 1))
    each = lambda fn, items: [fn(x) for x in items]

    first_half = lax.broadcasted_iota(jnp.int32, (c, LANES), 1) < RWKV_HEAD
    ti = lax.broadcasted_iota(jnp.int32, (c, c), 0)
    tj = lax.broadcasted_iota(jnp.int32, (c, c), 1)
    tri_incl = ti >= tj
    tri_strict = ti > tj
    tri_incl2 = jnp.concatenate([tri_incl, tri_incl], axis=1)
    eye = (ti == tj).astype(F32)
    base_shift = INV_BASE.bit_length() - 1
    blk_mask = _shr(ti, base_shift) == _shr(tj, base_shift)
    merge_masks = []
    sh = base_shift
    while (1 << sh) < c:
        merge_masks.append((_shr(ti, sh + 1) == _shr(tj, sh + 1)) & ((_shr(ti, sh) & 1) == 1) & ((_shr(tj, sh) & 1) == 0))
        sh += 1
    diag128 = pi == pj
    zeros_c = jnp.zeros((c, LANES), F32)
    pick = lambda x0, x1: jnp.where(first_half, x0, x1)
    half = lambda x, n: x[(n % 2) * c:(n % 2 + 1) * c]

    bk = [jnp.concatenate([bb[u], kb[u]], axis=0) for u in nu]
    ga = each(lambda u: _mm(jnp.concatenate([pick(ab[u], 0.0), pick(0.0, ab[u])], axis=0), bk[u], 3, NT), nu)
    gr = each(lambda u: _mm(jnp.concatenate([pick(rb[u], 0.0), pick(0.0, rb[u])], axis=0), bk[u], 1, NT), nu)
    a_ab = [jnp.where(tri_strict, half(ga[n // 2], n)[:, 0:c], 0.0) for n in nh]
    a_ak = [jnp.where(tri_strict, half(ga[n // 2], n)[:, c:2 * c], 0.0) for n in nh]
    lr = [jnp.where(tri_incl2, half(gr[n // 2], n), 0.0) for n in nh]
    t_inv = _tri_inverse_all(a_ab, blk_mask, merge_masks, eye, each)
    akv = each(lambda n: _mm(a_ak[n], vp[n // 2]), nh)
    xh = each(lambda n: _mm(t_inv[n], jnp.concatenate([ab[n // 2], akv[n]], axis=1)), nh)
    a_new = [pick(xh[2 * u][:, 0:LANES], xh[2 * u + 1][:, 0:LANES]) for u in nu]
    u0 = [pick(xh[2 * u][:, LANES:], xh[2 * u + 1][:, LANES:]) for u in nu]
    zmat = [jnp.concatenate([jnp.concatenate([a_new[u], u0[u]], axis=1),
                             jnp.concatenate([zeros_c, vp[u]], axis=1)], axis=0).astype(BF16) for u in nu]
    yh = each(lambda n: _dot(lr[n].astype(BF16), zmat[n // 2]), nh)
    mn = each(lambda u: _dot(jnp.concatenate([bt[u], kt[u]], axis=0).astype(BF16), zmat[u], TN), nu)
    o_units = []
    for u, (q, p) in enumerate(units):
        r_new = rb[u] + pick(yh[2 * u][:, 0:LANES], yh[2 * u + 1][:, 0:LANES])
        o0 = pick(yh[2 * u][:, LANES:], yh[2 * u + 1][:, LANES:])
        m_mat = jnp.where(diag128, g_all[u], 0.0) + jnp.where(head_block, mn[u][:, 0:LANES], 0.0)
        n0 = jnp.where(head_block, mn[u][:, LANES:], 0.0)
        state = _split(state_scr[q, p], 2)
        o_units.append(_dot(r_new.astype(BF16), state[0]) + o0)
        state_scr[q, p] = _mm3(_split(m_mat, 2), state) + n0

    inv_n = 1.0 / RWKV_HEAD
    for q in range(nb):
        sl = slice(q * n_pairs, (q + 1) * n_pairs)
        o = jnp.concatenate(o_units[sl], axis=1)
        mean = head_sum(o) * inv_n
        cen = o - mean
        var = head_sum(cen * cen) * inv_n
        y = cen * lax.rsqrt(var + LNX_EPS) * lnw_ref[...] + lnb_ref[...]
        o_ref[q] = ((y + jnp.concatenate(bonus[sl], axis=1)) * _silu(g_ref[q])).astype(BF16)

    @pl.when(i == pl.num_programs(1) - 1)
    def _():
        pout_ref[...] = state_scr[...]


RWKV_SEQS = 2


def _rwkv(gate, p0, lp, c, t_valid, xc=None, shift_prev=None, staged=None):
    b, t, _ = gate.shape
    vec = lambda n: pl.BlockSpec((1, n), lambda bi, i: (0, 0))
    n_pairs = N_RWKV_HEADS // 2
    nb = RWKV_SEQS if b % RWKV_SEQS == 0 else 1
    seq_block = lambda width: pl.BlockSpec((nb, c, width), lambda bi, i: (bi, i, 0))
    state_block = pl.BlockSpec((nb, n_pairs, LANES, LANES), lambda bi, i: (bi, 0, 0, 0))
    scratch = [pltpu.VMEM((nb, n_pairs, LANES, LANES), F32)]
    if staged is not None:
        st, gall = staged
        in_specs = [seq_block(N_STAGE * D_RWKV), pl.BlockSpec((nb, 1, 8, D_RWKV), lambda bi, i: (bi, i, 0, 0)),
                    seq_block(D_RWKV), state_block, vec(D_RWKV), vec(D_RWKV)]
        args = [st.reshape(b, t, N_STAGE * D_RWKV), gall.reshape(b, t // c, 8, D_RWKV), gate, p0, lp["lnx_w"], lp["lnx_b"]]
    else:
        in_specs = [seq_block(D_SHIFT), seq_block(D_RWKV), pl.BlockSpec((nb, 1, D_SHIFT), lambda bi, i: (bi, 0, 0)),
                    state_block, vec(D_SHIFT), vec(D_RWKV), vec(D_RWKV),
                    pl.BlockSpec((2 * LORA, 2 * D_RWKV), lambda bi, i: (0, 0)),
                    vec(D_RWKV), vec(D_RWKV), vec(D_RWKV), vec(D_RWKV), vec(D_RWKV)]
        args = [xc, gate, shift_prev.reshape(b, 1, D_SHIFT), p0, lp["mu"], lp["w0"], lp["a0"], lp["lora"],
                lp["k_k"], lp["k_a"], lp["r_k"], lp["lnx_w"], lp["lnx_b"]]
        scratch = [pltpu.VMEM((nb, 1, D_SHIFT), F32)] + scratch
    return pl.pallas_call(
        functools.partial(_rwkv_kernel, c=c, t_valid=t_valid, staged_input=staged is not None),
        grid=(b // nb, t // c),
        in_specs=in_specs,
        out_specs=[seq_block(D_RWKV), state_block],
        out_shape=[jax.ShapeDtypeStruct((b, t, D_RWKV), BF16),
                   jax.ShapeDtypeStruct((b, n_pairs, LANES, LANES), F32)],
        scratch_shapes=scratch,
        compiler_params=_cparams(("arbitrary", "arbitrary")),
        name="rwkv",
    )(*args)


def _state_to_pairs(s):
    b = s.shape[0]
    pt = jnp.swapaxes(s, -1, -2).reshape(b, N_RWKV_HEADS // 2, 2, RWKV_HEAD, RWKV_HEAD)
    z = jnp.zeros_like(pt[:, :, 0])
    top = jnp.concatenate([pt[:, :, 0], z], axis=-1)
    bot = jnp.concatenate([z, pt[:, :, 1]], axis=-1)
    return jnp.concatenate([top, bot], axis=-2)


def _pairs_to_state(pm):
    b = pm.shape[0]
    h0 = pm[:, :, 0:RWKV_HEAD, 0:RWKV_HEAD]
    h1 = pm[:, :, RWKV_HEAD:, RWKV_HEAD:]
    pt = jnp.stack([h0, h1], axis=2).reshape(b, N_RWKV_HEADS, RWKV_HEAD, RWKV_HEAD)
    return jnp.swapaxes(pt, -1, -2)


def _layer(h, lp, att, pool_hist, shift_prev, wkv0, kv_prefix, pos0, final_w, final, cfg):
    b, t, _ = h.shape
    hist16 = jnp.concatenate([jnp.zeros((b, 1, D_POOL), F32), pool_hist], axis=1)
    c = RWKV_CHUNK
    tm = cfg["tm_in"]
    fuse = t % tm == 0 and tm % c == 0
    fused = dict(t=t, pos0=pos0, hist16=hist16, pool_w=lp["pool_w"], pool_scale=lp["pool_scale"],
                 shift_prev=shift_prev, lp=lp) if fuse else None
    outs = _inproj(h.reshape(b * t, D_MODEL), lp["norm_w"], lp["w_in"], tm, fused)
    shape3 = lambda x: x.reshape(b, t, x.shape[-1])
    pg, q, kv, ga, xc, gr = (shape3(x) for x in outs[:len(SEGMENTS)])
    sink_rows = _sink_rows(lp["sinks"], cfg["tq"])
    fuse_attn = t % cfg["tm_out"] == 0 and cfg["tm_out"] % cfg["tq"] == 0 and cfg["tq"] == WINDOW and att["bias"].shape[0] == 2
    if fuse_attn:
        ma = None
        attn_args = dict(q=outs[1], kv=outs[2], prefix=kv_prefix, gate=outs[3], bias=att["bias"], sink_rows=sink_rows,
                         t=t, tq=cfg["tq"])
    else:
        ma = _attn(q, kv, kv_prefix, ga, att["bias"], sink_rows, cfg["tq"])
        attn_args = None
    p0 = _state_to_pairs(wkv0)
    if fuse:
        mp = shape3(outs[len(SEGMENTS)])
        mr, p_new = _rwkv(gr, p0, lp, c, c, staged=outs[len(SEGMENTS) + 1:])
    else:
        mp = _pool(pg, hist16, lp["pool_w"], lp["pool_scale"], pos0, cfg["tt_pool"])
        pad = (-t) % c
        xc_in = jnp.pad(xc, ((0, 0), (0, pad), (0, 0)))
        gr_in = jnp.pad(gr, ((0, 0), (0, pad), (0, 0)))
        mr, p_new = _rwkv(gr_in, p0, lp, c, t if pad else c, xc=xc_in, shift_prev=shift_prev)
        mr = mr[:, :t]

    flat = lambda x: x.reshape(b * t, x.shape[-1])
    h_new = _outproj(flat(h), flat(mp), None if fuse_attn else flat(ma), flat(mr), lp["w_out"], final_w, final,
                     cfg["tm_out"], attn_args)
    new_pool = pg[:, -POOL_HIST:, 0:D_POOL]
    kvf = kv[:, -WINDOW:] if t >= WINDOW else jnp.concatenate([kv_prefix[:, t:], kv], axis=1)
    new_k = kvf[:, :, 0:LANES].reshape(b, WINDOW, N_KV_HEADS, HEAD_DIM)
    new_v = kvf[:, :, LANES:].reshape(b, WINDOW, N_KV_HEADS, HEAD_DIM)
    new_shift = xc[:, -1]
    return h_new.reshape(b, t, D_MODEL), (new_pool, new_k, new_v, new_shift, _pairs_to_state(p_new))


def _group_cfg(b, t):
    m = b * t
    tq = min(t, ATT_TILE)
    return {"tm_in": min(m, 256), "tm_out": min(m, 512), "tt_pool": min(t, 512), "tq": tq}


def kernel(x_prompt, x_sample, state_pool, cache_swa_k, cache_swa_v, state_rwkv_shift, state_rwkv_wkv, norm_w, w_in, w_out, pool_w, pool_scale, attn_sinks, rel_bias_table, rwkv_mu, rwkv_w0, rwkv_w_up, rwkv_a0, rwkv_a_up, rwkv_k_k, rwkv_k_a, rwkv_r_k, rwkv_lnx_w, rwkv_lnx_b, final_norm_w):
    bp, tp, _ = x_prompt.shape
    bs, ts, _ = x_sample.shape
    cfg_p = _group_cfg(bp, tp)
    cfg_s = _group_cfg(bs, ts)
    att_p = {"bias": jnp.stack([_attn_bias(rel_bias_table, cfg_p["tq"], WINDOW + cfg_p["tq"], False),
                                _attn_bias(rel_bias_table, cfg_p["tq"], WINDOW + cfg_p["tq"], True)])}
    att_s = {"bias": _attn_bias(rel_bias_table, cfg_s["tq"], WINDOW + cfg_s["tq"], True)[None]}

    hp, hs = x_prompt, x_sample
    prompt_states, sample_states = [], []
    row = lambda x: x.astype(F32).reshape(1, -1)
    zero_lora = jnp.zeros((LORA, D_RWKV), F32)
    for l in range(DEPTH):
        lp = {
            "norm_w": norm_w[l], "w_in": _layer_bf16(w_in, l), "w_out": _layer_bf16(w_out, l),
            "pool_w": pool_w[l].astype(BF16), "pool_scale": pool_scale[l], "sinks": attn_sinks[l],
            "mu": row(rwkv_mu[l]), "w0": row(rwkv_w0[l]), "a0": row(rwkv_a0[l]),
            "lora": jnp.concatenate([jnp.concatenate([rwkv_w_up[l].astype(F32), zero_lora], axis=1),
                                     jnp.concatenate([zero_lora, rwkv_a_up[l].astype(F32)], axis=1)], axis=0),
            "k_k": row(rwkv_k_k[l]), "k_a": row(rwkv_k_a[l]), "r_k": row(rwkv_r_k[l]),
            "lnx_w": row(rwkv_lnx_w[l]), "lnx_b": row(rwkv_lnx_b[l]),
        }
        final = l == DEPTH - 1
        hp, sp = _layer(hp, lp, att_p, jnp.zeros((bp, POOL_HIST, D_POOL), F32), jnp.zeros((bp, D_SHIFT), F32),
                        jnp.zeros((bp, N_RWKV_HEADS, RWKV_HEAD, RWKV_HEAD), F32),
                        jnp.zeros((bp, WINDOW, 4 * HEAD_DIM), F32), 0, final_norm_w, final, cfg_p)
        prefix_s = jnp.concatenate([cache_swa_k[l].reshape(bs, WINDOW, LANES), cache_swa_v[l].reshape(bs, WINDOW, LANES)], axis=-1)
        hs, ss = _layer(hs, lp, att_s, state_pool[l], state_rwkv_shift[l], state_rwkv_wkv[l], prefix_s, PAST_LEN,
                        final_norm_w, final, cfg_s)
        prompt_states.append(sp)
        sample_states.append(ss)
    outs_p = [jnp.stack(x) for x in zip(*prompt_states)]
    outs_s = [jnp.stack(x) for x in zip(*sample_states)]
    return (hp, hs, *outs_p, *outs_s)
```

```python
import functools
import math

import jax
import jax.numpy as jnp
from jax import lax
from jax.experimental import pallas as pl
from jax.experimental.pallas import tpu as pltpu

F32 = jnp.float32
BF16 = jnp.bfloat16

D_MODEL = 2048
DEPTH = 4
PAST_LEN = 1024
CHUNK = 64
D_POOL = 512
POOL_WINDOWS = (2, 4, 8, 16)
POOL_GROUP = 128
POOL_HIST = 15
HEAD_DIM = 64
D_ATTN = 1024
N_Q_HEADS = 16
N_KV_HEADS = 2
WINDOW = 128
WIN_CHUNKS = 2
NUM_BUCKETS = 32
MAX_DISTANCE = 128
NEG = -1e30
LOG2E = 1.0 / math.log(2.0)
D_RWKV = 512
RWKV_HEAD = 64
N_RWKV_HEADS = 8
LORA = 64
D_SHIFT = 3 * D_RWKV + 2 * LORA
NORM_EPS = 1e-5
LNX_EPS = 1e-5 * RWKV_HEAD
SEGMENTS = (2 * D_POOL, D_ATTN, 2 * N_KV_HEADS * HEAD_DIM, D_ATTN, D_SHIFT, D_RWKV)
D_IN = sum(SEGMENTS)

LANES = 128
SUBLANES = 8
POOL_HALO = 16
ATT_TILE = 128
RWKV_CHUNK = 128
INV_BASE = 16
VMEM_LIMIT = 56 * 1024 * 1024


def _cparams(sem):
    return pltpu.CompilerParams(dimension_semantics=sem, vmem_limit_bytes=VMEM_LIMIT)


def _sigmoid(x):
    return 0.5 * jnp.tanh(0.5 * x) + 0.5


def _silu(x):
    return x * _sigmoid(x)


def _split(x, terms):
    parts = []
    rem = x
    for _ in range(terms):
        p = rem.astype(BF16)
        parts.append(p)
        rem = rem - p.astype(F32)
    return parts


def _dot(a, b, dims=None):
    if dims is None:
        return jnp.dot(a, b, preferred_element_type=F32)
    return lax.dot_general(a, b, (dims, ((), ())), preferred_element_type=F32)


def _mm(a, b, passes=1, dims=None):
    if passes == 1:
        return _dot(a.astype(BF16), b.astype(BF16), dims)
    ah, al = _split(a, 2)
    if passes == 2:
        bb = b.astype(BF16)
        return _dot(ah, bb, dims) + _dot(al, bb, dims)
    bh, bl = _split(b, 2)
    return _dot(ah, bh, dims) + (_dot(ah, bl, dims) + _dot(al, bh, dims))


HEAD_SHIFT = RWKV_HEAD.bit_length() - 1


def _shr(x, bits):
    return lax.shift_right_logical(x, jnp.full_like(x, bits))


NT = ((1,), (1,))
TN = ((0,), (0,))


CAST_ROWS = 256


def _cast_kernel(w_ref, o_ref):
    o_ref[...] = w_ref[...].astype(BF16)


def _layer_bf16(w, l):
    _, rows, cols = w.shape
    return pl.pallas_call(
        _cast_kernel,
        grid=(rows // CAST_ROWS,),
        in_specs=[pl.BlockSpec((None, CAST_ROWS, cols), lambda i: (l, i, 0))],
        out_specs=pl.BlockSpec((CAST_ROWS, cols), lambda i: (i, 0)),
        out_shape=jax.ShapeDtypeStruct((rows, cols), BF16),
        compiler_params=_cparams(("arbitrary",)),
        name="cast_bf16",
    )(w)


SEG_OFFSETS = tuple(sum(SEGMENTS[:n]) for n in range(len(SEGMENTS)))
FUSE_COLS = 512


class _ChunkView:
    def __init__(self, x, c):
        self.x, self.c = x, c

    def __getitem__(self, idx):
        q, _, cols = idx
        return self.x[q * self.c:(q + 1) * self.c, cols]


class _PrevRowView:
    def __init__(self, x, first, c):
        self.x, self.first, self.c = x, first, c

    def __getitem__(self, idx):
        q, _, cols = idx
        return self.first[:, cols] if q == 0 else self.x[q * self.c - 1:q * self.c, cols]


def _inproj_kernel(h_ref, nw_ref, w_ref, *refs, seq_tiles, pos0, c):
    x = h_ref[...]
    ms = jnp.mean(x * x, axis=-1, keepdims=True)
    xn = (x * lax.rsqrt(ms + NORM_EPS) * nw_ref[...]).astype(BF16)
    tm = x.shape[0]
    fused = seq_tiles > 0
    if fused:
        hist_ref, pw_ref, ps_ref, shift_ref, mu_ref, w0_ref, a0_ref, lora_ref, kk_ref, ka_ref, rk_ref, *refs = refs
        mp_ref, st_ref, gall_ref, halo_scr, prevrow_scr = refs[len(SEGMENTS):]
        it = lax.rem(pl.program_id(0), seq_tiles)
    o_refs = refs[:len(SEGMENTS)]

    def project(n, lo=0, width=None):
        width = SEGMENTS[n] if width is None else width
        seg = jnp.dot(xn, w_ref[:, SEG_OFFSETS[n] + lo:SEG_OFFSETS[n] + lo + width], preferred_element_type=F32)
        o_refs[n][:, lo:lo + width] = seg
        return seg

    if not fused:
        for n in range(len(SEGMENTS)):
            project(n)
        return

    xc = project(4)
    first = jnp.where(it == 0, shift_ref[0], prevrow_scr[...])
    prevrow_scr[...] = xc[tm - 1:tm]
    pi = lax.broadcasted_iota(jnp.int32, (LANES, LANES), 0)
    pj = lax.broadcasted_iota(jnp.int32, (LANES, LANES), 1)
    head_ones = (_shr(pi, HEAD_SHIFT) == _shr(pj, HEAD_SHIFT)).astype(BF16)

    def emit(q, s, p, x):
        if s == N_STAGE:
            gall_ref[q, :, p * LANES:(p + 1) * LANES] = jnp.broadcast_to(x, (SUBLANES, LANES))
        else:
            st_ref[q * c:(q + 1) * c, s * D_RWKV + p * LANES:s * D_RWKV + (p + 1) * LANES] = x

    def mixer_slices():
        for q in range(tm // c):
            yield from _rwkv_elementwise(q, _ChunkView(xc, c), _PrevRowView(xc, first, c), mu_ref, w0_ref, a0_ref,
                                         lora_ref, kk_ref, ka_ref, rk_ref, head_ones, functools.partial(emit, q), c, c)

    slices = mixer_slices()
    pg_parts = []
    for n in (0, 1, 2, 3, 5):
        for lo in range(0, SEGMENTS[n], FUSE_COLS):
            part = project(n, lo, min(FUSE_COLS, SEGMENTS[n] - lo))
            if n == 0:
                pg_parts.append(part)
            next(slices, None)
        if n == 0:
            pg = jnp.concatenate(pg_parts, axis=1)
            p = pg[:, 0:D_POOL]
            halo = jnp.where(it == 0, hist_ref[0], halo_scr[...])
            halo_scr[...] = p[tm - POOL_HALO:tm]
            mp_ref[...] = _pool_math(p, pg[:, D_POOL:2 * D_POOL], halo, pos0 + it * tm, pw_ref, ps_ref)
    for _ in slices:
        pass


def _inproj(h2d, norm_w, w_in_bf16, tm, fused=None):
    m = h2d.shape[0]
    c = RWKV_CHUNK
    in_specs = [
        pl.BlockSpec((tm, D_MODEL), lambda i: (i, 0)),
        pl.BlockSpec((1, D_MODEL), lambda i: (0, 0)),
        pl.BlockSpec(memory_space=pltpu.VMEM),
    ]
    out_specs = [pl.BlockSpec((tm, s), lambda i: (i, 0)) for s in SEGMENTS]
    out_shape = [jax.ShapeDtypeStruct((m, s), F32) for s in SEGMENTS]
    args = [h2d, norm_w.reshape(1, D_MODEL), w_in_bf16]
    scratch = []
    seq_tiles = pos0 = 0
    if fused is not None:
        seq_tiles, pos0, lp = fused["t"] // tm, fused["pos0"], fused["lp"]
        b = m // fused["t"]
        vec = lambda n: pl.BlockSpec((1, n), lambda i: (0, 0))
        in_specs += [pl.BlockSpec((1, POOL_HALO, D_POOL), lambda i: (i // seq_tiles, 0, 0)),
                     pl.BlockSpec((4, POOL_GROUP, POOL_GROUP), lambda i: (0, 0, 0)),
                     vec(D_POOL),
                     pl.BlockSpec((1, 1, D_SHIFT), lambda i: (i // seq_tiles, 0, 0)),
                     vec(D_SHIFT), vec(D_RWKV), vec(D_RWKV),
                     pl.BlockSpec((2 * LORA, 2 * D_RWKV), lambda i: (0, 0)),
                     vec(D_RWKV), vec(D_RWKV), vec(D_RWKV)]
        args += [fused["hist16"], fused["pool_w"], fused["pool_scale"].reshape(1, D_POOL),
                 fused["shift_prev"].reshape(b, 1, D_SHIFT), lp["mu"], lp["w0"], lp["a0"], lp["lora"],
                 lp["k_k"], lp["k_a"], lp["r_k"]]
        out_specs += [pl.BlockSpec((tm, D_POOL), lambda i: (i, 0)),
                      pl.BlockSpec((tm, N_STAGE * D_RWKV), lambda i: (i, 0)),
                      pl.BlockSpec((tm // c, SUBLANES, D_RWKV), lambda i: (i, 0, 0))]
        out_shape += [jax.ShapeDtypeStruct((m, D_POOL), BF16),
                      jax.ShapeDtypeStruct((m, N_STAGE * D_RWKV), F32),
                      jax.ShapeDtypeStruct((m // c, SUBLANES, D_RWKV), F32)]
        scratch = [pltpu.VMEM((POOL_HALO, D_POOL), F32), pltpu.VMEM((1, D_SHIFT), F32)]
    return pl.pallas_call(
        functools.partial(_inproj_kernel, seq_tiles=seq_tiles, pos0=pos0, c=c),
        grid=(m // tm,),
        in_specs=in_specs,
        out_specs=out_specs,
        out_shape=out_shape,
        scratch_shapes=scratch,
        compiler_params=_cparams(("arbitrary",)),
        name="inproj",
    )(*args)


OUT_COLS = 512
ATTN_SLICES = 7


def _outproj_kernel(*refs, final, attn_tiles, tq):
    if not attn_tiles:
        h_ref, mp_ref, ma_ref, mr_ref, w_ref, fw_ref, o_ref = refs
        acc = jnp.dot(mp_ref[...], w_ref[0:D_POOL, :], preferred_element_type=F32)
        acc += jnp.dot(ma_ref[...], w_ref[D_POOL:D_POOL + D_ATTN, :], preferred_element_type=F32)
        acc += jnp.dot(mr_ref[...], w_ref[D_POOL + D_ATTN:, :], preferred_element_type=F32)
        hn = h_ref[...] + acc
    else:
        h_ref, mp_ref, mr_ref, w_ref, fw_ref, q_ref, kv_ref, kvp_ref, pre_ref, g_ref, bias_ref, sink_ref, o_ref, ma_scr = refs
        tm = h_ref.shape[0]
        it = lax.rem(pl.program_id(0), attn_tiles)

        def attention_slices():
            for j in range(tm // tq):
                r = slice(j * tq, (j + 1) * tq)
                if j == 0:
                    kvp = jnp.where(it == 0, pre_ref[0], kvp_ref[...])
                    variant = jnp.minimum(it, 1)
                    bias_get = lambda rows, variant=variant: bias_ref[variant, rows, :]
                else:
                    kvp = kv_ref[j * tq - WINDOW:j * tq, :]
                    bias_get = lambda rows: bias_ref[1, rows, :]

                def put(cols, x, r=r):
                    ma_scr[r, cols] = x

                yield from _attn_tile(lambda cols, r=r: q_ref[r, cols], kvp, kv_ref[r, :], lambda cols, r=r: g_ref[r, cols],
                                      bias_get, sink_ref, put, tq)

        slices = attention_slices()
        blocks = [slice(j * OUT_COLS, (j + 1) * OUT_COLS) for j in range(D_MODEL // OUT_COLS)]
        per_block = -(-(tm // tq) * ATTN_SLICES // (2 * len(blocks)))
        acc = []
        for cols in blocks:
            part = jnp.dot(mp_ref[...], w_ref[0:D_POOL, cols], preferred_element_type=F32)
            for _ in range(per_block):
                next(slices, None)
            part += jnp.dot(mr_ref[...], w_ref[D_POOL + D_ATTN:, cols], preferred_element_type=F32)
            for _ in range(per_block):
                next(slices, None)
            acc.append(part)
        for _ in slices:
            pass
        ma = ma_scr[...]
        hn = jnp.concatenate([h_ref[:, cols] + acc[j] + jnp.dot(ma, w_ref[D_POOL:D_POOL + D_ATTN, cols], preferred_element_type=F32)
                              for j, cols in enumerate(blocks)], axis=1)
    if final:
        ms = jnp.mean(hn * hn, axis=-1, keepdims=True)
        hn = hn * lax.rsqrt(ms + NORM_EPS) * fw_ref[...]
    o_ref[...] = hn


def _outproj(h2d, mp, ma, mr, w_out_bf16, final_w, final, tm, attn=None):
    m = h2d.shape[0]
    row = lambda width: pl.BlockSpec((tm, width), lambda i: (i, 0))
    whole = pl.BlockSpec(memory_space=pltpu.VMEM)
    fw_spec = pl.BlockSpec((1, D_MODEL), lambda i: (0, 0))
    if attn is None:
        in_specs = [row(D_MODEL), row(D_POOL), row(D_ATTN), row(D_RWKV), whole, fw_spec]
        args = [h2d, mp, ma, mr, w_out_bf16, final_w.reshape(1, D_MODEL)]
        scratch, attn_tiles, tq = [], 0, 0
    else:
        tq, attn_tiles = attn["tq"], attn["t"] // tm
        per = tm // WINDOW
        in_specs = [row(D_MODEL), row(D_POOL), row(D_RWKV), whole, fw_spec, row(D_ATTN), row(4 * HEAD_DIM),
                    pl.BlockSpec((WINDOW, 4 * HEAD_DIM), lambda i: (jnp.maximum(i * per - 1, 0), 0)),
                    pl.BlockSpec((1, WINDOW, 4 * HEAD_DIM), lambda i: (i // attn_tiles, 0, 0)),
                    row(D_ATTN), whole, whole]
        args = [h2d, mp, mr, w_out_bf16, final_w.reshape(1, D_MODEL), attn["q"], attn["kv"], attn["kv"], attn["prefix"],
                attn["gate"], attn["bias"], attn["sink_rows"]]
        scratch = [pltpu.VMEM((tm, D_ATTN), BF16)]
    return pl.pallas_call(
        functools.partial(_outproj_kernel, final=final, attn_tiles=attn_tiles, tq=tq),
        grid=(m // tm,),
        in_specs=in_specs,
        out_specs=row(D_MODEL),
        out_shape=jax.ShapeDtypeStruct((m, D_MODEL), F32),
        scratch_shapes=scratch,
        compiler_params=_cparams(("arbitrary",)),
        name="outproj",
    )(*args)


def _pool_math(p, gate, halo, pos_first, pw_ref, ps_ref):
    tt = p.shape[0]
    ext = jnp.concatenate([halo, p], axis=0)
    pos = pos_first + lax.broadcasted_iota(jnp.int32, (tt, 1), 0)
    outs = []
    for g, w in enumerate(POOL_WINDOWS):
        s = ext[:, g * POOL_GROUP:(g + 1) * POOL_GROUP]
        span = 1
        while span < w:
            n = s.shape[0]
            s = s[span:n] + s[0:n - span]
            span *= 2
        win = s[POOL_HALO - (w - 1):POOL_HALO - (w - 1) + tt]
        cnt = jnp.minimum(pos + 1, w).astype(F32)
        d = win / cnt - p[:, g * POOL_GROUP:(g + 1) * POOL_GROUP]
        outs.append(jnp.dot(d.astype(BF16), pw_ref[g], preferred_element_type=F32))
    y = jnp.concatenate(outs, axis=1) * ps_ref[...]
    return (y * _silu(gate)).astype(BF16)


def _pool_kernel(pg_ref, halo_ref, hist_ref, pw_ref, ps_ref, o_ref, *, tt, pos0):
    i = pl.program_id(1)
    halo = jnp.where(i == 0, hist_ref[0], halo_ref[0, :, 0:D_POOL])
    o_ref[0] = _pool_math(pg_ref[0, :, 0:D_POOL], pg_ref[0, :, D_POOL:2 * D_POOL], halo, pos0 + i * tt, pw_ref, ps_ref)


def _pool(pg, hist16, pool_w_bf16, pool_scale, pos0, tt):
    b, t, _ = pg.shape
    nh = tt // POOL_HALO
    return pl.pallas_call(
        functools.partial(_pool_kernel, tt=tt, pos0=pos0),
        grid=(b, t // tt),
        in_specs=[
            pl.BlockSpec((1, tt, 2 * D_POOL), lambda bi, i: (bi, i, 0)),
            pl.BlockSpec((1, POOL_HALO, 2 * D_POOL), lambda bi, i: (bi, jnp.maximum(i * nh - 1, 0), 0)),
            pl.BlockSpec((1, POOL_HALO, D_POOL), lambda bi, i: (bi, 0, 0)),
            pl.BlockSpec((4, POOL_GROUP, POOL_GROUP), lambda bi, i: (0, 0, 0)),
            pl.BlockSpec((1, D_POOL), lambda bi, i: (0, 0)),
        ],
        out_specs=pl.BlockSpec((1, tt, D_POOL), lambda bi, i: (bi, i, 0)),
        out_shape=jax.ShapeDtypeStruct((b, t, D_POOL), BF16),
        compiler_params=_cparams(("arbitrary", "arbitrary")),
        name="pool",
    )(pg, pg, hist16, pool_w_bf16, pool_scale.reshape(1, D_POOL))


def _t5_bucket(rel):
    nb = NUM_BUCKETS // 2
    max_exact = nb // 2
    ret = jnp.where(rel > 0, nb, 0)
    n = jnp.abs(rel)
    nf = jnp.maximum(n, 1).astype(F32)
    large = max_exact + (jnp.log(nf / max_exact) / math.log(MAX_DISTANCE / max_exact) * (nb - max_exact)).astype(jnp.int32)
    large = jnp.minimum(large, nb - 1)
    return ret + jnp.where(n < max_exact, n, large)


def _bias_kernel(bucket_ref, tab_ref, o_ref):
    bucket = bucket_ref[...]
    tab = tab_ref[...]
    out = jnp.full(bucket.shape, NEG, F32)
    for b in range(NUM_BUCKETS):
        out = jnp.where(bucket == b, tab[:, b:b + 1] * LOG2E, out)
    o_ref[...] = out


def _attn_bias(table, tq, nk, prefix_valid):
    qi = jnp.arange(tq)
    kj = jnp.arange(nk)
    rel = kj[None, :] - WINDOW - qi[:, None]
    bucket = _t5_bucket(rel)
    qc = qi // CHUNK
    kc = (kj - WINDOW) // CHUNK
    valid = (kc[None, :] <= qc[:, None]) & (kc[None, :] >= qc[:, None] - WIN_CHUNKS)
    valid &= (kj < WINDOW + tq)[None, :]
    if not prefix_valid:
        valid &= (kj >= WINDOW)[None, :]
    bucket = jnp.where(valid, bucket, -1).astype(jnp.int32)
    heads = jnp.array([[[8 * g + 2 * p + par for p in range(4)] for par in range(2)] for g in range(2)]).reshape(-1)
    rows = heads.shape[0] * tq
    bucket_rows = jnp.broadcast_to(bucket[None], (heads.shape[0], tq, nk)).reshape(rows, nk)
    tab_rows = jnp.broadcast_to(table.astype(F32).T[heads][:, None, :], (heads.shape[0], tq, NUM_BUCKETS)).reshape(rows, NUM_BUCKETS)
    tr = 4 * tq
    return pl.pallas_call(
        _bias_kernel,
        grid=(rows // tr,),
        in_specs=[pl.BlockSpec((tr, nk), lambda i: (i, 0)), pl.BlockSpec((tr, NUM_BUCKETS), lambda i: (i, 0))],
        out_specs=pl.BlockSpec((tr, nk), lambda i: (i, 0)),
        out_shape=jax.ShapeDtypeStruct((rows, nk), F32),
        name="attn_bias",
    )(bucket_rows, tab_rows)


def _sink_rows(sinks, tq):
    heads = jnp.array([[[8 * g + 2 * p + par for p in range(4)] for par in range(2)] for g in range(2)]).reshape(-1)
    return jnp.broadcast_to(sinks.astype(F32)[heads][:, None, None] * LOG2E, (heads.shape[0], tq, LANES)).reshape(-1, LANES)


def _attn_tile(q_get, kvp, kvc, gate_get, bias_get, sink_ref, put, tq):
    kv = jnp.concatenate([kvp, kvc], axis=0)
    nk = kv.shape[0]
    k = kv[:, 0:LANES] * (HEAD_DIM ** -0.5 * LOG2E)
    v = kv[:, LANES:2 * LANES]
    low = lax.broadcasted_iota(jnp.int32, (nk, LANES), 1) < HEAD_DIM
    k_sw = pltpu.roll(k, HEAD_DIM, axis=1)
    v_sw = pltpu.roll(v, HEAD_DIM, axis=1)
    ones = jnp.ones_like(v)

    def place(x, x_sw, g, par):
        src = x if g == par else x_sw
        return jnp.where(low, src, 0.0) if par == 0 else jnp.where(low, 0.0, src)

    blocks = [(g, par) for g in range(N_KV_HEADS) for par in range(2)]
    nblk = range(len(blocks))
    kx = [place(k, k_sw, g, par).astype(BF16) for g, par in blocks]
    vx = [jnp.concatenate([place(v, v_sw, g, par), ones], axis=1).astype(BF16) for g, par in blocks]
    qs = [jnp.concatenate([q_get(slice((4 * g + p) * LANES, (4 * g + p + 1) * LANES)) for p in range(4)], axis=0).astype(BF16)
          for g in range(N_KV_HEADS)]
    rows = [slice(n * 4 * tq, (n + 1) * 4 * tq) for n in nblk]
    yield
    s = [_dot(qs[g], kx[n], NT) + bias_get(rows[n]) for n, (g, par) in enumerate(blocks)]
    yield
    sink = [sink_ref[rows[n], :] for n in nblk]
    m = [jnp.maximum(jnp.broadcast_to(jnp.max(s[n], axis=-1, keepdims=True), (4 * tq, LANES)), sink[n]) for n in nblk]
    yield
    widen = lambda x: jnp.concatenate([x] * (nk // LANES) + ([x[:, 0:nk % LANES]] if nk % LANES else []), axis=1)
    e = [jnp.exp2(s[n] - widen(m[n])).astype(BF16) for n in nblk]
    yield
    pv = [_dot(e[n], vx[n]) for n in nblk]
    den = [pv[n][:, LANES:] + jnp.exp2(sink[n] - m[n]) for n in nblk]
    yield
    even_lanes = lax.broadcasted_iota(jnp.int32, (4 * tq, LANES), 1) < HEAD_DIM
    for g in range(N_KV_HEADS):
        acc = (pv[2 * g][:, 0:LANES] + pv[2 * g + 1][:, 0:LANES]) / jnp.where(even_lanes, den[2 * g], den[2 * g + 1])
        for p in range(4):
            cols = slice((4 * g + p) * LANES, (4 * g + p + 1) * LANES)
            put(cols, (acc[p * tq:(p + 1) * tq] * _silu(gate_get(cols))).astype(BF16))
        yield


def _attn_kernel(*refs, tq, has_prev):
    if has_prev:
        q_ref, kvc_ref, kvp_ref, pre_ref, g_ref, bias_ref, sink_ref, o_ref = refs
    else:
        q_ref, kvc_ref, pre_ref, g_ref, bias_ref, sink_ref, o_ref = refs
    i = pl.program_id(1)
    kvp = jnp.where(i == 0, pre_ref[0], kvp_ref[0]) if has_prev else pre_ref[0]

    def put(cols, x):
        o_ref[0, :, cols] = x

    for _ in _attn_tile(lambda cols: q_ref[0, :, cols], kvp, kvc_ref[0], lambda cols: g_ref[0, :, cols],
                        lambda rows: bias_ref[rows, :], sink_ref, put, tq):
        pass


def _attn(q, kv, prefix, gate, bias, sink_rows, tq):
    b, t, _ = q.shape
    n_tiles = t // tq
    has_prev = n_tiles > 1
    nk = WINDOW + tq
    rows = bias.shape[-2]
    in_specs = [
        pl.BlockSpec((1, tq, D_ATTN), lambda bi, i: (bi, i, 0)),
        pl.BlockSpec((1, tq, 4 * HEAD_DIM), lambda bi, i: (bi, i, 0)),
    ]
    args = [q, kv]
    if has_prev:
        in_specs.append(pl.BlockSpec((1, WINDOW, 4 * HEAD_DIM), lambda bi, i: (bi, jnp.maximum(i - 1, 0), 0)))
        args.append(kv)
    in_specs += [
        pl.BlockSpec((1, WINDOW, 4 * HEAD_DIM), lambda bi, i: (bi, 0, 0)),
        pl.BlockSpec((1, tq, D_ATTN), lambda bi, i: (bi, i, 0)),
        pl.BlockSpec((None, rows, nk), lambda bi, i: (jnp.minimum(i, bias.shape[0] - 1), 0, 0)),
        pl.BlockSpec((rows, LANES), lambda bi, i: (0, 0)),
    ]
    args += [prefix, gate, bias, sink_rows]
    return pl.pallas_call(
        functools.partial(_attn_kernel, tq=tq, has_prev=has_prev),
        grid=(b, n_tiles),
        in_specs=in_specs,
        out_specs=pl.BlockSpec((1, tq, D_ATTN), lambda bi, i: (bi, i, 0)),
        out_shape=jax.ShapeDtypeStruct((b, t, D_ATTN), BF16),
        compiler_params=_cparams(("arbitrary", "arbitrary")),
        name="attn",
    )(*args)


def _mm3(a, b):
    return _dot(a[0], b[0]) + (_dot(a[0], b[1]) + _dot(a[1], b[0]))


def _tri_inverse_all(mats, blk_mask, merge_masks, eye, each):
    idx = range(len(mats))
    c = eye.shape[0]
    power = [jnp.where(blk_mask, a, 0.0) for a in mats]
    t = [eye + d for d in power]
    power = each(lambda i: _dot(power[i].astype(BF16), power[i].astype(BF16)), idx)
    span = 2
    while 2 * span < INV_BASE:
        both = each(lambda i: _dot(power[i].astype(BF16),
                                   jnp.concatenate([power[i], t[i]], axis=1).astype(BF16)), idx)
        power = [x[:, 0:c] for x in both]
        t = [t[i] + both[i][:, c:] for i in idx]
        span *= 2
    t = each(lambda i: t[i] + _dot(power[i].astype(BF16), t[i].astype(BF16)), idx)
    for mask in merge_masks:
        tb = [x.astype(BF16) for x in t]
        low = each(lambda i: _dot(tb[i], jnp.where(mask, mats[i], 0.0).astype(BF16)).astype(BF16), idx)
        t = each(lambda i: t[i] + _dot(low[i], tb[i]), idx)
    tb = [x.astype(BF16) for x in t]
    t = [x.astype(F32) for x in tb]

    def residual(i):
        a_hi, a_lo = _split(mats[i], 2)
        return ((eye - t[i]) + (_dot(a_hi, tb[i]) + _dot(a_lo, tb[i]))).astype(BF16)

    resid = each(residual, idx)
    return each(lambda i: t[i] + _dot(tb[i], resid[i]), idx)


N_STAGE = 8


def _rwkv_elementwise(q, xc_ref, prev_scr, mu_ref, w0_ref, a0_ref, lora_ref, kk_ref, ka_ref, rk_ref, head_ones, emit, c, t_valid):
    row = lax.broadcasted_iota(jnp.int32, (c, 1), 0)
    live = row < t_valid
    mask = (lambda x: jnp.where(live, x, 0.0)) if t_valid < c else (lambda x: x)

    def shifted(lo_col, width):
        x = xc_ref[q, :, lo_col:lo_col + width]
        prev = jnp.where(row == 0, prev_scr[q, :, lo_col:lo_col + width], pltpu.roll(x, 1, axis=0))
        return x + (prev - x) * mu_ref[:, lo_col:lo_col + width]

    lo = shifted(3 * D_RWKV, 2 * LORA)
    first_half = lax.broadcasted_iota(jnp.int32, (c, LANES), 1) < RWKV_HEAD
    lora = _mm(jnp.where(first_half, jnp.tanh(lo), lo), lora_ref[...], 2)
    tri = (lax.broadcasted_iota(jnp.int32, (c, c), 0) >= lax.broadcasted_iota(jnp.int32, (c, c), 1)).astype(BF16)
    yield
    for p in range(N_RWKV_HEADS // 2):
        cols = slice(p * LANES, (p + 1) * LANES)
        r = shifted(p * LANES, LANES)
        k = shifted(D_RWKV + p * LANES, LANES)
        v = shifted(2 * D_RWKV + p * LANES, LANES)
        emit(6, p, v)
        z = -(w0_ref[:, cols] + lora[:, cols])
        softplus = jnp.maximum(z, 0.0) + jnp.log(1.0 + jnp.exp(-jnp.abs(z)))
        logw = mask(-jnp.exp(-softplus - 0.5))
        cum = None
        for part in _split(logw, 3):
            d = _dot(tri, part)
            cum = d if cum is None else cum + d
        cum_last = cum[c - 1:c]
        a = _sigmoid(a0_ref[:, cols] + lora[:, D_RWKV + p * LANES:D_RWKV + (p + 1) * LANES])
        kk = k * kk_ref[:, cols]
        kk = kk * jnp.minimum(lax.rsqrt(_dot((kk * kk).astype(BF16), head_ones)), 1e12)
        k = k * (1.0 + (a - 1.0) * ka_ref[:, cols])
        emit(7, p, _dot((r * k * rk_ref[:, cols]).astype(BF16), head_ones) * v)
        k = mask(k)
        alpha = mask(-kk)
        beta = mask(kk * a)
        emit(0, p, r * jnp.exp(cum))
        emit(1, p, alpha * jnp.exp(cum - logw))
        g_inv = jnp.exp(-cum)
        emit(2, p, beta * g_inv)
        emit(3, p, k * g_inv)
        g_rest = jnp.exp(cum_last - cum)
        emit(4, p, beta * g_rest)
        emit(5, p, k * g_rest)
        emit(N_STAGE, p, jnp.exp(cum_last))
        yield


def _rwkv_kernel(*refs, c, t_valid, staged_input):
    if staged_input:
        st_ref, gall_ref, g_ref, p0_ref, lnw_ref, lnb_ref, o_ref, pout_ref, state_scr = refs
    else:
        (xc_ref, g_ref, shift_ref, p0_ref, mu_ref, w0_ref, a0_ref, lora_ref, kk_ref, ka_ref, rk_ref,
         lnw_ref, lnb_ref, o_ref, pout_ref, prev_scr, state_scr) = refs
    i = pl.program_id(1)
    nb = g_ref.shape[0]
    n_pairs = N_RWKV_HEADS // 2

    @pl.when(i == 0)
    def _():
        state_scr[...] = p0_ref[...]
        if not staged_input:
            prev_scr[...] = shift_ref[...]

    units = [(q, p) for q in range(nb) for p in range(n_pairs)]
    nu = range(len(units))
    nh = range(2 * len(units))

    pi = lax.broadcasted_iota(jnp.int32, (LANES, LANES), 0)
    pj = lax.broadcasted_iota(jnp.int32, (LANES, LANES), 1)
    head_block = _shr(pi, HEAD_SHIFT) == _shr(pj, HEAD_SHIFT)
    head_ones = head_block.astype(BF16)

    def head_sum(x):
        stacked = jnp.concatenate([x[:, p * LANES:(p + 1) * LANES] for p in range(n_pairs)], axis=0)
        s = _dot(stacked.astype(BF16), head_ones)
        return jnp.concatenate([s[p * c:(p + 1) * c] for p in range(n_pairs)], axis=1)

    staged = {}

    def emit(q, s, p, x):
        staged[s, q, p] = x

    last_row = t_valid - 1 if t_valid < c else c - 1
    for q in range(nb):
        if staged_input:
            for p in range(n_pairs):
                for s in range(N_STAGE):
                    emit(q, s, p, st_ref[q, :, s * D_RWKV + p * LANES:s * D_RWKV + (p + 1) * LANES])
                emit(q, N_STAGE, p, gall_ref[q, 0, 0:1, p * LANES:(p + 1) * LANES])
        else:
            for _ in _rwkv_elementwise(q, xc_ref, prev_scr, mu_ref, w0_ref, a0_ref, lora_ref, kk_ref, ka_ref, rk_ref,
                                       head_ones, functools.partial(emit, q), c, t_valid):
                pass
            prev_scr[q] = xc_ref[q, last_row:last_row + 1, :]
    rb, ab, bb, kb, bt, kt, vp, bonus, g_all = ([staged[s, q, p] for q, p in units] for s in range(N_STAGE + 1))
    each = lambda fn, items: [fn(x) for x in items]

    first_half = lax.broadcasted_iota(jnp.int32, (c, LANES), 1) < RWKV_HEAD
    ti = lax.broadcasted_iota(jnp.int32, (c, c), 0)
    tj = lax.broadcasted_iota(jnp.int32, (c, c), 1)
    tri_incl = ti >= tj
    tri_strict = ti > tj
    tri_incl2 = jnp.concatenate([tri_incl, tri_incl], axis=1)
    eye = (ti == tj).astype(F32)
    base_shift = INV_BASE.bit_length() - 1
    blk_mask = _shr(ti, base_shift) == _shr(tj, base_shift)
    merge_masks = []
    sh = base_shift
    while (1 << sh) < c:
        merge_masks.append((_shr(ti, sh + 1) == _shr(tj, sh + 1)) & ((_shr(ti, sh) & 1) == 1) & ((_shr(tj, sh) & 1) == 0))
        sh += 1
    diag128 = pi == pj
    zeros_c = jnp.zeros((c, LANES), F32)
    pick = lambda x0, x1: jnp.where(first_half, x0, x1)
    half = lambda x, n: x[(n % 2) * c:(n % 2 + 1) * c]

    bk = [jnp.concatenate([bb[u], kb[u]], axis=0) for u in nu]
    ga = each(lambda u: _mm(jnp.concatenate([pick(ab[u], 0.0), pick(0.0, ab[u])], axis=0), bk[u], 2, NT), nu)
    gr = each(lambda u: _mm(jnp.concatenate([pick(rb[u], 0.0), pick(0.0, rb[u])], axis=0), bk[u], 1, NT), nu)
    a_ab = [jnp.where(tri_strict, half(ga[n // 2], n)[:, 0:c], 0.0) for n in nh]
    a_ak = [jnp.where(tri_strict, half(ga[n // 2], n)[:, c:2 * c], 0.0) for n in nh]
    lr = [jnp.where(tri_incl2, half(gr[n // 2], n), 0.0) for n in nh]
    t_inv = _tri_inverse_all(a_ab, blk_mask, merge_masks, eye, each)
    akv = each(lambda n: _mm(a_ak[n], vp[n // 2]), nh)
    xh = each(lambda n: _mm(t_inv[n], jnp.concatenate([ab[n // 2], akv[n]], axis=1)), nh)
    a_new = [pick(xh[2 * u][:, 0:LANES], xh[2 * u + 1][:, 0:LANES]) for u in nu]
    u0 = [pick(xh[2 * u][:, LANES:], xh[2 * u + 1][:, LANES:]) for u in nu]
    zmat = [jnp.concatenate([jnp.concatenate([a_new[u], u0[u]], axis=1),
                             jnp.concatenate([zeros_c, vp[u]], axis=1)], axis=0).astype(BF16) for u in nu]
    yh = each(lambda n: _dot(lr[n].astype(BF16), zmat[n // 2]), nh)
    mn = each(lambda u: _dot(jnp.concatenate([bt[u], kt[u]], axis=0).astype(BF16), zmat[u], TN), nu)
    o_units = []
    for u, (q, p) in enumerate(units):
        r_new = rb[u] + pick(yh[2 * u][:, 0:LANES], yh[2 * u + 1][:, 0:LANES])
        o0 = pick(yh[2 * u][:, LANES:], yh[2 * u + 1][:, LANES:])
        m_mat = jnp.where(diag128, g_all[u], 0.0) + jnp.where(head_block, mn[u][:, 0:LANES], 0.0)
        n0 = jnp.where(head_block, mn[u][:, LANES:], 0.0)
        state = _split(state_scr[q, p], 2)
        o_units.append(_dot(r_new.astype(BF16), state[0]) + o0)
        state_scr[q, p] = _mm3(_split(m_mat, 2), state) + n0

    inv_n = 1.0 / RWKV_HEAD
    for q in range(nb):
        sl = slice(q * n_pairs, (q + 1) * n_pairs)
        o = jnp.concatenate(o_units[sl], axis=1)
        mean = head_sum(o) * inv_n
        cen = o - mean
        var = head_sum(cen * cen) * inv_n
        y = cen * lax.rsqrt(var + LNX_EPS) * lnw_ref[...] + lnb_ref[...]
        o_ref[q] = ((y + jnp.concatenate(bonus[sl], axis=1)) * _silu(g_ref[q])).astype(BF16)

    @pl.when(i == pl.num_programs(1) - 1)
    def _():
        pout_ref[...] = state_scr[...]


RWKV_SEQS = 2


def _rwkv(gate, p0, lp, c, t_valid, xc=None, shift_prev=None, staged=None):
    b, t, _ = gate.shape
    vec = lambda n: pl.BlockSpec((1, n), lambda bi, i: (0, 0))
    n_pairs = N_RWKV_HEADS // 2
    nb = RWKV_SEQS if b % RWKV_SEQS == 0 else 1
    seq_block = lambda width: pl.BlockSpec((nb, c, width), lambda bi, i: (bi, i, 0))
    state_block = pl.BlockSpec((nb, n_pairs, LANES, LANES), lambda bi, i: (bi, 0, 0, 0))
    scratch = [pltpu.VMEM((nb, n_pairs, LANES, LANES), F32)]
    if staged is not None:
        st, gall = staged
        in_specs = [seq_block(N_STAGE * D_RWKV), pl.BlockSpec((nb, 1, SUBLANES, D_RWKV), lambda bi, i: (bi, i, 0, 0)),
                    seq_block(D_RWKV), state_block, vec(D_RWKV), vec(D_RWKV)]
        args = [st.reshape(b, t, N_STAGE * D_RWKV), gall.reshape(b, t // c, SUBLANES, D_RWKV), gate, p0, lp["lnx_w"], lp["lnx_b"]]
    else:
        in_specs = [seq_block(D_SHIFT), seq_block(D_RWKV), pl.BlockSpec((nb, 1, D_SHIFT), lambda bi, i: (bi, 0, 0)),
                    state_block, vec(D_SHIFT), vec(D_RWKV), vec(D_RWKV),
                    pl.BlockSpec((2 * LORA, 2 * D_RWKV), lambda bi, i: (0, 0)),
                    vec(D_RWKV), vec(D_RWKV), vec(D_RWKV), vec(D_RWKV), vec(D_RWKV)]
        args = [xc, gate, shift_prev.reshape(b, 1, D_SHIFT), p0, lp["mu"], lp["w0"], lp["a0"], lp["lora"],
                lp["k_k"], lp["k_a"], lp["r_k"], lp["lnx_w"], lp["lnx_b"]]
        scratch = [pltpu.VMEM((nb, 1, D_SHIFT), F32)] + scratch
    return pl.pallas_call(
        functools.partial(_rwkv_kernel, c=c, t_valid=t_valid, staged_input=staged is not None),
        grid=(b // nb, t // c),
        in_specs=in_specs,
        out_specs=[seq_block(D_RWKV), state_block],
        out_shape=[jax.ShapeDtypeStruct((b, t, D_RWKV), BF16),
                   jax.ShapeDtypeStruct((b, n_pairs, LANES, LANES), F32)],
        scratch_shapes=scratch,
        compiler_params=_cparams(("arbitrary", "arbitrary")),
        name="rwkv",
    )(*args)


def _state_to_pairs(s):
    b = s.shape[0]
    pt = jnp.swapaxes(s, -1, -2).reshape(b, N_RWKV_HEADS // 2, 2, RWKV_HEAD, RWKV_HEAD)
    z = jnp.zeros_like(pt[:, :, 0])
    top = jnp.concatenate([pt[:, :, 0], z], axis=-1)
    bot = jnp.concatenate([z, pt[:, :, 1]], axis=-1)
    return jnp.concatenate([top, bot], axis=-2)


def _pairs_to_state(pm):
    b = pm.shape[0]
    h0 = pm[:, :, 0:RWKV_HEAD, 0:RWKV_HEAD]
    h1 = pm[:, :, RWKV_HEAD:, RWKV_HEAD:]
    pt = jnp.stack([h0, h1], axis=2).reshape(b, N_RWKV_HEADS, RWKV_HEAD, RWKV_HEAD)
    return jnp.swapaxes(pt, -1, -2)


def _layer(h, lp, att, pool_hist, shift_prev, wkv0, kv_prefix, pos0, final_w, final, cfg):
    b, t, _ = h.shape
    hist16 = jnp.concatenate([jnp.zeros((b, POOL_HALO - POOL_HIST, D_POOL), F32), pool_hist], axis=1)
    c = RWKV_CHUNK
    tm = cfg["tm_in"]
    fuse = t % tm == 0 and tm % c == 0
    fused = dict(t=t, pos0=pos0, hist16=hist16, pool_w=lp["pool_w"], pool_scale=lp["pool_scale"],
                 shift_prev=shift_prev, lp=lp) if fuse else None
    outs = _inproj(h.reshape(b * t, D_MODEL), lp["norm_w"], lp["w_in"], tm, fused)
    shape3 = lambda x: x.reshape(b, t, x.shape[-1])
    pg, q, kv, ga, xc, gr = (shape3(x) for x in outs[:len(SEGMENTS)])
    sink_rows = _sink_rows(lp["sinks"], cfg["tq"])
    fuse_attn = t % cfg["tm_out"] == 0 and cfg["tm_out"] % cfg["tq"] == 0 and cfg["tq"] == WINDOW and att["bias"].shape[0] == 2
    if fuse_attn:
        ma = None
        attn_args = dict(q=outs[1], kv=outs[2], prefix=kv_prefix, gate=outs[3], bias=att["bias"], sink_rows=sink_rows,
                         t=t, tq=cfg["tq"])
    else:
        ma = _attn(q, kv, kv_prefix, ga, att["bias"], sink_rows, cfg["tq"])
        attn_args = None
    p0 = _state_to_pairs(wkv0)
    if fuse:
        mp = shape3(outs[len(SEGMENTS)])
        mr, p_new = _rwkv(gr, p0, lp, c, c, staged=outs[len(SEGMENTS) + 1:])
    else:
        mp = _pool(pg, hist16, lp["pool_w"], lp["pool_scale"], pos0, cfg["tt_pool"])
        pad = (-t) % c
        xc_in = jnp.pad(xc, ((0, 0), (0, pad), (0, 0)))
        gr_in = jnp.pad(gr, ((0, 0), (0, pad), (0, 0)))
        mr, p_new = _rwkv(gr_in, p0, lp, c, t if pad else c, xc=xc_in, shift_prev=shift_prev)
        mr = mr[:, :t]

    flat = lambda x: x.reshape(b * t, x.shape[-1])
    h_new = _outproj(flat(h), flat(mp), None if fuse_attn else flat(ma), flat(mr), lp["w_out"], final_w, final,
                     cfg["tm_out"], attn_args)
    new_pool = pg[:, -POOL_HIST:, 0:D_POOL]
    kvf = kv[:, -WINDOW:] if t >= WINDOW else jnp.concatenate([kv_prefix[:, t:], kv], axis=1)
    new_k = kvf[:, :, 0:LANES].reshape(b, WINDOW, N_KV_HEADS, HEAD_DIM)
    new_v = kvf[:, :, LANES:].reshape(b, WINDOW, N_KV_HEADS, HEAD_DIM)
    new_shift = xc[:, -1]
    return h_new.reshape(b, t, D_MODEL), (new_pool, new_k, new_v, new_shift, _pairs_to_state(p_new))


def _group_cfg(b, t):
    m = b * t
    tq = min(t, ATT_TILE)
    return {"tm_in": min(m, 256), "tm_out": min(m, 512), "tt_pool": min(t, 512), "tq": tq}


def kernel(x_prompt, x_sample, state_pool, cache_swa_k, cache_swa_v, state_rwkv_shift, state_rwkv_wkv, norm_w, w_in, w_out, pool_w, pool_scale, attn_sinks, rel_bias_table, rwkv_mu, rwkv_w0, rwkv_w_up, rwkv_a0, rwkv_a_up, rwkv_k_k, rwkv_k_a, rwkv_r_k, rwkv_lnx_w, rwkv_lnx_b, final_norm_w):
    bp, tp, _ = x_prompt.shape
    bs, ts, _ = x_sample.shape
    cfg_p = _group_cfg(bp, tp)
    cfg_s = _group_cfg(bs, ts)
    att_p = {"bias": jnp.stack([_attn_bias(rel_bias_table, cfg_p["tq"], WINDOW + cfg_p["tq"], False),
                                _attn_bias(rel_bias_table, cfg_p["tq"], WINDOW + cfg_p["tq"], True)])}
    att_s = {"bias": _attn_bias(rel_bias_table, cfg_s["tq"], WINDOW + cfg_s["tq"], True)[None]}

    hp, hs = x_prompt, x_sample
    prompt_states, sample_states = [], []
    row = lambda x: x.astype(F32).reshape(1, -1)
    zero_lora = jnp.zeros((LORA, D_RWKV), F32)
    for l in range(DEPTH):
        lp = {
            "norm_w": norm_w[l], "w_in": _layer_bf16(w_in, l), "w_out": _layer_bf16(w_out, l),
            "pool_w": pool_w[l].astype(BF16), "pool_scale": pool_scale[l], "sinks": attn_sinks[l],
            "mu": row(rwkv_mu[l]), "w0": row(rwkv_w0[l]), "a0": row(rwkv_a0[l]),
            "lora": jnp.concatenate([jnp.concatenate([rwkv_w_up[l].astype(F32), zero_lora], axis=1),
                                     jnp.concatenate([zero_lora, rwkv_a_up[l].astype(F32)], axis=1)], axis=0),
            "k_k": row(rwkv_k_k[l]), "k_a": row(rwkv_k_a[l]), "r_k": row(rwkv_r_k[l]),
            "lnx_w": row(rwkv_lnx_w[l]), "lnx_b": row(rwkv_lnx_b[l]),
        }
        final = l == DEPTH - 1
        hp, sp = _layer(hp, lp, att_p, jnp.zeros((bp, POOL_HIST, D_POOL), F32), jnp.zeros((bp, D_SHIFT), F32),
                        jnp.zeros((bp, N_RWKV_HEADS, RWKV_HEAD, RWKV_HEAD), F32),
                        jnp.zeros((bp, WINDOW, 4 * HEAD_DIM), F32), 0, final_norm_w, final, cfg_p)
        prefix_s = jnp.concatenate([cache_swa_k[l].reshape(bs, WINDOW, LANES), cache_swa_v[l].reshape(bs, WINDOW, LANES)], axis=-1)
        hs, ss = _layer(hs, lp, att_s, state_pool[l], state_rwkv_shift[l], state_rwkv_wkv[l], prefix_s, PAST_LEN,
                        final_norm_w, final, cfg_s)
        prompt_states.append(sp)
        sample_states.append(ss)
    outs_p = [jnp.stack(x) for x in zip(*prompt_states)]
    outs_s = [jnp.stack(x) for x in zip(*sample_states)]
    return (hp, hs, *outs_p, *outs_s)
```

```python
import functools
import math

import jax
import jax.numpy as jnp
from jax import lax
from jax.experimental import pallas as pl
from jax.experimental.pallas import tpu as pltpu

F32 = jnp.float32
BF16 = jnp.bfloat16

D_MODEL = 2048
DEPTH = 4
PAST_LEN = 1024
CHUNK = 64
D_POOL = 512
POOL_WINDOWS = (2, 4, 8, 16)
POOL_GROUP = 128
POOL_HIST = 15
HEAD_DIM = 64
D_ATTN = 1024
N_Q_HEADS = 16
N_KV_HEADS = 2
WINDOW = 128
WIN_CHUNKS = 2
NUM_BUCKETS = 32
MAX_DISTANCE = 128
NEG = -1e30
LOG2E = 1.0 / math.log(2.0)
D_RWKV = 512
RWKV_HEAD = 64
N_RWKV_HEADS = 8
LORA = 64
D_SHIFT = 3 * D_RWKV + 2 * LORA
NORM_EPS = 1e-5
LNX_EPS = 1e-5 * RWKV_HEAD
SEGMENTS = (2 * D_POOL, D_ATTN, 2 * N_KV_HEADS * HEAD_DIM, D_ATTN, D_SHIFT, D_RWKV)
D_IN = sum(SEGMENTS)

LANES = 128
SUBLANES = 8
POOL_HALO = 16
ATT_TILE = 128
RWKV_CHUNK = 128
VMEM_LIMIT = 56 * 1024 * 1024


def _cparams(sem):
    return pltpu.CompilerParams(dimension_semantics=sem, vmem_limit_bytes=VMEM_LIMIT)


def _sigmoid(x):
    return 0.5 * jnp.tanh(0.5 * x) + 0.5


def _silu(x):
    return x * _sigmoid(x)


def _split(x, terms):
    parts = []
    rem = x
    for _ in range(terms):
        p = rem.astype(BF16)
        parts.append(p)
        rem = rem - p.astype(F32)
    return parts


def _dot(a, b, dims=None):
    if dims is None:
        return jnp.dot(a, b, preferred_element_type=F32)
    return lax.dot_general(a, b, (dims, ((), ())), preferred_element_type=F32)


def _mm(a, b, passes=1, dims=None):
    if passes == 1:
        return _dot(a.astype(BF16), b.astype(BF16), dims)
    ah, al = _split(a, 2)
    if passes == 2:
        bb = b.astype(BF16)
        return _dot(ah, bb, dims) + _dot(al, bb, dims)
    bh, bl = _split(b, 2)
    return _dot(ah, bh, dims) + (_dot(ah, bl, dims) + _dot(al, bh, dims))


HEAD_SHIFT = RWKV_HEAD.bit_length() - 1


def _shr(x, bits):
    return lax.shift_right_logical(x, jnp.full_like(x, bits))


NT = ((1,), (1,))
TN = ((0,), (0,))


CAST_ROWS = 256


def _cast_kernel(w_ref, o_ref):
    o_ref[...] = w_ref[...].astype(BF16)


def _layer_bf16(w, l):
    _, rows, cols = w.shape
    return pl.pallas_call(
        _cast_kernel,
        grid=(rows // CAST_ROWS,),
        in_specs=[pl.BlockSpec((None, CAST_ROWS, cols), lambda i: (l, i, 0))],
        out_specs=pl.BlockSpec((CAST_ROWS, cols), lambda i: (i, 0)),
        out_shape=jax.ShapeDtypeStruct((rows, cols), BF16),
        compiler_params=_cparams(("arbitrary",)),
        name="cast_bf16",
    )(w)


SEG_OFFSETS = tuple(sum(SEGMENTS[:n]) for n in range(len(SEGMENTS)))
FUSE_COLS = 512


class _ChunkView:
    def __init__(self, x, c):
        self.x, self.c = x, c

    def __getitem__(self, idx):
        q, _, cols = idx
        return self.x[q * self.c:(q + 1) * self.c, cols]


class _PrevRowView:
    def __init__(self, x, first, c):
        self.x, self.first, self.c = x, first, c

    def __getitem__(self, idx):
        q, _, cols = idx
        return self.first[:, cols] if q == 0 else self.x[q * self.c - 1:q * self.c, cols]


def _inproj_kernel(h_ref, nw_ref, w_ref, *refs, seq_tiles, pos0, c):
    x = h_ref[...]
    ms = jnp.mean(x * x, axis=-1, keepdims=True)
    xn = (x * lax.rsqrt(ms + NORM_EPS) * nw_ref[...]).astype(BF16)
    tm = x.shape[0]
    fused = seq_tiles > 0
    if fused:
        hist_ref, pw_ref, ps_ref, shift_ref, mu_ref, w0_ref, a0_ref, lora_ref, kk_ref, ka_ref, rk_ref, *refs = refs
        mp_ref, st_ref, gall_ref, halo_scr, prevrow_scr = refs[len(SEGMENTS):]
        it = lax.rem(pl.program_id(0), seq_tiles)
    o_refs = refs[:len(SEGMENTS)]

    def project(n, lo=0, width=None):
        width = SEGMENTS[n] if width is None else width
        seg = jnp.dot(xn, w_ref[:, SEG_OFFSETS[n] + lo:SEG_OFFSETS[n] + lo + width], preferred_element_type=F32)
        o_refs[n][:, lo:lo + width] = seg
        return seg

    if not fused:
        for n in range(len(SEGMENTS)):
            project(n)
        return

    xc = project(4)
    first = jnp.where(it == 0, shift_ref[0], prevrow_scr[...])
    prevrow_scr[...] = xc[tm - 1:tm]
    pi = lax.broadcasted_iota(jnp.int32, (LANES, LANES), 0)
    pj = lax.broadcasted_iota(jnp.int32, (LANES, LANES), 1)
    head_ones = (_shr(pi, HEAD_SHIFT) == _shr(pj, HEAD_SHIFT)).astype(BF16)

    def emit(q, s, p, x):
        if s == N_STAGE:
            gall_ref[q, :, p * LANES:(p + 1) * LANES] = jnp.broadcast_to(x, (SUBLANES, LANES))
        else:
            st_ref[q * c:(q + 1) * c, s * D_RWKV + p * LANES:s * D_RWKV + (p + 1) * LANES] = x

    def mixer_slices():
        for q in range(tm // c):
            yield from _rwkv_elementwise(q, _ChunkView(xc, c), _PrevRowView(xc, first, c), mu_ref, w0_ref, a0_ref,
                                         lora_ref, kk_ref, ka_ref, rk_ref, head_ones, functools.partial(emit, q), c, c)

    slices = mixer_slices()
    pg_parts = []
    for n in (0, 1, 2, 3, 5):
        for lo in range(0, SEGMENTS[n], FUSE_COLS):
            part = project(n, lo, min(FUSE_COLS, SEGMENTS[n] - lo))
            if n == 0:
                pg_parts.append(part)
            next(slices, None)
        if n == 0:
            pg = jnp.concatenate(pg_parts, axis=1)
            p = pg[:, 0:D_POOL]
            halo = jnp.where(it == 0, hist_ref[0], halo_scr[...])
            halo_scr[...] = p[tm - POOL_HALO:tm]
            mp_ref[...] = _pool_math(p, pg[:, D_POOL:2 * D_POOL], halo, pos0 + it * tm, pw_ref, ps_ref)
    for _ in slices:
        pass


def _inproj(h2d, norm_w, w_in_bf16, tm, fused=None):
    m = h2d.shape[0]
    c = RWKV_CHUNK
    in_specs = [
        pl.BlockSpec((tm, D_MODEL), lambda i: (i, 0)),
        pl.BlockSpec((1, D_MODEL), lambda i: (0, 0)),
        pl.BlockSpec(memory_space=pltpu.VMEM),
    ]
    out_specs = [pl.BlockSpec((tm, s), lambda i: (i, 0)) for s in SEGMENTS]
    out_shape = [jax.ShapeDtypeStruct((m, s), F32) for s in SEGMENTS]
    args = [h2d, norm_w.reshape(1, D_MODEL), w_in_bf16]
    scratch = []
    seq_tiles = pos0 = 0
    if fused is not None:
        seq_tiles, pos0, lp = fused["t"] // tm, fused["pos0"], fused["lp"]
        b = m // fused["t"]
        vec = lambda n: pl.BlockSpec((1, n), lambda i: (0, 0))
        in_specs += [pl.BlockSpec((1, POOL_HALO, D_POOL), lambda i: (i // seq_tiles, 0, 0)),
                     pl.BlockSpec((4, POOL_GROUP, POOL_GROUP), lambda i: (0, 0, 0)),
                     vec(D_POOL),
                     pl.BlockSpec((1, 1, D_SHIFT), lambda i: (i // seq_tiles, 0, 0)),
                     vec(D_SHIFT), vec(D_RWKV), vec(D_RWKV),
                     pl.BlockSpec((2 * LORA, 2 * D_RWKV), lambda i: (0, 0)),
                     vec(D_RWKV), vec(D_RWKV), vec(D_RWKV)]
        args += [fused["hist16"], fused["pool_w"], fused["pool_scale"].reshape(1, D_POOL),
                 fused["shift_prev"].reshape(b, 1, D_SHIFT), lp["mu"], lp["w0"], lp["a0"], lp["lora"],
                 lp["k_k"], lp["k_a"], lp["r_k"]]
        out_specs += [pl.BlockSpec((tm, D_POOL), lambda i: (i, 0)),
                      pl.BlockSpec((tm, N_STAGE * D_RWKV), lambda i: (i, 0)),
                      pl.BlockSpec((tm // c, SUBLANES, D_RWKV), lambda i: (i, 0, 0))]
        out_shape += [jax.ShapeDtypeStruct((m, D_POOL), BF16),
                      jax.ShapeDtypeStruct((m, N_STAGE * D_RWKV), F32),
                      jax.ShapeDtypeStruct((m // c, SUBLANES, D_RWKV), F32)]
        scratch = [pltpu.VMEM((POOL_HALO, D_POOL), F32), pltpu.VMEM((1, D_SHIFT), F32)]
    return pl.pallas_call(
        functools.partial(_inproj_kernel, seq_tiles=seq_tiles, pos0=pos0, c=c),
        grid=(m // tm,),
        in_specs=in_specs,
        out_specs=out_specs,
        out_shape=out_shape,
        scratch_shapes=scratch,
        compiler_params=_cparams(("arbitrary",)),
        name="inproj",
    )(*args)


OUT_COLS = 512
ATTN_SLICES = 7


def _outproj_kernel(*refs, final, attn_tiles, tq):
    if not attn_tiles:
        h_ref, mp_ref, ma_ref, mr_ref, w_ref, fw_ref, o_ref = refs
        acc = jnp.dot(mp_ref[...], w_ref[0:D_POOL, :], preferred_element_type=F32)
        acc += jnp.dot(ma_ref[...], w_ref[D_POOL:D_POOL + D_ATTN, :], preferred_element_type=F32)
        acc += jnp.dot(mr_ref[...], w_ref[D_POOL + D_ATTN:, :], preferred_element_type=F32)
        hn = h_ref[...] + acc
    else:
        h_ref, mp_ref, mr_ref, w_ref, fw_ref, q_ref, kv_ref, kvp_ref, pre_ref, g_ref, bias_ref, sink_ref, o_ref, ma_scr = refs
        tm = h_ref.shape[0]
        it = lax.rem(pl.program_id(0), attn_tiles)

        def attention_slices():
            for j in range(tm // tq):
                r = slice(j * tq, (j + 1) * tq)
                if j == 0:
                    kvp = jnp.where(it == 0, pre_ref[0], kvp_ref[...])
                    variant = jnp.minimum(it, 1)
                    bias_get = lambda rows, variant=variant: bias_ref[variant, rows, :]
                else:
                    kvp = kv_ref[j * tq - WINDOW:j * tq, :]
                    bias_get = lambda rows: bias_ref[1, rows, :]

                def put(cols, x, r=r):
                    ma_scr[r, cols] = x

                yield from _attn_tile(lambda cols, r=r: q_ref[r, cols], kvp, kv_ref[r, :], lambda cols, r=r: g_ref[r, cols],
                                      bias_get, sink_ref, put, tq)

        slices = attention_slices()
        blocks = [slice(j * OUT_COLS, (j + 1) * OUT_COLS) for j in range(D_MODEL // OUT_COLS)]
        per_block = -(-(tm // tq) * ATTN_SLICES // (2 * len(blocks)))
        acc = []
        for cols in blocks:
            part = jnp.dot(mp_ref[...], w_ref[0:D_POOL, cols], preferred_element_type=F32)
            for _ in range(per_block):
                next(slices, None)
            part += jnp.dot(mr_ref[...], w_ref[D_POOL + D_ATTN:, cols], preferred_element_type=F32)
            for _ in range(per_block):
                next(slices, None)
            acc.append(part)
        for _ in slices:
            pass
        ma = ma_scr[...]
        hn = jnp.concatenate([h_ref[:, cols] + acc[j] + jnp.dot(ma, w_ref[D_POOL:D_POOL + D_ATTN, cols], preferred_element_type=F32)
                              for j, cols in enumerate(blocks)], axis=1)
    if final:
        ms = jnp.mean(hn * hn, axis=-1, keepdims=True)
        hn = hn * lax.rsqrt(ms + NORM_EPS) * fw_ref[...]
    o_ref[...] = hn


def _outproj(h2d, mp, ma, mr, w_out_bf16, final_w, final, tm, attn=None):
    m = h2d.shape[0]
    row = lambda width: pl.BlockSpec((tm, width), lambda i: (i, 0))
    whole = pl.BlockSpec(memory_space=pltpu.VMEM)
    fw_spec = pl.BlockSpec((1, D_MODEL), lambda i: (0, 0))
    if attn is None:
        in_specs = [row(D_MODEL), row(D_POOL), row(D_ATTN), row(D_RWKV), whole, fw_spec]
        args = [h2d, mp, ma, mr, w_out_bf16, final_w.reshape(1, D_MODEL)]
        scratch, attn_tiles, tq = [], 0, 0
    else:
        tq, attn_tiles = attn["tq"], attn["t"] // tm
        per = tm // WINDOW
        in_specs = [row(D_MODEL), row(D_POOL), row(D_RWKV), whole, fw_spec, row(D_ATTN), row(4 * HEAD_DIM),
                    pl.BlockSpec((WINDOW, 4 * HEAD_DIM), lambda i: (jnp.maximum(i * per - 1, 0), 0)),
                    pl.BlockSpec((1, WINDOW, 4 * HEAD_DIM), lambda i: (i // attn_tiles, 0, 0)),
                    row(D_ATTN), whole, whole]
        args = [h2d, mp, mr, w_out_bf16, final_w.reshape(1, D_MODEL), attn["q"], attn["kv"], attn["kv"], attn["prefix"],
                attn["gate"], attn["bias"], attn["sink_rows"]]
        scratch = [pltpu.VMEM((tm, D_ATTN), BF16)]
    return pl.pallas_call(
        functools.partial(_outproj_kernel, final=final, attn_tiles=attn_tiles, tq=tq),
        grid=(m // tm,),
        in_specs=in_specs,
        out_specs=row(D_MODEL),
        out_shape=jax.ShapeDtypeStruct((m, D_MODEL), F32),
        scratch_shapes=scratch,
        compiler_params=_cparams(("arbitrary",)),
        name="outproj",
    )(*args)


def _pool_math(p, gate, halo, pos_first, pw_ref, ps_ref):
    tt = p.shape[0]
    ext = jnp.concatenate([halo, p], axis=0)
    pos = pos_first + lax.broadcasted_iota(jnp.int32, (tt, 1), 0)
    outs = []
    for g, w in enumerate(POOL_WINDOWS):
        s = ext[:, g * POOL_GROUP:(g + 1) * POOL_GROUP]
        span = 1
        while span < w:
            n = s.shape[0]
            s = s[span:n] + s[0:n - span]
            span *= 2
        win = s[POOL_HALO - (w - 1):POOL_HALO - (w - 1) + tt]
        cnt = jnp.minimum(pos + 1, w).astype(F32)
        d = win / cnt - p[:, g * POOL_GROUP:(g + 1) * POOL_GROUP]
        outs.append(jnp.dot(d.astype(BF16), pw_ref[g], preferred_element_type=F32))
    y = jnp.concatenate(outs, axis=1) * ps_ref[...]
    return (y * _silu(gate)).astype(BF16)


def _pool_kernel(pg_ref, halo_ref, hist_ref, pw_ref, ps_ref, o_ref, *, tt, pos0):
    i = pl.program_id(1)
    halo = jnp.where(i == 0, hist_ref[0], halo_ref[0, :, 0:D_POOL])
    o_ref[0] = _pool_math(pg_ref[0, :, 0:D_POOL], pg_ref[0, :, D_POOL:2 * D_POOL], halo, pos0 + i * tt, pw_ref, ps_ref)


def _pool(pg, hist16, pool_w_bf16, pool_scale, pos0, tt):
    b, t, _ = pg.shape
    nh = tt // POOL_HALO
    return pl.pallas_call(
        functools.partial(_pool_kernel, tt=tt, pos0=pos0),
        grid=(b, t // tt),
        in_specs=[
            pl.BlockSpec((1, tt, 2 * D_POOL), lambda bi, i: (bi, i, 0)),
            pl.BlockSpec((1, POOL_HALO, 2 * D_POOL), lambda bi, i: (bi, jnp.maximum(i * nh - 1, 0), 0)),
            pl.BlockSpec((1, POOL_HALO, D_POOL), lambda bi, i: (bi, 0, 0)),
            pl.BlockSpec((4, POOL_GROUP, POOL_GROUP), lambda bi, i: (0, 0, 0)),
            pl.BlockSpec((1, D_POOL), lambda bi, i: (0, 0)),
        ],
        out_specs=pl.BlockSpec((1, tt, D_POOL), lambda bi, i: (bi, i, 0)),
        out_shape=jax.ShapeDtypeStruct((b, t, D_POOL), BF16),
        compiler_params=_cparams(("arbitrary", "arbitrary")),
        name="pool",
    )(pg, pg, hist16, pool_w_bf16, pool_scale.reshape(1, D_POOL))


def _t5_bucket(rel):
    nb = NUM_BUCKETS // 2
    max_exact = nb // 2
    ret = jnp.where(rel > 0, nb, 0)
    n = jnp.abs(rel)
    nf = jnp.maximum(n, 1).astype(F32)
    large = max_exact + (jnp.log(nf / max_exact) / math.log(MAX_DISTANCE / max_exact) * (nb - max_exact)).astype(jnp.int32)
    large = jnp.minimum(large, nb - 1)
    return ret + jnp.where(n < max_exact, n, large)


def _bias_kernel(bucket_ref, tab_ref, o_ref):
    bucket = bucket_ref[...]
    tab = tab_ref[...]
    out = jnp.full(bucket.shape, NEG, F32)
    for b in range(NUM_BUCKETS):
        out = jnp.where(bucket == b, tab[:, b:b + 1] * LOG2E, out)
    o_ref[...] = out


def _attn_bias(table, tq, nk, prefix_valid):
    qi = jnp.arange(tq)
    kj = jnp.arange(nk)
    rel = kj[None, :] - WINDOW - qi[:, None]
    bucket = _t5_bucket(rel)
    qc = qi // CHUNK
    kc = (kj - WINDOW) // CHUNK
    valid = (kc[None, :] <= qc[:, None]) & (kc[None, :] >= qc[:, None] - WIN_CHUNKS)
    valid &= (kj < WINDOW + tq)[None, :]
    if not prefix_valid:
        valid &= (kj >= WINDOW)[None, :]
    bucket = jnp.where(valid, bucket, -1).astype(jnp.int32)
    heads = jnp.array([[[8 * g + 2 * p + par for p in range(4)] for par in range(2)] for g in range(2)]).reshape(-1)
    rows = heads.shape[0] * tq
    bucket_rows = jnp.broadcast_to(bucket[None], (heads.shape[0], tq, nk)).reshape(rows, nk)
    tab_rows = jnp.broadcast_to(table.astype(F32).T[heads][:, None, :], (heads.shape[0], tq, NUM_BUCKETS)).reshape(rows, NUM_BUCKETS)
    tr = 4 * tq
    return pl.pallas_call(
        _bias_kernel,
        grid=(rows // tr,),
        in_specs=[pl.BlockSpec((tr, nk), lambda i: (i, 0)), pl.BlockSpec((tr, NUM_BUCKETS), lambda i: (i, 0))],
        out_specs=pl.BlockSpec((tr, nk), lambda i: (i, 0)),
        out_shape=jax.ShapeDtypeStruct((rows, nk), F32),
        name="attn_bias",
    )(bucket_rows, tab_rows)


def _sink_rows(sinks, tq):
    heads = jnp.array([[[8 * g + 2 * p + par for p in range(4)] for par in range(2)] for g in range(2)]).reshape(-1)
    return jnp.broadcast_to(sinks.astype(F32)[heads][:, None, None] * LOG2E, (heads.shape[0], tq, LANES)).reshape(-1, LANES)


def _attn_tile(q_get, kvp, kvc, gate_get, bias_get, sink_ref, put, tq):
    kv = jnp.concatenate([kvp, kvc], axis=0)
    nk = kv.shape[0]
    k = kv[:, 0:LANES] * (HEAD_DIM ** -0.5 * LOG2E)
    v = kv[:, LANES:2 * LANES]
    low = lax.broadcasted_iota(jnp.int32, (nk, LANES), 1) < HEAD_DIM
    k_sw = pltpu.roll(k, HEAD_DIM, axis=1)
    v_sw = pltpu.roll(v, HEAD_DIM, axis=1)
    ones = jnp.ones_like(v)

    def place(x, x_sw, g, par):
        src = x if g == par else x_sw
        return jnp.where(low, src, 0.0) if par == 0 else jnp.where(low, 0.0, src)

    blocks = [(g, par) for g in range(N_KV_HEADS) for par in range(2)]
    nblk = range(len(blocks))
    kx = [place(k, k_sw, g, par).astype(BF16) for g, par in blocks]
    vx = [jnp.concatenate([place(v, v_sw, g, par), ones], axis=1).astype(BF16) for g, par in blocks]
    qs = [jnp.concatenate([q_get(slice((4 * g + p) * LANES, (4 * g + p + 1) * LANES)) for p in range(4)], axis=0).astype(BF16)
          for g in range(N_KV_HEADS)]
    rows = [slice(n * 4 * tq, (n + 1) * 4 * tq) for n in nblk]
    yield
    s = [_dot(qs[g], kx[n], NT) + bias_get(rows[n]) for n, (g, par) in enumerate(blocks)]
    yield
    sink = [sink_ref[rows[n], :] for n in nblk]
    m = [jnp.maximum(jnp.broadcast_to(jnp.max(s[n], axis=-1, keepdims=True), (4 * tq, LANES)), sink[n]) for n in nblk]
    yield
    widen = lambda x: jnp.concatenate([x] * (nk // LANES) + ([x[:, 0:nk % LANES]] if nk % LANES else []), axis=1)
    e = [jnp.exp2(s[n] - widen(m[n])).astype(BF16) for n in nblk]
    yield
    pv = [_dot(e[n], vx[n]) for n in nblk]
    den = [pv[n][:, LANES:] + jnp.exp2(sink[n] - m[n]) for n in nblk]
    yield
    even_lanes = lax.broadcasted_iota(jnp.int32, (4 * tq, LANES), 1) < HEAD_DIM
    for g in range(N_KV_HEADS):
        acc = (pv[2 * g][:, 0:LANES] + pv[2 * g + 1][:, 0:LANES]) / jnp.where(even_lanes, den[2 * g], den[2 * g + 1])
        for p in range(4):
            cols = slice((4 * g + p) * LANES, (4 * g + p + 1) * LANES)
            put(cols, (acc[p * tq:(p + 1) * tq] * _silu(gate_get(cols))).astype(BF16))
        yield


def _attn_kernel(*refs, tq, has_prev):
    if has_prev:
        q_ref, kvc_ref, kvp_ref, pre_ref, g_ref, bias_ref, sink_ref, o_ref = refs
    else:
        q_ref, kvc_ref, pre_ref, g_ref, bias_ref, sink_ref, o_ref = refs
    i = pl.program_id(1)
    kvp = jnp.where(i == 0, pre_ref[0], kvp_ref[0]) if has_prev else pre_ref[0]

    def put(cols, x):
        o_ref[0, :, cols] = x

    for _ in _attn_tile(lambda cols: q_ref[0, :, cols], kvp, kvc_ref[0], lambda cols: g_ref[0, :, cols],
                        lambda rows: bias_ref[rows, :], sink_ref, put, tq):
        pass


def _attn(q, kv, prefix, gate, bias, sink_rows, tq):
    b, t, _ = q.shape
    n_tiles = t // tq
    has_prev = n_tiles > 1
    nk = WINDOW + tq
    rows = bias.shape[-2]
    in_specs = [
        pl.BlockSpec((1, tq, D_ATTN), lambda bi, i: (bi, i, 0)),
        pl.BlockSpec((1, tq, 4 * HEAD_DIM), lambda bi, i: (bi, i, 0)),
    ]
    args = [q, kv]
    if has_prev:
        in_specs.append(pl.BlockSpec((1, WINDOW, 4 * HEAD_DIM), lambda bi, i: (bi, jnp.maximum(i - 1, 0), 0)))
        args.append(kv)
    in_specs += [
        pl.BlockSpec((1, WINDOW, 4 * HEAD_DIM), lambda bi, i: (bi, 0, 0)),
        pl.BlockSpec((1, tq, D_ATTN), lambda bi, i: (bi, i, 0)),
        pl.BlockSpec((None, rows, nk), lambda bi, i: (jnp.minimum(i, bias.shape[0] - 1), 0, 0)),
        pl.BlockSpec((rows, LANES), lambda bi, i: (0, 0)),
    ]
    args += [prefix, gate, bias, sink_rows]
    return pl.pallas_call(
        functools.partial(_attn_kernel, tq=tq, has_prev=has_prev),
        grid=(b, n_tiles),
        in_specs=in_specs,
        out_specs=pl.BlockSpec((1, tq, D_ATTN), lambda bi, i: (bi, i, 0)),
        out_shape=jax.ShapeDtypeStruct((b, t, D_ATTN), BF16),
        compiler_params=_cparams(("arbitrary", "arbitrary")),
        name="attn",
    )(*args)


def _mm3(a, b):
    return _dot(a[0], b[0]) + (_dot(a[0], b[1]) + _dot(a[1], b[0]))


def _tri_inverse_all(mats, merge_masks, eye, each):
    idx = range(len(mats))
    t = [eye for _ in mats]
    for mask in merge_masks:
        tb = [x.astype(BF16) for x in t]
        low = each(lambda i: _dot(tb[i], jnp.where(mask, mats[i], 0.0).astype(BF16)).astype(BF16), idx)
        t = each(lambda i: t[i] + _dot(low[i], tb[i]), idx)
    tb = [x.astype(BF16) for x in t]
    t = [x.astype(F32) for x in tb]

    def residual(i):
        a_hi, a_lo = _split(mats[i], 2)
        return ((eye - t[i]) + (_dot(a_hi, tb[i]) + _dot(a_lo, tb[i]))).astype(BF16)

    resid = each(residual, idx)
    return each(lambda i: t[i] + _dot(tb[i], resid[i]), idx)


N_STAGE = 8


def _rwkv_elementwise(q, xc_ref, prev_scr, mu_ref, w0_ref, a0_ref, lora_ref, kk_ref, ka_ref, rk_ref, head_ones, emit, c, t_valid):
    row = lax.broadcasted_iota(jnp.int32, (c, 1), 0)
    live = row < t_valid
    mask = (lambda x: jnp.where(live, x, 0.0)) if t_valid < c else (lambda x: x)

    def shifted(lo_col, width):
        x = xc_ref[q, :, lo_col:lo_col + width]
        prev = jnp.where(row == 0, prev_scr[q, :, lo_col:lo_col + width], pltpu.roll(x, 1, axis=0))
        return x + (prev - x) * mu_ref[:, lo_col:lo_col + width]

    lo = shifted(3 * D_RWKV, 2 * LORA)
    first_half = lax.broadcasted_iota(jnp.int32, (c, LANES), 1) < RWKV_HEAD
    lora = _mm(jnp.where(first_half, jnp.tanh(lo), lo), lora_ref[...], 2)
    tri = (lax.broadcasted_iota(jnp.int32, (c, c), 0) >= lax.broadcasted_iota(jnp.int32, (c, c), 1)).astype(BF16)
    yield
    for p in range(N_RWKV_HEADS // 2):
        cols = slice(p * LANES, (p + 1) * LANES)
        r = shifted(p * LANES, LANES)
        k = shifted(D_RWKV + p * LANES, LANES)
        v = shifted(2 * D_RWKV + p * LANES, LANES)
        emit(6, p, v)
        z = -(w0_ref[:, cols] + lora[:, cols])
        softplus = jnp.maximum(z, 0.0) + jnp.log(1.0 + jnp.exp(-jnp.abs(z)))
        logw = mask(-jnp.exp(-softplus - 0.5))
        cum = None
        for part in _split(logw, 3):
            d = _dot(tri, part)
            cum = d if cum is None else cum + d
        cum_last = cum[c - 1:c]
        a = _sigmoid(a0_ref[:, cols] + lora[:, D_RWKV + p * LANES:D_RWKV + (p + 1) * LANES])
        kk = k * kk_ref[:, cols]
        kk = kk * jnp.minimum(lax.rsqrt(_dot((kk * kk).astype(BF16), head_ones)), 1e12)
        k = k * (1.0 + (a - 1.0) * ka_ref[:, cols])
        emit(7, p, _dot((r * k * rk_ref[:, cols]).astype(BF16), head_ones) * v)
        k = mask(k)
        alpha = mask(-kk)
        beta = mask(kk * a)
        emit(0, p, r * jnp.exp(cum))
        emit(1, p, alpha * jnp.exp(cum - logw))
        g_inv = jnp.exp(-cum)
        emit(2, p, beta * g_inv)
        emit(3, p, k * g_inv)
        g_rest = jnp.exp(cum_last - cum)
        emit(4, p, beta * g_rest)
        emit(5, p, k * g_rest)
        emit(N_STAGE, p, jnp.exp(cum_last))
        yield


def _rwkv_kernel(*refs, c, t_valid, staged_input):
    if staged_input:
        st_ref, gall_ref, g_ref, p0_ref, lnw_ref, lnb_ref, o_ref, pout_ref, state_scr = refs
    else:
        (xc_ref, g_ref, shift_ref, p0_ref, mu_ref, w0_ref, a0_ref, lora_ref, kk_ref, ka_ref, rk_ref,
         lnw_ref, lnb_ref, o_ref, pout_ref, prev_scr, state_scr) = refs
    i = pl.program_id(1)
    nb = g_ref.shape[0]
    n_pairs = N_RWKV_HEADS // 2

    @pl.when(i == 0)
    def _():
        state_scr[...] = p0_ref[...]
        if not staged_input:
            prev_scr[...] = shift_ref[...]

    units = [(q, p) for q in range(nb) for p in range(n_pairs)]
    nu = range(len(units))
    nh = range(2 * len(units))

    pi = lax.broadcasted_iota(jnp.int32, (LANES, LANES), 0)
    pj = lax.broadcasted_iota(jnp.int32, (LANES, LANES), 1)
    head_block = _shr(pi, HEAD_SHIFT) == _shr(pj, HEAD_SHIFT)
    head_ones = head_block.astype(BF16)

    def head_sum(x):
        stacked = jnp.concatenate([x[:, p * LANES:(p + 1) * LANES] for p in range(n_pairs)], axis=0)
        s = _dot(stacked.astype(BF16), head_ones)
        return jnp.concatenate([s[p * c:(p + 1) * c] for p in range(n_pairs)], axis=1)

    staged = {}

    def emit(q, s, p, x):
        staged[s, q, p] = x

    last_row = t_valid - 1 if t_valid < c else c - 1
    for q in range(nb):
        if staged_input:
            for p in range(n_pairs):
                for s in range(N_STAGE):
                    emit(q, s, p, st_ref[q, :, s * D_RWKV + p * LANES:s * D_RWKV + (p + 1) * LANES])
                emit(q, N_STAGE, p, gall_ref[q, 0, 0:1, p * LANES:(p + 1) * LANES])
        else:
            for _ in _rwkv_elementwise(q, xc_ref, prev_scr, mu_ref, w0_ref, a0_ref, lora_ref, kk_ref, ka_ref, rk_ref,
                                       head_ones, functools.partial(emit, q), c, t_valid):
                pass
            prev_scr[q] = xc_ref[q, last_row:last_row + 1, :]
    rb, ab, bb, kb, bt, kt, vp, bonus, g_all = ([staged[s, q, p] for q, p in units] for s in range(N_STAGE + 1))
    each = lambda fn, items: [fn(x) for x in items]

    first_half = lax.broadcasted_iota(jnp.int32, (c, LANES), 1) < RWKV_HEAD
    ti = lax.broadcasted_iota(jnp.int32, (c, c), 0)
    tj = lax.broadcasted_iota(jnp.int32, (c, c), 1)
    tri_incl = ti >= tj
    tri_strict = ti > tj
    tri_incl2 = jnp.concatenate([tri_incl, tri_incl], axis=1)
    eye = (ti == tj).astype(F32)
    merge_masks = []
    sh = 0
    while (1 << sh) < c:
        merge_masks.append((_shr(ti, sh + 1) == _shr(tj, sh + 1)) & ((_shr(ti, sh) & 1) == 1) & ((_shr(tj, sh) & 1) == 0))
        sh += 1
    diag128 = pi == pj
    zeros_c = jnp.zeros((c, LANES), F32)
    pick = lambda x0, x1: jnp.where(first_half, x0, x1)
    half = lambda x, n: x[(n % 2) * c:(n % 2 + 1) * c]

    bk = [jnp.concatenate([bb[u], kb[u]], axis=0) for u in nu]
    ga = each(lambda u: _mm(jnp.concatenate([pick(ab[u], 0.0), pick(0.0, ab[u])], axis=0), bk[u], 3, NT), nu)
    gr = each(lambda u: _mm(jnp.concatenate([pick(rb[u], 0.0), pick(0.0, rb[u])], axis=0), bk[u], 3, NT), nu)
    a_ab = [jnp.where(tri_strict, half(ga[n // 2], n)[:, 0:c], 0.0) for n in nh]
    a_ak = [jnp.where(tri_strict, half(ga[n // 2], n)[:, c:2 * c], 0.0) for n in nh]
    lr = [jnp.where(tri_incl2, half(gr[n // 2], n), 0.0) for n in nh]
    t_inv = _tri_inverse_all(a_ab, merge_masks, eye, each)
    akv = each(lambda n: _mm(a_ak[n], vp[n // 2], 3), nh)
    xh = each(lambda n: _mm(t_inv[n], jnp.concatenate([ab[n // 2], akv[n]], axis=1), 3), nh)
    a_new = [pick(xh[2 * u][:, 0:LANES], xh[2 * u + 1][:, 0:LANES]) for u in nu]
    u0 = [pick(xh[2 * u][:, LANES:], xh[2 * u + 1][:, LANES:]) for u in nu]
    zmat = [jnp.concatenate([jnp.concatenate([a_new[u], u0[u]], axis=1),
                             jnp.concatenate([zeros_c, vp[u]], axis=1)], axis=0) for u in nu]
    yh = each(lambda n: _mm(lr[n], zmat[n // 2], 3), nh)
    mn = each(lambda u: _mm(jnp.concatenate([bt[u], kt[u]], axis=0), zmat[u], 3, TN), nu)
    o_units = []
    for u, (q, p) in enumerate(units):
        r_new = rb[u] + pick(yh[2 * u][:, 0:LANES], yh[2 * u + 1][:, 0:LANES])
        o0 = pick(yh[2 * u][:, LANES:], yh[2 * u + 1][:, LANES:])
        m_mat = jnp.where(diag128, g_all[u], 0.0) + jnp.where(head_block, mn[u][:, 0:LANES], 0.0)
        n0 = jnp.where(head_block, mn[u][:, LANES:], 0.0)
        state = _split(state_scr[q, p], 2)
        o_units.append(_mm3(_split(r_new, 2), state) + o0)
        state_scr[q, p] = _mm3(_split(m_mat, 2), state) + n0

    inv_n = 1.0 / RWKV_HEAD
    for q in range(nb):
        sl = slice(q * n_pairs, (q + 1) * n_pairs)
        o = jnp.concatenate(o_units[sl], axis=1)
        mean = head_sum(o) * inv_n
        cen = o - mean
        var = head_sum(cen * cen) * inv_n
        y = cen * lax.rsqrt(var + LNX_EPS) * lnw_ref[...] + lnb_ref[...]
        o_ref[q] = ((y + jnp.concatenate(bonus[sl], axis=1)) * _silu(g_ref[q])).astype(BF16)

    @pl.when(i == pl.num_programs(1) - 1)
    def _():
        pout_ref[...] = state_scr[...]


RWKV_SEQS = 2


def _rwkv(gate, p0, lp, c, t_valid, xc=None, shift_prev=None, staged=None):
    b, t, _ = gate.shape
    vec = lambda n: pl.BlockSpec((1, n), lambda bi, i: (0, 0))
    n_pairs = N_RWKV_HEADS // 2
    nb = RWKV_SEQS if b % RWKV_SEQS == 0 else 1
    seq_block = lambda width: pl.BlockSpec((nb, c, width), lambda bi, i: (bi, i, 0))
    state_block = pl.BlockSpec((nb, n_pairs, LANES, LANES), lambda bi, i: (bi, 0, 0, 0))
    scratch = [pltpu.VMEM((nb, n_pairs, LANES, LANES), F32)]
    if staged is not None:
        st, gall = staged
        in_specs = [seq_block(N_STAGE * D_RWKV), pl.BlockSpec((nb, 1, SUBLANES, D_RWKV), lambda bi, i: (bi, i, 0, 0)),
                    seq_block(D_RWKV), state_block, vec(D_RWKV), vec(D_RWKV)]
        args = [st.reshape(b, t, N_STAGE * D_RWKV), gall.reshape(b, t // c, SUBLANES, D_RWKV), gate, p0, lp["lnx_w"], lp["lnx_b"]]
    else:
        in_specs = [seq_block(D_SHIFT), seq_block(D_RWKV), pl.BlockSpec((nb, 1, D_SHIFT), lambda bi, i: (bi, 0, 0)),
                    state_block, vec(D_SHIFT), vec(D_RWKV), vec(D_RWKV),
                    pl.BlockSpec((2 * LORA, 2 * D_RWKV), lambda bi, i: (0, 0)),
                    vec(D_RWKV), vec(D_RWKV), vec(D_RWKV), vec(D_RWKV), vec(D_RWKV)]
        args = [xc, gate, shift_prev.reshape(b, 1, D_SHIFT), p0, lp["mu"], lp["w0"], lp["a0"], lp["lora"],
                lp["k_k"], lp["k_a"], lp["r_k"], lp["lnx_w"], lp["lnx_b"]]
        scratch = [pltpu.VMEM((nb, 1, D_SHIFT), F32)] + scratch
    return pl.pallas_call(
        functools.partial(_rwkv_kernel, c=c, t_valid=t_valid, staged_input=staged is not None),
        grid=(b // nb, t // c),
        in_specs=in_specs,
        out_specs=[seq_block(D_RWKV), state_block],
        out_shape=[jax.ShapeDtypeStruct((b, t, D_RWKV), BF16),
                   jax.ShapeDtypeStruct((b, n_pairs, LANES, LANES), F32)],
        scratch_shapes=scratch,
        compiler_params=_cparams(("arbitrary", "arbitrary")),
        name="rwkv",
    )(*args)


def _state_to_pairs(s):
    b = s.shape[0]
    pt = jnp.swapaxes(s, -1, -2).reshape(b, N_RWKV_HEADS // 2, 2, RWKV_HEAD, RWKV_HEAD)
    z = jnp.zeros_like(pt[:, :, 0])
    top = jnp.concatenate([pt[:, :, 0], z], axis=-1)
    bot = jnp.concatenate([z, pt[:, :, 1]], axis=-1)
    return jnp.concatenate([top, bot], axis=-2)


def _pairs_to_state(pm):
    b = pm.shape[0]
    h0 = pm[:, :, 0:RWKV_HEAD, 0:RWKV_HEAD]
    h1 = pm[:, :, RWKV_HEAD:, RWKV_HEAD:]
    pt = jnp.stack([h0, h1], axis=2).reshape(b, N_RWKV_HEADS, RWKV_HEAD, RWKV_HEAD)
    return jnp.swapaxes(pt, -1, -2)


def _layer(h, lp, att, pool_hist, shift_prev, wkv0, kv_prefix, pos0, final_w, final, cfg):
    b, t, _ = h.shape
    hist16 = jnp.concatenate([jnp.zeros((b, POOL_HALO - POOL_HIST, D_POOL), F32), pool_hist], axis=1)
    c = RWKV_CHUNK
    tm = cfg["tm_in"]
    fuse = t % tm == 0 and tm % c == 0
    fused = dict(t=t, pos0=pos0, hist16=hist16, pool_w=lp["pool_w"], pool_scale=lp["pool_scale"],
                 shift_prev=shift_prev, lp=lp) if fuse else None
    outs = _inproj(h.reshape(b * t, D_MODEL), lp["norm_w"], lp["w_in"], tm, fused)
    shape3 = lambda x: x.reshape(b, t, x.shape[-1])
    pg, q, kv, ga, xc, gr = (shape3(x) for x in outs[:len(SEGMENTS)])
    sink_rows = _sink_rows(lp["sinks"], cfg["tq"])
    fuse_attn = t % cfg["tm_out"] == 0 and cfg["tm_out"] % cfg["tq"] == 0 and cfg["tq"] == WINDOW and att["bias"].shape[0] == 2
    if fuse_attn:
        ma = None
        attn_args = dict(q=outs[1], kv=outs[2], prefix=kv_prefix, gate=outs[3], bias=att["bias"], sink_rows=sink_rows,
                         t=t, tq=cfg["tq"])
    else:
        ma = _attn(q, kv, kv_prefix, ga, att["bias"], sink_rows, cfg["tq"])
        attn_args = None
    p0 = _state_to_pairs(wkv0)
    if fuse:
        mp = shape3(outs[len(SEGMENTS)])
        mr, p_new = _rwkv(gr, p0, lp, c, c, staged=outs[len(SEGMENTS) + 1:])
    else:
        mp = _pool(pg, hist16, lp["pool_w"], lp["pool_scale"], pos0, cfg["tt_pool"])
        pad = (-t) % c
        xc_in = jnp.pad(xc, ((0, 0), (0, pad), (0, 0)))
        gr_in = jnp.pad(gr, ((0, 0), (0, pad), (0, 0)))
        mr, p_new = _rwkv(gr_in, p0, lp, c, t if pad else c, xc=xc_in, shift_prev=shift_prev)
        mr = mr[:, :t]

    flat = lambda x: x.reshape(b * t, x.shape[-1])
    h_new = _outproj(flat(h), flat(mp), None if fuse_attn else flat(ma), flat(mr), lp["w_out"], final_w, final,
                     cfg["tm_out"], attn_args)
    new_pool = pg[:, -POOL_HIST:, 0:D_POOL]
    kvf = kv[:, -WINDOW:] if t >= WINDOW else jnp.concatenate([kv_prefix[:, t:], kv], axis=1)
    new_k = kvf[:, :, 0:LANES].reshape(b, WINDOW, N_KV_HEADS, HEAD_DIM)
    new_v = kvf[:, :, LANES:].reshape(b, WINDOW, N_KV_HEADS, HEAD_DIM)
    new_shift = xc[:, -1]
    return h_new.reshape(b, t, D_MODEL), (new_pool, new_k, new_v, new_shift, _pairs_to_state(p_new))


def _group_cfg(b, t):
    m = b * t
    tq = min(t, ATT_TILE)
    return {"tm_in": min(m, 256), "tm_out": min(m, 512), "tt_pool": min(t, 512), "tq": tq}


def kernel(x_prompt, x_sample, state_pool, cache_swa_k, cache_swa_v, state_rwkv_shift, state_rwkv_wkv, norm_w, w_in, w_out, pool_w, pool_scale, attn_sinks, rel_bias_table, rwkv_mu, rwkv_w0, rwkv_w_up, rwkv_a0, rwkv_a_up, rwkv_k_k, rwkv_k_a, rwkv_r_k, rwkv_lnx_w, rwkv_lnx_b, final_norm_w):
    bp, tp, _ = x_prompt.shape
    bs, ts, _ = x_sample.shape
    cfg_p = _group_cfg(bp, tp)
    cfg_s = _group_cfg(bs, ts)
    att_p = {"bias": jnp.stack([_attn_bias(rel_bias_table, cfg_p["tq"], WINDOW + cfg_p["tq"], False),
                                _attn_bias(rel_bias_table, cfg_p["tq"], WINDOW + cfg_p["tq"], True)])}
    att_s = {"bias": _attn_bias(rel_bias_table, cfg_s["tq"], WINDOW + cfg_s["tq"], True)[None]}

    hp, hs = x_prompt, x_sample
    prompt_states, sample_states = [], []
    row = lambda x: x.astype(F32).reshape(1, -1)
    zero_lora = jnp.zeros((LORA, D_RWKV), F32)
    for l in range(DEPTH):
        lp = {
            "norm_w": norm_w[l], "w_in": _layer_bf16(w_in, l), "w_out": _layer_bf16(w_out, l),
            "pool_w": pool_w[l].astype(BF16), "pool_scale": pool_scale[l], "sinks": attn_sinks[l],
            "mu": row(rwkv_mu[l]), "w0": row(rwkv_w0[l]), "a0": row(rwkv_a0[l]),
            "lora": jnp.concatenate([jnp.concatenate([rwkv_w_up[l].astype(F32), zero_lora], axis=1),
                                     jnp.concatenate([zero_lora, rwkv_a_up[l].astype(F32)], axis=1)], axis=0),
            "k_k": row(rwkv_k_k[l]), "k_a": row(rwkv_k_a[l]), "r_k": row(rwkv_r_k[l]),
            "lnx_w": row(rwkv_lnx_w[l]), "lnx_b": row(rwkv_lnx_b[l]),
        }
        final = l == DEPTH - 1
        hp, sp = _layer(hp, lp, att_p, jnp.zeros((bp, POOL_HIST, D_POOL), F32), jnp.zeros((bp, D_SHIFT), F32),
                        jnp.zeros((bp, N_RWKV_HEADS, RWKV_HEAD, RWKV_HEAD), F32),
                        jnp.zeros((bp, WINDOW, 4 * HEAD_DIM), F32), 0, final_norm_w, final, cfg_p)
        prefix_s = jnp.concatenate([cache_swa_k[l].reshape(bs, WINDOW, LANES), cache_swa_v[l].reshape(bs, WINDOW, LANES)], axis=-1)
        hs, ss = _layer(hs, lp, att_s, state_pool[l], state_rwkv_shift[l], state_rwkv_wkv[l], prefix_s, PAST_LEN,
                        final_norm_w, final, cfg_s)
        prompt_states.append(sp)
        sample_states.append(ss)
    outs_p = [jnp.stack(x) for x in zip(*prompt_states)]
    outs_s = [jnp.stack(x) for x in zip(*sample_states)]
    return (hp, hs, *outs_p, *outs_s)
```

```python
import functools
import math

import jax
import jax.numpy as jnp
from jax import lax
from jax.experimental import pallas as pl
from jax.experimental.pallas import tpu as pltpu

F32 = jnp.float32
BF16 = jnp.bfloat16

D_MODEL = 2048
DEPTH = 4
PAST_LEN = 1024
CHUNK = 64
D_POOL = 512
POOL_WINDOWS = (2, 4, 8, 16)
POOL_GROUP = 128
POOL_HIST = 15
HEAD_DIM = 64
D_ATTN = 1024
N_Q_HEADS = 16
N_KV_HEADS = 2
WINDOW = 128
WIN_CHUNKS = 2
NUM_BUCKETS = 32
MAX_DISTANCE = 128
NEG = -1e30
LOG2E = 1.0 / math.log(2.0)
D_RWKV = 512
RWKV_HEAD = 64
N_RWKV_HEADS = 8
LORA = 64
D_SHIFT = 3 * D_RWKV + 2 * LORA
NORM_EPS = 1e-5
LNX_EPS = 1e-5 * RWKV_HEAD
SEGMENTS = (2 * D_POOL, D_ATTN, 2 * N_KV_HEADS * HEAD_DIM, D_ATTN, D_SHIFT, D_RWKV)
D_IN = sum(SEGMENTS)

LANES = 128
SUBLANES = 8
POOL_HALO = 16
ATT_TILE = 128
RWKV_CHUNK = 128
VMEM_LIMIT = 56 * 1024 * 1024


def _cparams(sem):
    return pltpu.CompilerParams(dimension_semantics=sem, vmem_limit_bytes=VMEM_LIMIT)


def _sigmoid(x):
    return 0.5 * jnp.tanh(0.5 * x) + 0.5


def _silu(x):
    return x * _sigmoid(x)


def _split(x, terms):
    parts = []
    rem = x
    for _ in range(terms):
        p = rem.astype(BF16)
        parts.append(p)
        rem = rem - p.astype(F32)
    return parts


def _dot(a, b, dims=None):
    if dims is None:
        return jnp.dot(a, b, preferred_element_type=F32)
    return lax.dot_general(a, b, (dims, ((), ())), preferred_element_type=F32)


def _mm(a, b, passes=1, dims=None):
    if passes == 1:
        return _dot(a.astype(BF16), b.astype(BF16), dims)
    ah, al = _split(a, 2)
    if passes == 2:
        bb = b.astype(BF16)
        return _dot(ah, bb, dims) + _dot(al, bb, dims)
    bh, bl = _split(b, 2)
    return _dot(ah, bh, dims) + (_dot(ah, bl, dims) + _dot(al, bh, dims))


HEAD_SHIFT = RWKV_HEAD.bit_length() - 1


def _shr(x, bits):
    return lax.shift_right_logical(x, jnp.full_like(x, bits))


NT = ((1,), (1,))
TN = ((0,), (0,))


CAST_ROWS = 256


def _cast_kernel(w_ref, o_ref):
    o_ref[...] = w_ref[...].astype(BF16)


def _layer_bf16(w, l):
    _, rows, cols = w.shape
    return pl.pallas_call(
        _cast_kernel,
        grid=(rows // CAST_ROWS,),
        in_specs=[pl.BlockSpec((None, CAST_ROWS, cols), lambda i: (l, i, 0))],
        out_specs=pl.BlockSpec((CAST_ROWS, cols), lambda i: (i, 0)),
        out_shape=jax.ShapeDtypeStruct((rows, cols), BF16),
        compiler_params=_cparams(("arbitrary",)),
        name="cast_bf16",
    )(w)


SEG_OFFSETS = tuple(sum(SEGMENTS[:n]) for n in range(len(SEGMENTS)))
FUSE_COLS = 512


class _ChunkView:
    def __init__(self, x, c):
        self.x, self.c = x, c

    def __getitem__(self, idx):
        q, _, cols = idx
        return self.x[q * self.c:(q + 1) * self.c, cols]


class _PrevRowView:
    def __init__(self, x, first, c):
        self.x, self.first, self.c = x, first, c

    def __getitem__(self, idx):
        q, _, cols = idx
        return self.first[:, cols] if q == 0 else self.x[q * self.c - 1:q * self.c, cols]


def _inproj_kernel(h_ref, nw_ref, w_ref, *refs, seq_tiles, pos0, c):
    x = h_ref[...]
    ms = jnp.mean(x * x, axis=-1, keepdims=True)
    xn = (x * lax.rsqrt(ms + NORM_EPS) * nw_ref[...]).astype(BF16)
    tm = x.shape[0]
    fused = seq_tiles > 0
    if fused:
        hist_ref, pw_ref, ps_ref, shift_ref, mu_ref, w0_ref, a0_ref, lora_ref, kk_ref, ka_ref, rk_ref, *refs = refs
        mp_ref, st_ref, gall_ref, halo_scr, prevrow_scr = refs[len(SEGMENTS):]
        it = lax.rem(pl.program_id(0), seq_tiles)
    o_refs = refs[:len(SEGMENTS)]

    def project(n, lo=0, width=None):
        width = SEGMENTS[n] if width is None else width
        seg = jnp.dot(xn, w_ref[:, SEG_OFFSETS[n] + lo:SEG_OFFSETS[n] + lo + width], preferred_element_type=F32)
        o_refs[n][:, lo:lo + width] = seg
        return seg

    if not fused:
        for n in range(len(SEGMENTS)):
            project(n)
        return

    xc = project(4)
    first = jnp.where(it == 0, shift_ref[0], prevrow_scr[...])
    prevrow_scr[...] = xc[tm - 1:tm]
    pi = lax.broadcasted_iota(jnp.int32, (LANES, LANES), 0)
    pj = lax.broadcasted_iota(jnp.int32, (LANES, LANES), 1)
    head_ones = (_shr(pi, HEAD_SHIFT) == _shr(pj, HEAD_SHIFT)).astype(BF16)

    def emit(q, s, p, x):
        if s == N_STAGE:
            gall_ref[q, :, p * LANES:(p + 1) * LANES] = jnp.broadcast_to(x, (SUBLANES, LANES))
        else:
            st_ref[q * c:(q + 1) * c, s * D_RWKV + p * LANES:s * D_RWKV + (p + 1) * LANES] = x

    def mixer_slices():
        for q in range(tm // c):
            yield from _rwkv_elementwise(q, _ChunkView(xc, c), _PrevRowView(xc, first, c), mu_ref, w0_ref, a0_ref,
                                         lora_ref, kk_ref, ka_ref, rk_ref, head_ones, functools.partial(emit, q), c, c)

    slices = mixer_slices()
    pg_parts = []
    for n in (0, 1, 2, 3, 5):
        for lo in range(0, SEGMENTS[n], FUSE_COLS):
            part = project(n, lo, min(FUSE_COLS, SEGMENTS[n] - lo))
            if n == 0:
                pg_parts.append(part)
            next(slices, None)
        if n == 0:
            pg = jnp.concatenate(pg_parts, axis=1)
            p = pg[:, 0:D_POOL]
            halo = jnp.where(it == 0, hist_ref[0], halo_scr[...])
            halo_scr[...] = p[tm - POOL_HALO:tm]
            mp_ref[...] = _pool_math(p, pg[:, D_POOL:2 * D_POOL], halo, pos0 + it * tm, pw_ref, ps_ref)
    for _ in slices:
        pass


def _inproj(h2d, norm_w, w_in_bf16, tm, fused=None):
    m = h2d.shape[0]
    c = RWKV_CHUNK
    in_specs = [
        pl.BlockSpec((tm, D_MODEL), lambda i: (i, 0)),
        pl.BlockSpec((1, D_MODEL), lambda i: (0, 0)),
        pl.BlockSpec(memory_space=pltpu.VMEM),
    ]
    out_specs = [pl.BlockSpec((tm, s), lambda i: (i, 0)) for s in SEGMENTS]
    out_shape = [jax.ShapeDtypeStruct((m, s), F32) for s in SEGMENTS]
    args = [h2d, norm_w.reshape(1, D_MODEL), w_in_bf16]
    scratch = []
    seq_tiles = pos0 = 0
    if fused is not None:
        seq_tiles, pos0, lp = fused["t"] // tm, fused["pos0"], fused["lp"]
        b = m // fused["t"]
        vec = lambda n: pl.BlockSpec((1, n), lambda i: (0, 0))
        in_specs += [pl.BlockSpec((1, POOL_HALO, D_POOL), lambda i: (i // seq_tiles, 0, 0)),
                     pl.BlockSpec((4, POOL_GROUP, POOL_GROUP), lambda i: (0, 0, 0)),
                     vec(D_POOL),
                     pl.BlockSpec((1, 1, D_SHIFT), lambda i: (i // seq_tiles, 0, 0)),
                     vec(D_SHIFT), vec(D_RWKV), vec(D_RWKV),
                     pl.BlockSpec((2 * LORA, 2 * D_RWKV), lambda i: (0, 0)),
                     vec(D_RWKV), vec(D_RWKV), vec(D_RWKV)]
        args += [fused["hist16"], fused["pool_w"], fused["pool_scale"].reshape(1, D_POOL),
                 fused["shift_prev"].reshape(b, 1, D_SHIFT), lp["mu"], lp["w0"], lp["a0"], lp["lora"],
                 lp["k_k"], lp["k_a"], lp["r_k"]]
        out_specs += [pl.BlockSpec((tm, D_POOL), lambda i: (i, 0)),
                      pl.BlockSpec((tm, N_STAGE * D_RWKV), lambda i: (i, 0)),
                      pl.BlockSpec((tm // c, SUBLANES, D_RWKV), lambda i: (i, 0, 0))]
        out_shape += [jax.ShapeDtypeStruct((m, D_POOL), BF16),
                      jax.ShapeDtypeStruct((m, N_STAGE * D_RWKV), F32),
                      jax.ShapeDtypeStruct((m // c, SUBLANES, D_RWKV), F32)]
        scratch = [pltpu.VMEM((POOL_HALO, D_POOL), F32), pltpu.VMEM((1, D_SHIFT), F32)]
    return pl.pallas_call(
        functools.partial(_inproj_kernel, seq_tiles=seq_tiles, pos0=pos0, c=c),
        grid=(m // tm,),
        in_specs=in_specs,
        out_specs=out_specs,
        out_shape=out_shape,
        scratch_shapes=scratch,
        compiler_params=_cparams(("arbitrary",)),
        name="inproj",
    )(*args)


OUT_COLS = 512
ATTN_SLICES = 7


def _outproj_kernel(*refs, final, attn_tiles, tq):
    if not attn_tiles:
        h_ref, mp_ref, ma_ref, mr_ref, w_ref, fw_ref, o_ref = refs
        acc = jnp.dot(mp_ref[...], w_ref[0:D_POOL, :], preferred_element_type=F32)
        acc += jnp.dot(ma_ref[...], w_ref[D_POOL:D_POOL + D_ATTN, :], preferred_element_type=F32)
        acc += jnp.dot(mr_ref[...], w_ref[D_POOL + D_ATTN:, :], preferred_element_type=F32)
        hn = h_ref[...] + acc
    else:
        h_ref, mp_ref, mr_ref, w_ref, fw_ref, q_ref, kv_ref, kvp_ref, pre_ref, g_ref, bias_ref, sink_ref, o_ref, ma_scr = refs
        tm = h_ref.shape[0]
        it = lax.rem(pl.program_id(0), attn_tiles)

        def attention_slices():
            for j in range(tm // tq):
                r = slice(j * tq, (j + 1) * tq)
                if j == 0:
                    kvp = jnp.where(it == 0, pre_ref[0], kvp_ref[...])
                    variant = jnp.minimum(it, 1)
                    bias_get = lambda rows, variant=variant: bias_ref[variant, rows, :]
                else:
                    kvp = kv_ref[j * tq - WINDOW:j * tq, :]
                    bias_get = lambda rows: bias_ref[1, rows, :]

                def put(cols, x, r=r):
                    ma_scr[r, cols] = x

                yield from _attn_tile(lambda cols, r=r: q_ref[r, cols], kvp, kv_ref[r, :], lambda cols, r=r: g_ref[r, cols],
                                      bias_get, sink_ref, put, tq)

        slices = attention_slices()
        blocks = [slice(j * OUT_COLS, (j + 1) * OUT_COLS) for j in range(D_MODEL // OUT_COLS)]
        per_block = -(-(tm // tq) * ATTN_SLICES // (2 * len(blocks)))
        acc = []
        for cols in blocks:
            part = jnp.dot(mp_ref[...], w_ref[0:D_POOL, cols], preferred_element_type=F32)
            for _ in range(per_block):
                next(slices, None)
            part += jnp.dot(mr_ref[...], w_ref[D_POOL + D_ATTN:, cols], preferred_element_type=F32)
            for _ in range(per_block):
                next(slices, None)
            acc.append(part)
        for _ in slices:
            pass
        ma = ma_scr[...]
        hn = jnp.concatenate([h_ref[:, cols] + acc[j] + jnp.dot(ma, w_ref[D_POOL:D_POOL + D_ATTN, cols], preferred_element_type=F32)
                              for j, cols in enumerate(blocks)], axis=1)
    if final:
        ms = jnp.mean(hn * hn, axis=-1, keepdims=True)
        hn = hn * lax.rsqrt(ms + NORM_EPS) * fw_ref[...]
    o_ref[...] = hn


def _outproj(h2d, mp, ma, mr, w_out_bf16, final_w, final, tm, attn=None):
    m = h2d.shape[0]
    row = lambda width: pl.BlockSpec((tm, width), lambda i: (i, 0))
    whole = pl.BlockSpec(memory_space=pltpu.VMEM)
    fw_spec = pl.BlockSpec((1, D_MODEL), lambda i: (0, 0))
    if attn is None:
        in_specs = [row(D_MODEL), row(D_POOL), row(D_ATTN), row(D_RWKV), whole, fw_spec]
        args = [h2d, mp, ma, mr, w_out_bf16, final_w.reshape(1, D_MODEL)]
        scratch, attn_tiles, tq = [], 0, 0
    else:
        tq, attn_tiles = attn["tq"], attn["t"] // tm
        per = tm // WINDOW
        in_specs = [row(D_MODEL), row(D_POOL), row(D_RWKV), whole, fw_spec, row(D_ATTN), row(4 * HEAD_DIM),
                    pl.BlockSpec((WINDOW, 4 * HEAD_DIM), lambda i: (jnp.maximum(i * per - 1, 0), 0)),
                    pl.BlockSpec((1, WINDOW, 4 * HEAD_DIM), lambda i: (i // attn_tiles, 0, 0)),
                    row(D_ATTN), whole, whole]
        args = [h2d, mp, mr, w_out_bf16, final_w.reshape(1, D_MODEL), attn["q"], attn["kv"], attn["kv"], attn["prefix"],
                attn["gate"], attn["bias"], attn["sink_rows"]]
        scratch = [pltpu.VMEM((tm, D_ATTN), BF16)]
    return pl.pallas_call(
        functools.partial(_outproj_kernel, final=final, attn_tiles=attn_tiles, tq=tq),
        grid=(m // tm,),
        in_specs=in_specs,
        out_specs=row(D_MODEL),
        out_shape=jax.ShapeDtypeStruct((m, D_MODEL), F32),
        scratch_shapes=scratch,
        compiler_params=_cparams(("arbitrary",)),
        name="outproj",
    )(*args)


def _pool_math(p, gate, halo, pos_first, pw_ref, ps_ref):
    tt = p.shape[0]
    ext = jnp.concatenate([halo, p], axis=0)
    pos = pos_first + lax.broadcasted_iota(jnp.int32, (tt, 1), 0)
    outs = []
    for g, w in enumerate(POOL_WINDOWS):
        s = ext[:, g * POOL_GROUP:(g + 1) * POOL_GROUP]
        span = 1
        while span < w:
            n = s.shape[0]
            s = s[span:n] + s[0:n - span]
            span *= 2
        win = s[POOL_HALO - (w - 1):POOL_HALO - (w - 1) + tt]
        cnt = jnp.minimum(pos + 1, w).astype(F32)
        d = win / cnt - p[:, g * POOL_GROUP:(g + 1) * POOL_GROUP]
        outs.append(jnp.dot(d.astype(BF16), pw_ref[g], preferred_element_type=F32))
    y = jnp.concatenate(outs, axis=1) * ps_ref[...]
    return (y * _silu(gate)).astype(BF16)


def _pool_kernel(pg_ref, halo_ref, hist_ref, pw_ref, ps_ref, o_ref, *, tt, pos0):
    i = pl.program_id(1)
    halo = jnp.where(i == 0, hist_ref[0], halo_ref[0, :, 0:D_POOL])
    o_ref[0] = _pool_math(pg_ref[0, :, 0:D_POOL], pg_ref[0, :, D_POOL:2 * D_POOL], halo, pos0 + i * tt, pw_ref, ps_ref)


def _pool(pg, hist16, pool_w_bf16, pool_scale, pos0, tt):
    b, t, _ = pg.shape
    nh = tt // POOL_HALO
    return pl.pallas_call(
        functools.partial(_pool_kernel, tt=tt, pos0=pos0),
        grid=(b, t // tt),
        in_specs=[
            pl.BlockSpec((1, tt, 2 * D_POOL), lambda bi, i: (bi, i, 0)),
            pl.BlockSpec((1, POOL_HALO, 2 * D_POOL), lambda bi, i: (bi, jnp.maximum(i * nh - 1, 0), 0)),
            pl.BlockSpec((1, POOL_HALO, D_POOL), lambda bi, i: (bi, 0, 0)),
            pl.BlockSpec((4, POOL_GROUP, POOL_GROUP), lambda bi, i: (0, 0, 0)),
            pl.BlockSpec((1, D_POOL), lambda bi, i: (0, 0)),
        ],
        out_specs=pl.BlockSpec((1, tt, D_POOL), lambda bi, i: (bi, i, 0)),
        out_shape=jax.ShapeDtypeStruct((b, t, D_POOL), BF16),
        compiler_params=_cparams(("arbitrary", "arbitrary")),
        name="pool",
    )(pg, pg, hist16, pool_w_bf16, pool_scale.reshape(1, D_POOL))


def _t5_bucket(rel):
    nb = NUM_BUCKETS // 2
    max_exact = nb // 2
    ret = jnp.where(rel > 0, nb, 0)
    n = jnp.abs(rel)
    nf = jnp.maximum(n, 1).astype(F32)
    large = max_exact + (jnp.log(nf / max_exact) / math.log(MAX_DISTANCE / max_exact) * (nb - max_exact)).astype(jnp.int32)
    large = jnp.minimum(large, nb - 1)
    return ret + jnp.where(n < max_exact, n, large)


def _bias_kernel(bucket_ref, tab_ref, o_ref):
    bucket = bucket_ref[...]
    tab = tab_ref[...]
    out = jnp.full(bucket.shape, NEG, F32)
    for b in range(NUM_BUCKETS):
        out = jnp.where(bucket == b, tab[:, b:b + 1] * LOG2E, out)
    o_ref[...] = out


def _attn_bias(table, tq, nk, prefix_valid):
    qi = jnp.arange(tq)
    kj = jnp.arange(nk)
    rel = kj[None, :] - WINDOW - qi[:, None]
    bucket = _t5_bucket(rel)
    qc = qi // CHUNK
    kc = (kj - WINDOW) // CHUNK
    valid = (kc[None, :] <= qc[:, None]) & (kc[None, :] >= qc[:, None] - WIN_CHUNKS)
    valid &= (kj < WINDOW + tq)[None, :]
    if not prefix_valid:
        valid &= (kj >= WINDOW)[None, :]
    bucket = jnp.where(valid, bucket, -1).astype(jnp.int32)
    heads = jnp.array([[[8 * g + 2 * p + par for p in range(4)] for par in range(2)] for g in range(2)]).reshape(-1)
    rows = heads.shape[0] * tq
    bucket_rows = jnp.broadcast_to(bucket[None], (heads.shape[0], tq, nk)).reshape(rows, nk)
    tab_rows = jnp.broadcast_to(table.astype(F32).T[heads][:, None, :], (heads.shape[0], tq, NUM_BUCKETS)).reshape(rows, NUM_BUCKETS)
    tr = 4 * tq
    return pl.pallas_call(
        _bias_kernel,
        grid=(rows // tr,),
        in_specs=[pl.BlockSpec((tr, nk), lambda i: (i, 0)), pl.BlockSpec((tr, NUM_BUCKETS), lambda i: (i, 0))],
        out_specs=pl.BlockSpec((tr, nk), lambda i: (i, 0)),
        out_shape=jax.ShapeDtypeStruct((rows, nk), F32),
        name="attn_bias",
    )(bucket_rows, tab_rows)


def _sink_rows(sinks, tq):
    heads = jnp.array([[[8 * g + 2 * p + par for p in range(4)] for par in range(2)] for g in range(2)]).reshape(-1)
    return jnp.broadcast_to(sinks.astype(F32)[heads][:, None, None] * LOG2E, (heads.shape[0], tq, LANES)).reshape(-1, LANES)


def _attn_tile(q_get, kvp, kvc, gate_get, bias_get, sink_ref, put, tq):
    kv = jnp.concatenate([kvp, kvc], axis=0)
    nk = kv.shape[0]
    k = kv[:, 0:LANES] * (HEAD_DIM ** -0.5 * LOG2E)
    v = kv[:, LANES:2 * LANES]
    low = lax.broadcasted_iota(jnp.int32, (nk, LANES), 1) < HEAD_DIM
    k_sw = pltpu.roll(k, HEAD_DIM, axis=1)
    v_sw = pltpu.roll(v, HEAD_DIM, axis=1)
    ones = jnp.ones_like(v)

    def place(x, x_sw, g, par):
        src = x if g == par else x_sw
        return jnp.where(low, src, 0.0) if par == 0 else jnp.where(low, 0.0, src)

    blocks = [(g, par) for g in range(N_KV_HEADS) for par in range(2)]
    nblk = range(len(blocks))
    kx = [place(k, k_sw, g, par).astype(BF16) for g, par in blocks]
    vx = [jnp.concatenate([place(v, v_sw, g, par), ones], axis=1).astype(BF16) for g, par in blocks]
    qs = [jnp.concatenate([q_get(slice((4 * g + p) * LANES, (4 * g + p + 1) * LANES)) for p in range(4)], axis=0).astype(BF16)
          for g in range(N_KV_HEADS)]
    rows = [slice(n * 4 * tq, (n + 1) * 4 * tq) for n in nblk]
    yield
    s = [_dot(qs[g], kx[n], NT) + bias_get(rows[n]) for n, (g, par) in enumerate(blocks)]
    yield
    sink = [sink_ref[rows[n], :] for n in nblk]
    m = [jnp.maximum(jnp.broadcast_to(jnp.max(s[n], axis=-1, keepdims=True), (4 * tq, LANES)), sink[n]) for n in nblk]
    yield
    widen = lambda x: jnp.concatenate([x] * (nk // LANES) + ([x[:, 0:nk % LANES]] if nk % LANES else []), axis=1)
    e = [jnp.exp2(s[n] - widen(m[n])).astype(BF16) for n in nblk]
    yield
    pv = [_dot(e[n], vx[n]) for n in nblk]
    den = [pv[n][:, LANES:] + jnp.exp2(sink[n] - m[n]) for n in nblk]
    yield
    even_lanes = lax.broadcasted_iota(jnp.int32, (4 * tq, LANES), 1) < HEAD_DIM
    for g in range(N_KV_HEADS):
        acc = (pv[2 * g][:, 0:LANES] + pv[2 * g + 1][:, 0:LANES]) / jnp.where(even_lanes, den[2 * g], den[2 * g + 1])
        for p in range(4):
            cols = slice((4 * g + p) * LANES, (4 * g + p + 1) * LANES)
            put(cols, (acc[p * tq:(p + 1) * tq] * _silu(gate_get(cols))).astype(BF16))
        yield


def _attn_kernel(*refs, tq, has_prev):
    if has_prev:
        q_ref, kvc_ref, kvp_ref, pre_ref, g_ref, bias_ref, sink_ref, o_ref = refs
    else:
        q_ref, kvc_ref, pre_ref, g_ref, bias_ref, sink_ref, o_ref = refs
    i = pl.program_id(1)
    kvp = jnp.where(i == 0, pre_ref[0], kvp_ref[0]) if has_prev else pre_ref[0]

    def put(cols, x):
        o_ref[0, :, cols] = x

    for _ in _attn_tile(lambda cols: q_ref[0, :, cols], kvp, kvc_ref[0], lambda cols: g_ref[0, :, cols],
                        lambda rows: bias_ref[rows, :], sink_ref, put, tq):
        pass


def _attn(q, kv, prefix, gate, bias, sink_rows, tq):
    b, t, _ = q.shape
    n_tiles = t // tq
    has_prev = n_tiles > 1
    nk = WINDOW + tq
    rows = bias.shape[-2]
    in_specs = [
        pl.BlockSpec((1, tq, D_ATTN), lambda bi, i: (bi, i, 0)),
        pl.BlockSpec((1, tq, 4 * HEAD_DIM), lambda bi, i: (bi, i, 0)),
    ]
    args = [q, kv]
    if has_prev:
        in_specs.append(pl.BlockSpec((1, WINDOW, 4 * HEAD_DIM), lambda bi, i: (bi, jnp.maximum(i - 1, 0), 0)))
        args.append(kv)
    in_specs += [
        pl.BlockSpec((1, WINDOW, 4 * HEAD_DIM), lambda bi, i: (bi, 0, 0)),
        pl.BlockSpec((1, tq, D_ATTN), lambda bi, i: (bi, i, 0)),
        pl.BlockSpec((None, rows, nk), lambda bi, i: (jnp.minimum(i, bias.shape[0] - 1), 0, 0)),
        pl.BlockSpec((rows, LANES), lambda bi, i: (0, 0)),
    ]
    args += [prefix, gate, bias, sink_rows]
    return pl.pallas_call(
        functools.partial(_attn_kernel, tq=tq, has_prev=has_prev),
        grid=(b, n_tiles),
        in_specs=in_specs,
        out_specs=pl.BlockSpec((1, tq, D_ATTN), lambda bi, i: (bi, i, 0)),
        out_shape=jax.ShapeDtypeStruct((b, t, D_ATTN), BF16),
        compiler_params=_cparams(("arbitrary", "arbitrary")),
        name="attn",
    )(*args)


def _mm3(a, b):
    return _dot(a[0], b[0]) + (_dot(a[0], b[1]) + _dot(a[1], b[0]))


def _tri_inverse_all(mats, merge_masks, eye, each):
    idx = range(len(mats))
    t = [eye + jnp.where(merge_masks[0], a, 0.0) for a in mats]
    for mask in merge_masks[1:]:
        tb = [x.astype(BF16) for x in t]
        low = each(lambda i: _dot(tb[i], jnp.where(mask, mats[i], 0.0).astype(BF16)).astype(BF16), idx)
        t = each(lambda i: t[i] + _dot(low[i], tb[i]), idx)
    tb = [x.astype(BF16) for x in t]
    t = [x.astype(F32) for x in tb]

    def residual(i):
        a_hi, a_lo = _split(mats[i], 2)
        return ((eye - t[i]) + (_dot(a_hi, tb[i]) + _dot(a_lo, tb[i]))).astype(BF16)

    resid = each(residual, idx)
    return each(lambda i: t[i] + _dot(tb[i], resid[i]), idx)


N_STAGE = 8


def _rwkv_elementwise(q, xc_ref, prev_scr, mu_ref, w0_ref, a0_ref, lora_ref, kk_ref, ka_ref, rk_ref, head_ones, emit, c, t_valid):
    row = lax.broadcasted_iota(jnp.int32, (c, 1), 0)
    live = row < t_valid
    mask = (lambda x: jnp.where(live, x, 0.0)) if t_valid < c else (lambda x: x)

    def shifted(lo_col, width):
        x = xc_ref[q, :, lo_col:lo_col + width]
        prev = jnp.where(row == 0, prev_scr[q, :, lo_col:lo_col + width], pltpu.roll(x, 1, axis=0))
        return x + (prev - x) * mu_ref[:, lo_col:lo_col + width]

    lo = shifted(3 * D_RWKV, 2 * LORA)
    first_half = lax.broadcasted_iota(jnp.int32, (c, LANES), 1) < RWKV_HEAD
    lora = _mm(jnp.where(first_half, jnp.tanh(lo), lo), lora_ref[...], 2)
    tri = (lax.broadcasted_iota(jnp.int32, (c, c), 0) >= lax.broadcasted_iota(jnp.int32, (c, c), 1)).astype(BF16)
    yield
    for p in range(N_RWKV_HEADS // 2):
        cols = slice(p * LANES, (p + 1) * LANES)
        r = shifted(p * LANES, LANES)
        k = shifted(D_RWKV + p * LANES, LANES)
        v = shifted(2 * D_RWKV + p * LANES, LANES)
        emit(6, p, v)
        z = -(w0_ref[:, cols] + lora[:, cols])
        softplus = jnp.maximum(z, 0.0) + jnp.log(1.0 + jnp.exp(-jnp.abs(z)))
        logw = mask(-jnp.exp(-softplus - 0.5))
        cum = None
        for part in _split(logw, 3):
            d = _dot(tri, part)
            cum = d if cum is None else cum + d
        cum_last = cum[c - 1:c]
        a = _sigmoid(a0_ref[:, cols] + lora[:, D_RWKV + p * LANES:D_RWKV + (p + 1) * LANES])
        kk = k * kk_ref[:, cols]
        kk = kk * jnp.minimum(lax.rsqrt(_dot((kk * kk).astype(BF16), head_ones)), 1e12)
        k = k * (1.0 + (a - 1.0) * ka_ref[:, cols])
        emit(7, p, _dot((r * k * rk_ref[:, cols]).astype(BF16), head_ones) * v)
        k = mask(k)
        alpha = mask(-kk)
        beta = mask(kk * a)
        emit(0, p, r * jnp.exp(cum))
        emit(1, p, alpha * jnp.exp(cum - logw))
        g_inv = jnp.exp(-cum)
        emit(2, p, beta * g_inv)
        emit(3, p, k * g_inv)
        g_rest = jnp.exp(cum_last - cum)
        emit(4, p, beta * g_rest)
        emit(5, p, k * g_rest)
        emit(N_STAGE, p, jnp.exp(cum_last))
        yield


def _rwkv_kernel(*refs, c, t_valid, staged_input):
    if staged_input:
        st_ref, gall_ref, g_ref, p0_ref, lnw_ref, lnb_ref, o_ref, pout_ref, state_scr = refs
    else:
        (xc_ref, g_ref, shift_ref, p0_ref, mu_ref, w0_ref, a0_ref, lora_ref, kk_ref, ka_ref, rk_ref,
         lnw_ref, lnb_ref, o_ref, pout_ref, prev_scr, state_scr) = refs
    i = pl.program_id(1)
    nb = g_ref.shape[0]
    n_pairs = N_RWKV_HEADS // 2

    @pl.when(i == 0)
    def _():
        state_scr[...] = p0_ref[...]
        if not staged_input:
            prev_scr[...] = shift_ref[...]

    units = [(q, p) for q in range(nb) for p in range(n_pairs)]
    nu = range(len(units))
    nh = range(2 * len(units))

    pi = lax.broadcasted_iota(jnp.int32, (LANES, LANES), 0)
    pj = lax.broadcasted_iota(jnp.int32, (LANES, LANES), 1)
    head_block = _shr(pi, HEAD_SHIFT) == _shr(pj, HEAD_SHIFT)
    head_ones = head_block.astype(BF16)

    def head_sum(x):
        stacked = jnp.concatenate([x[:, p * LANES:(p + 1) * LANES] for p in range(n_pairs)], axis=0)
        s = _dot(stacked.astype(BF16), head_ones)
        return jnp.concatenate([s[p * c:(p + 1) * c] for p in range(n_pairs)], axis=1)

    staged = {}

    def emit(q, s, p, x):
        staged[s, q, p] = x

    last_row = t_valid - 1 if t_valid < c else c - 1
    for q in range(nb):
        if staged_input:
            for p in range(n_pairs):
                for s in range(N_STAGE):
                    emit(q, s, p, st_ref[q, :, s * D_RWKV + p * LANES:s * D_RWKV + (p + 1) * LANES])
                emit(q, N_STAGE, p, gall_ref[q, 0, 0:1, p * LANES:(p + 1) * LANES])
        else:
            for _ in _rwkv_elementwise(q, xc_ref, prev_scr, mu_ref, w0_ref, a0_ref, lora_ref, kk_ref, ka_ref, rk_ref,
                                       head_ones, functools.partial(emit, q), c, t_valid):
                pass
            prev_scr[q] = xc_ref[q, last_row:last_row + 1, :]
    rb, ab, bb, kb, bt, kt, vp, bonus, g_all = ([staged[s, q, p] for q, p in units] for s in range(N_STAGE + 1))
    each = lambda fn, items: [fn(x) for x in items]

    first_half = lax.broadcasted_iota(jnp.int32, (c, LANES), 1) < RWKV_HEAD
    ti = lax.broadcasted_iota(jnp.int32, (c, c), 0)
    tj = lax.broadcasted_iota(jnp.int32, (c, c), 1)
    tri_incl = ti >= tj
    tri_strict = ti > tj
    tri_incl2 = jnp.concatenate([tri_incl, tri_incl], axis=1)
    eye = (ti == tj).astype(F32)
    merge_masks = []
    sh = 0
    while (1 << sh) < c:
        merge_masks.append((_shr(ti, sh + 1) == _shr(tj, sh + 1)) & ((_shr(ti, sh) & 1) == 1) & ((_shr(tj, sh) & 1) == 0))
        sh += 1
    diag128 = pi == pj
    zeros_c = jnp.zeros((c, LANES), F32)
    pick = lambda x0, x1: jnp.where(first_half, x0, x1)
    half = lambda x, n: x[(n % 2) * c:(n % 2 + 1) * c]

    bk = [jnp.concatenate([bb[u], kb[u]], axis=0) for u in nu]
    ga = each(lambda u: _mm(jnp.concatenate([pick(ab[u], 0.0), pick(0.0, ab[u])], axis=0), bk[u], 3, NT), nu)
    gr = each(lambda u: _mm(jnp.concatenate([pick(rb[u], 0.0), pick(0.0, rb[u])], axis=0), bk[u], 3, NT), nu)
    a_ab = [jnp.where(tri_strict, half(ga[n // 2], n)[:, 0:c], 0.0) for n in nh]
    a_ak = [jnp.where(tri_strict, half(ga[n // 2], n)[:, c:2 * c], 0.0) for n in nh]
    lr = [jnp.where(tri_incl2, half(gr[n // 2], n), 0.0) for n in nh]
    t_inv = _tri_inverse_all(a_ab, merge_masks, eye, each)
    akv = each(lambda n: _mm(a_ak[n], vp[n // 2], 3), nh)
    xh = each(lambda n: _mm(t_inv[n], jnp.concatenate([ab[n // 2], akv[n]], axis=1), 3), nh)
    a_new = [pick(xh[2 * u][:, 0:LANES], xh[2 * u + 1][:, 0:LANES]) for u in nu]
    u0 = [pick(xh[2 * u][:, LANES:], xh[2 * u + 1][:, LANES:]) for u in nu]
    zmat = [jnp.concatenate([jnp.concatenate([a_new[u], u0[u]], axis=1),
                             jnp.concatenate([zeros_c, vp[u]], axis=1)], axis=0) for u in nu]
    yh = each(lambda n: _mm(lr[n], zmat[n // 2], 3), nh)
    mn = each(lambda u: _mm(jnp.concatenate([bt[u], kt[u]], axis=0), zmat[u], 3, TN), nu)
    o_units = []
    for u, (q, p) in enumerate(units):
        r_new = rb[u] + pick(yh[2 * u][:, 0:LANES], yh[2 * u + 1][:, 0:LANES])
        o0 = pick(yh[2 * u][:, LANES:], yh[2 * u + 1][:, LANES:])
        m_mat = jnp.where(diag128, g_all[u], 0.0) + jnp.where(head_block, mn[u][:, 0:LANES], 0.0)
        n0 = jnp.where(head_block, mn[u][:, LANES:], 0.0)
        state = _split(state_scr[q, p], 2)
        o_units.append(_mm3(_split(r_new, 2), state) + o0)
        state_scr[q, p] = _mm3(_split(m_mat, 2), state) + n0

    inv_n = 1.0 / RWKV_HEAD
    for q in range(nb):
        sl = slice(q * n_pairs, (q + 1) * n_pairs)
        o = jnp.concatenate(o_units[sl], axis=1)
        mean = head_sum(o) * inv_n
        cen = o - mean
        var = head_sum(cen * cen) * inv_n
        y = cen * lax.rsqrt(var + LNX_EPS) * lnw_ref[...] + lnb_ref[...]
        o_ref[q] = ((y + jnp.concatenate(bonus[sl], axis=1)) * _silu(g_ref[q])).astype(BF16)

    @pl.when(i == pl.num_programs(1) - 1)
    def _():
        pout_ref[...] = state_scr[...]


RWKV_SEQS = 2


def _rwkv(gate, p0, lp, c, t_valid, xc=None, shift_prev=None, staged=None):
    b, t, _ = gate.shape
    vec = lambda n: pl.BlockSpec((1, n), lambda bi, i: (0, 0))
    n_pairs = N_RWKV_HEADS // 2
    nb = RWKV_SEQS if b % RWKV_SEQS == 0 else 1
    seq_block = lambda width: pl.BlockSpec((nb, c, width), lambda bi, i: (bi, i, 0))
    state_block = pl.BlockSpec((nb, n_pairs, LANES, LANES), lambda bi, i: (bi, 0, 0, 0))
    scratch = [pltpu.VMEM((nb, n_pairs, LANES, LANES), F32)]
    if staged is not None:
        st, gall = staged
        in_specs = [seq_block(N_STAGE * D_RWKV), pl.BlockSpec((nb, 1, SUBLANES, D_RWKV), lambda bi, i: (bi, i, 0, 0)),
                    seq_block(D_RWKV), state_block, vec(D_RWKV), vec(D_RWKV)]
        args = [st.reshape(b, t, N_STAGE * D_RWKV), gall.reshape(b, t // c, SUBLANES, D_RWKV), gate, p0, lp["lnx_w"], lp["lnx_b"]]
    else:
        in_specs = [seq_block(D_SHIFT), seq_block(D_RWKV), pl.BlockSpec((nb, 1, D_SHIFT), lambda bi, i: (bi, 0, 0)),
                    state_block, vec(D_SHIFT), vec(D_RWKV), vec(D_RWKV),
                    pl.BlockSpec((2 * LORA, 2 * D_RWKV), lambda bi, i: (0, 0)),
                    vec(D_RWKV), vec(D_RWKV), vec(D_RWKV), vec(D_RWKV), vec(D_RWKV)]
        args = [xc, gate, shift_prev.reshape(b, 1, D_SHIFT), p0, lp["mu"], lp["w0"], lp["a0"], lp["lora"],
                lp["k_k"], lp["k_a"], lp["r_k"], lp["lnx_w"], lp["lnx_b"]]
        scratch = [pltpu.VMEM((nb, 1, D_SHIFT), F32)] + scratch
    return pl.pallas_call(
        functools.partial(_rwkv_kernel, c=c, t_valid=t_valid, staged_input=staged is not None),
        grid=(b // nb, t // c),
        in_specs=in_specs,
        out_specs=[seq_block(D_RWKV), state_block],
        out_shape=[jax.ShapeDtypeStruct((b, t, D_RWKV), BF16),
                   jax.ShapeDtypeStruct((b, n_pairs, LANES, LANES), F32)],
        scratch_shapes=scratch,
        compiler_params=_cparams(("arbitrary", "arbitrary")),
        name="rwkv",
    )(*args)


def _state_to_pairs(s):
    b = s.shape[0]
    pt = jnp.swapaxes(s, -1, -2).reshape(b, N_RWKV_HEADS // 2, 2, RWKV_HEAD, RWKV_HEAD)
    z = jnp.zeros_like(pt[:, :, 0])
    top = jnp.concatenate([pt[:, :, 0], z], axis=-1)
    bot = jnp.concatenate([z, pt[:, :, 1]], axis=-1)
    return jnp.concatenate([top, bot], axis=-2)


def _pairs_to_state(pm):
    b = pm.shape[0]
    h0 = pm[:, :, 0:RWKV_HEAD, 0:RWKV_HEAD]
    h1 = pm[:, :, RWKV_HEAD:, RWKV_HEAD:]
    pt = jnp.stack([h0, h1], axis=2).reshape(b, N_RWKV_HEADS, RWKV_HEAD, RWKV_HEAD)
    return jnp.swapaxes(pt, -1, -2)


def _layer(h, lp, att, pool_hist, shift_prev, wkv0, kv_prefix, pos0, final_w, final, cfg):
    b, t, _ = h.shape
    hist16 = jnp.concatenate([jnp.zeros((b, POOL_HALO - POOL_HIST, D_POOL), F32), pool_hist], axis=1)
    c = RWKV_CHUNK
    tm = cfg["tm_in"]
    fuse = t % tm == 0 and tm % c == 0
    fused = dict(t=t, pos0=pos0, hist16=hist16, pool_w=lp["pool_w"], pool_scale=lp["pool_scale"],
                 shift_prev=shift_prev, lp=lp) if fuse else None
    outs = _inproj(h.reshape(b * t, D_MODEL), lp["norm_w"], lp["w_in"], tm, fused)
    shape3 = lambda x: x.reshape(b, t, x.shape[-1])
    pg, q, kv, ga, xc, gr = (shape3(x) for x in outs[:len(SEGMENTS)])
    sink_rows = _sink_rows(lp["sinks"], cfg["tq"])
    fuse_attn = t % cfg["tm_out"] == 0 and cfg["tm_out"] % cfg["tq"] == 0 and cfg["tq"] == WINDOW and att["bias"].shape[0] == 2
    if fuse_attn:
        ma = None
        attn_args = dict(q=outs[1], kv=outs[2], prefix=kv_prefix, gate=outs[3], bias=att["bias"], sink_rows=sink_rows,
                         t=t, tq=cfg["tq"])
    else:
        ma = _attn(q, kv, kv_prefix, ga, att["bias"], sink_rows, cfg["tq"])
        attn_args = None
    p0 = _state_to_pairs(wkv0)
    if fuse:
        mp = shape3(outs[len(SEGMENTS)])
        mr, p_new = _rwkv(gr, p0, lp, c, c, staged=outs[len(SEGMENTS) + 1:])
    else:
        mp = _pool(pg, hist16, lp["pool_w"], lp["pool_scale"], pos0, cfg["tt_pool"])
        pad = (-t) % c
        xc_in = jnp.pad(xc, ((0, 0), (0, pad), (0, 0)))
        gr_in = jnp.pad(gr, ((0, 0), (0, pad), (0, 0)))
        mr, p_new = _rwkv(gr_in, p0, lp, c, t if pad else c, xc=xc_in, shift_prev=shift_prev)
        mr = mr[:, :t]

    flat = lambda x: x.reshape(b * t, x.shape[-1])
    h_new = _outproj(flat(h), flat(mp), None if fuse_attn else flat(ma), flat(mr), lp["w_out"], final_w, final,
                     cfg["tm_out"], attn_args)
    new_pool = pg[:, -POOL_HIST:, 0:D_POOL]
    kvf = kv[:, -WINDOW:] if t >= WINDOW else jnp.concatenate([kv_prefix[:, t:], kv], axis=1)
    new_k = kvf[:, :, 0:LANES].reshape(b, WINDOW, N_KV_HEADS, HEAD_DIM)
    new_v = kvf[:, :, LANES:].reshape(b, WINDOW, N_KV_HEADS, HEAD_DIM)
    new_shift = xc[:, -1]
    return h_new.reshape(b, t, D_MODEL), (new_pool, new_k, new_v, new_shift, _pairs_to_state(p_new))


def _group_cfg(b, t):
    m = b * t
    tq = min(t, ATT_TILE)
    return {"tm_in": min(m, 256), "tm_out": min(m, 512), "tt_pool": min(t, 512), "tq": tq}


def kernel(x_prompt, x_sample, state_pool, cache_swa_k, cache_swa_v, state_rwkv_shift, state_rwkv_wkv, norm_w, w_in, w_out, pool_w, pool_scale, attn_sinks, rel_bias_table, rwkv_mu, rwkv_w0, rwkv_w_up, rwkv_a0, rwkv_a_up, rwkv_k_k, rwkv_k_a, rwkv_r_k, rwkv_lnx_w, rwkv_lnx_b, final_norm_w):
    bp, tp, _ = x_prompt.shape
    bs, ts, _ = x_sample.shape
    cfg_p = _group_cfg(bp, tp)
    cfg_s = _group_cfg(bs, ts)
    att_p = {"bias": jnp.stack([_attn_bias(rel_bias_table, cfg_p["tq"], WINDOW + cfg_p["tq"], False),
                                _attn_bias(rel_bias_table, cfg_p["tq"], WINDOW + cfg_p["tq"], True)])}
    att_s = {"bias": _attn_bias(rel_bias_table, cfg_s["tq"], WINDOW + cfg_s["tq"], True)[None]}

    hp, hs = x_prompt, x_sample
    prompt_states, sample_states = [], []
    row = lambda x: x.astype(F32).reshape(1, -1)
    zero_lora = jnp.zeros((LORA, D_RWKV), F32)
    for l in range(DEPTH):
        lp = {
            "norm_w": norm_w[l], "w_in": _layer_bf16(w_in, l), "w_out": _layer_bf16(w_out, l),
            "pool_w": pool_w[l].astype(BF16), "pool_scale": pool_scale[l], "sinks": attn_sinks[l],
            "mu": row(rwkv_mu[l]), "w0": row(rwkv_w0[l]), "a0": row(rwkv_a0[l]),
            "lora": jnp.concatenate([jnp.concatenate([rwkv_w_up[l].astype(F32), zero_lora], axis=1),
                                     jnp.concatenate([zero_lora, rwkv_a_up[l].astype(F32)], axis=1)], axis=0),
            "k_k": row(rwkv_k_k[l]), "k_a": row(rwkv_k_a[l]), "r_k": row(rwkv_r_k[l]),
            "lnx_w": row(rwkv_lnx_w[l]), "lnx_b": row(rwkv_lnx_b[l]),
        }
        final = l == DEPTH - 1
        hp, sp = _layer(hp, lp, att_p, jnp.zeros((bp, POOL_HIST, D_POOL), F32), jnp.zeros((bp, D_SHIFT), F32),
                        jnp.zeros((bp, N_RWKV_HEADS, RWKV_HEAD, RWKV_HEAD), F32),
                        jnp.zeros((bp, WINDOW, 4 * HEAD_DIM), F32), 0, final_norm_w, final, cfg_p)
        prefix_s = jnp.concatenate([cache_swa_k[l].reshape(bs, WINDOW, LANES), cache_swa_v[l].reshape(bs, WINDOW, LANES)], axis=-1)
        hs, ss = _layer(hs, lp, att_s, state_pool[l], state_rwkv_shift[l], state_rwkv_wkv[l], prefix_s, PAST_LEN,
                        final_norm_w, final, cfg_s)
        prompt_states.append(sp)
        sample_states.append(ss)
    outs_p = [jnp.stack(x) for x in zip(*prompt_states)]
    outs_s = [jnp.stack(x) for x in zip(*sample_states)]
    return (hp, hs, *outs_p, *outs_s)
```

```python
import functools
import math

import jax
import jax.numpy as jnp
from jax import lax
from jax.experimental import pallas as pl
from jax.experimental.pallas import tpu as pltpu

F32 = jnp.float32
BF16 = jnp.bfloat16

D_MODEL = 2048
DEPTH = 4
PAST_LEN = 1024
CHUNK = 64
D_POOL = 512
POOL_WINDOWS = (2, 4, 8, 16)
POOL_GROUP = 128
POOL_HIST = 15
HEAD_DIM = 64
D_ATTN = 1024
N_Q_HEADS = 16
N_KV_HEADS = 2
WINDOW = 128
WIN_CHUNKS = 2
NUM_BUCKETS = 32
MAX_DISTANCE = 128
NEG = -1e30
LOG2E = 1.0 / math.log(2.0)
D_RWKV = 512
RWKV_HEAD = 64
N_RWKV_HEADS = 8
LORA = 64
D_SHIFT = 3 * D_RWKV + 2 * LORA
NORM_EPS = 1e-5
LNX_EPS = 1e-5 * RWKV_HEAD
SEGMENTS = (2 * D_POOL, D_ATTN, 2 * N_KV_HEADS * HEAD_DIM, D_ATTN, D_SHIFT, D_RWKV)
D_IN = sum(SEGMENTS)

LANES = 128
SUBLANES = 8
POOL_HALO = 16
ATT_TILE = 128
RWKV_CHUNK = 128
VMEM_LIMIT = 56 * 1024 * 1024


def _cparams(sem):
    return pltpu.CompilerParams(dimension_semantics=sem, vmem_limit_bytes=VMEM_LIMIT)


def _sigmoid(x):
    return 0.5 * jnp.tanh(0.5 * x) + 0.5


def _silu(x):
    return x * _sigmoid(x)


def _split(x, terms):
    parts = []
    rem = x
    for _ in range(terms):
        p = rem.astype(BF16)
        parts.append(p)
        rem = rem - p.astype(F32)
    return parts


def _dot(a, b, dims=None):
    if dims is None:
        return jnp.dot(a, b, preferred_element_type=F32)
    return lax.dot_general(a, b, (dims, ((), ())), preferred_element_type=F32)


def _mm(a, b, passes=1, dims=None):
    if passes == 1:
        return _dot(a.astype(BF16), b.astype(BF16), dims)
    ah, al = _split(a, 2)
    if passes == 2:
        bb = b.astype(BF16)
        return _dot(ah, bb, dims) + _dot(al, bb, dims)
    bh, bl = _split(b, 2)
    return _dot(ah, bh, dims) + (_dot(ah, bl, dims) + _dot(al, bh, dims))


HEAD_SHIFT = RWKV_HEAD.bit_length() - 1


def _shr(x, bits):
    return lax.shift_right_logical(x, jnp.full_like(x, bits))


NT = ((1,), (1,))
TN = ((0,), (0,))


CAST_ROWS = 256


def _cast_kernel(w_ref, o_ref):
    o_ref[...] = w_ref[...].astype(BF16)


def _layer_bf16(w, l):
    _, rows, cols = w.shape
    return pl.pallas_call(
        _cast_kernel,
        grid=(rows // CAST_ROWS,),
        in_specs=[pl.BlockSpec((None, CAST_ROWS, cols), lambda i: (l, i, 0))],
        out_specs=pl.BlockSpec((CAST_ROWS, cols), lambda i: (i, 0)),
        out_shape=jax.ShapeDtypeStruct((rows, cols), BF16),
        compiler_params=_cparams(("arbitrary",)),
        name="cast_bf16",
    )(w)


SEG_OFFSETS = tuple(sum(SEGMENTS[:n]) for n in range(len(SEGMENTS)))
FUSE_COLS = 512


class _ChunkView:
    def __init__(self, x, c):
        self.x, self.c = x, c

    def __getitem__(self, idx):
        q, _, cols = idx
        return self.x[q * self.c:(q + 1) * self.c, cols]


class _PrevRowView:
    def __init__(self, x, first, c):
        self.x, self.first, self.c = x, first, c

    def __getitem__(self, idx):
        q, _, cols = idx
        return self.first[:, cols] if q == 0 else self.x[q * self.c - 1:q * self.c, cols]


def _inproj_kernel(h_ref, nw_ref, w_ref, *refs, seq_tiles, pos0, c):
    x = h_ref[...]
    ms = jnp.mean(x * x, axis=-1, keepdims=True)
    xn = (x * lax.rsqrt(ms + NORM_EPS) * nw_ref[...]).astype(BF16)
    tm = x.shape[0]
    fused = seq_tiles > 0
    if fused:
        hist_ref, pw_ref, ps_ref, shift_ref, mu_ref, w0_ref, a0_ref, lora_ref, kk_ref, ka_ref, rk_ref, *refs = refs
        mp_ref, st_ref, gall_ref, halo_scr, prevrow_scr = refs[len(SEGMENTS):]
        it = lax.rem(pl.program_id(0), seq_tiles)
    o_refs = refs[:len(SEGMENTS)]

    def project(n, lo=0, width=None):
        width = SEGMENTS[n] if width is None else width
        seg = jnp.dot(xn, w_ref[:, SEG_OFFSETS[n] + lo:SEG_OFFSETS[n] + lo + width], preferred_element_type=F32)
        o_refs[n][:, lo:lo + width] = seg
        return seg

    if not fused:
        for n in range(len(SEGMENTS)):
            project(n)
        return

    xc = project(4)
    first = jnp.where(it == 0, shift_ref[0], prevrow_scr[...])
    prevrow_scr[...] = xc[tm - 1:tm]
    pi = lax.broadcasted_iota(jnp.int32, (LANES, LANES), 0)
    pj = lax.broadcasted_iota(jnp.int32, (LANES, LANES), 1)
    head_ones = (_shr(pi, HEAD_SHIFT) == _shr(pj, HEAD_SHIFT)).astype(BF16)

    def emit(q, s, p, x):
        if s == N_STAGE:
            gall_ref[q, :, p * LANES:(p + 1) * LANES] = jnp.broadcast_to(x, (SUBLANES, LANES))
        else:
            st_ref[q * c:(q + 1) * c, s * D_RWKV + p * LANES:s * D_RWKV + (p + 1) * LANES] = x

    def mixer_slices():
        for q in range(tm // c):
            yield from _rwkv_elementwise(q, _ChunkView(xc, c), _PrevRowView(xc, first, c), mu_ref, w0_ref, a0_ref,
                                         lora_ref, kk_ref, ka_ref, rk_ref, head_ones, functools.partial(emit, q), c, c)

    slices = mixer_slices()
    pg_parts = []
    for n in (0, 1, 2, 3, 5):
        for lo in range(0, SEGMENTS[n], FUSE_COLS):
            part = project(n, lo, min(FUSE_COLS, SEGMENTS[n] - lo))
            if n == 0:
                pg_parts.append(part)
            next(slices, None)
        if n == 0:
            pg = jnp.concatenate(pg_parts, axis=1)
            p = pg[:, 0:D_POOL]
            halo = jnp.where(it == 0, hist_ref[0], halo_scr[...])
            halo_scr[...] = p[tm - POOL_HALO:tm]
            mp_ref[...] = _pool_math(p, pg[:, D_POOL:2 * D_POOL], halo, pos0 + it * tm, pw_ref, ps_ref)
    for _ in slices:
        pass


def _inproj(h2d, norm_w, w_in_bf16, tm, fused=None):
    m = h2d.shape[0]
    c = RWKV_CHUNK
    in_specs = [
        pl.BlockSpec((tm, D_MODEL), lambda i: (i, 0)),
        pl.BlockSpec((1, D_MODEL), lambda i: (0, 0)),
        pl.BlockSpec(memory_space=pltpu.VMEM),
    ]
    out_specs = [pl.BlockSpec((tm, s), lambda i: (i, 0)) for s in SEGMENTS]
    out_shape = [jax.ShapeDtypeStruct((m, s), F32) for s in SEGMENTS]
    args = [h2d, norm_w.reshape(1, D_MODEL), w_in_bf16]
    scratch = []
    seq_tiles = pos0 = 0
    if fused is not None:
        seq_tiles, pos0, lp = fused["t"] // tm, fused["pos0"], fused["lp"]
        b = m // fused["t"]
        vec = lambda n: pl.BlockSpec((1, n), lambda i: (0, 0))
        in_specs += [pl.BlockSpec((1, POOL_HALO, D_POOL), lambda i: (i // seq_tiles, 0, 0)),
                     pl.BlockSpec((4, POOL_GROUP, POOL_GROUP), lambda i: (0, 0, 0)),
                     vec(D_POOL),
                     pl.BlockSpec((1, 1, D_SHIFT), lambda i: (i // seq_tiles, 0, 0)),
                     vec(D_SHIFT), vec(D_RWKV), vec(D_RWKV),
                     pl.BlockSpec((2 * LORA, 2 * D_RWKV), lambda i: (0, 0)),
                     vec(D_RWKV), vec(D_RWKV), vec(D_RWKV)]
        args += [fused["hist16"], fused["pool_w"], fused["pool_scale"].reshape(1, D_POOL),
                 fused["shift_prev"].reshape(b, 1, D_SHIFT), lp["mu"], lp["w0"], lp["a0"], lp["lora"],
                 lp["k_k"], lp["k_a"], lp["r_k"]]
        out_specs += [pl.BlockSpec((tm, D_POOL), lambda i: (i, 0)),
                      pl.BlockSpec((tm, N_STAGE * D_RWKV), lambda i: (i, 0)),
                      pl.BlockSpec((tm // c, SUBLANES, D_RWKV), lambda i: (i, 0, 0))]
        out_shape += [jax.ShapeDtypeStruct((m, D_POOL), BF16),
                      jax.ShapeDtypeStruct((m, N_STAGE * D_RWKV), F32),
                      jax.ShapeDtypeStruct((m // c, SUBLANES, D_RWKV), F32)]
        scratch = [pltpu.VMEM((POOL_HALO, D_POOL), F32), pltpu.VMEM((1, D_SHIFT), F32)]
    return pl.pallas_call(
        functools.partial(_inproj_kernel, seq_tiles=seq_tiles, pos0=pos0, c=c),
        grid=(m // tm,),
        in_specs=in_specs,
        out_specs=out_specs,
        out_shape=out_shape,
        scratch_shapes=scratch,
        compiler_params=_cparams(("arbitrary",)),
        name="inproj",
    )(*args)


OUT_COLS = 512
ATTN_SLICES = 7
ATTN_GROUPS = 2


def _outproj_kernel(*refs, final, attn_tiles, tq):
    if not attn_tiles:
        h_ref, mp_ref, ma_ref, mr_ref, w_ref, fw_ref, o_ref = refs
        acc = jnp.dot(mp_ref[...], w_ref[0:D_POOL, :], preferred_element_type=F32)
        acc += jnp.dot(ma_ref[...], w_ref[D_POOL:D_POOL + D_ATTN, :], preferred_element_type=F32)
        acc += jnp.dot(mr_ref[...], w_ref[D_POOL + D_ATTN:, :], preferred_element_type=F32)
        hn = h_ref[...] + acc
    else:
        h_ref, mp_ref, mr_ref, w_ref, fw_ref, q_ref, kv_ref, kvp_ref, pre_ref, g_ref, bias_ref, sink_ref, o_ref, ma_scr = refs
        tm = h_ref.shape[0]
        it = lax.rem(pl.program_id(0), attn_tiles)

        def attention_slices(tiles):
            for j in tiles:
                r = slice(j * tq, (j + 1) * tq)
                if j == 0:
                    kvp = jnp.where(it == 0, pre_ref[0], kvp_ref[...])
                    variant = jnp.minimum(it, 1)
                    bias_get = lambda rows, variant=variant: bias_ref[variant, rows, :]
                else:
                    kvp = kv_ref[j * tq - WINDOW:j * tq, :]
                    bias_get = lambda rows: bias_ref[1, rows, :]

                def put(cols, x, r=r):
                    ma_scr[r, cols] = x

                yield from _attn_tile(lambda cols, r=r: q_ref[r, cols], kvp, kv_ref[r, :], lambda cols, r=r: g_ref[r, cols],
                                      bias_get, sink_ref, put, tq)

        def issue(slices, count):
            for _ in range(count):
                next(slices, None)

        n_tiles = tm // tq
        per_group = n_tiles // ATTN_GROUPS
        groups = [slice(g * per_group * tq, (g + 1) * per_group * tq) for g in range(ATTN_GROUPS)]
        stages = [attention_slices(range(g * per_group, (g + 1) * per_group)) for g in range(ATTN_GROUPS)]
        blocks = [slice(j * OUT_COLS, (j + 1) * OUT_COLS) for j in range(D_MODEL // OUT_COLS)]
        n_slices = per_group * ATTN_SLICES
        w_attn = lambda cols: w_ref[D_POOL:D_POOL + D_ATTN, cols]
        acc = []
        for cols in blocks:
            part = jnp.dot(mp_ref[...], w_ref[0:D_POOL, cols], preferred_element_type=F32)
            issue(stages[0], -(-n_slices // (2 * len(blocks))))
            part += jnp.dot(mr_ref[...], w_ref[D_POOL + D_ATTN:, cols], preferred_element_type=F32)
            issue(stages[0], -(-n_slices // (2 * len(blocks))))
            acc.append(part)
        out_rows = []
        for g, rows in enumerate(groups):
            for _ in stages[g]:
                pass
            parts = []
            for j, cols in enumerate(blocks):
                parts.append(h_ref[rows, cols] + acc[j][rows] + jnp.dot(ma_scr[rows, :], w_attn(cols), preferred_element_type=F32))
                if g + 1 < ATTN_GROUPS:
                    issue(stages[g + 1], -(-n_slices // len(blocks)))
            out_rows.append(jnp.concatenate(parts, axis=1))
        hn = jnp.concatenate(out_rows, axis=0)
    if final:
        ms = jnp.mean(hn * hn, axis=-1, keepdims=True)
        hn = hn * lax.rsqrt(ms + NORM_EPS) * fw_ref[...]
    o_ref[...] = hn


def _outproj(h2d, mp, ma, mr, w_out_bf16, final_w, final, tm, attn=None):
    m = h2d.shape[0]
    row = lambda width: pl.BlockSpec((tm, width), lambda i: (i, 0))
    whole = pl.BlockSpec(memory_space=pltpu.VMEM)
    fw_spec = pl.BlockSpec((1, D_MODEL), lambda i: (0, 0))
    if attn is None:
        in_specs = [row(D_MODEL), row(D_POOL), row(D_ATTN), row(D_RWKV), whole, fw_spec]
        args = [h2d, mp, ma, mr, w_out_bf16, final_w.reshape(1, D_MODEL)]
        scratch, attn_tiles, tq = [], 0, 0
    else:
        tq, attn_tiles = attn["tq"], attn["t"] // tm
        per = tm // WINDOW
        in_specs = [row(D_MODEL), row(D_POOL), row(D_RWKV), whole, fw_spec, row(D_ATTN), row(4 * HEAD_DIM),
                    pl.BlockSpec((WINDOW, 4 * HEAD_DIM), lambda i: (jnp.maximum(i * per - 1, 0), 0)),
                    pl.BlockSpec((1, WINDOW, 4 * HEAD_DIM), lambda i: (i // attn_tiles, 0, 0)),
                    row(D_ATTN), whole, whole]
        args = [h2d, mp, mr, w_out_bf16, final_w.reshape(1, D_MODEL), attn["q"], attn["kv"], attn["kv"], attn["prefix"],
                attn["gate"], attn["bias"], attn["sink_rows"]]
        scratch = [pltpu.VMEM((tm, D_ATTN), BF16)]
    return pl.pallas_call(
        functools.partial(_outproj_kernel, final=final, attn_tiles=attn_tiles, tq=tq),
        grid=(m // tm,),
        in_specs=in_specs,
        out_specs=row(D_MODEL),
        out_shape=jax.ShapeDtypeStruct((m, D_MODEL), F32),
        scratch_shapes=scratch,
        compiler_params=_cparams(("arbitrary",)),
        name="outproj",
    )(*args)


def _pool_math(p, gate, halo, pos_first, pw_ref, ps_ref):
    tt = p.shape[0]
    ext = jnp.concatenate([halo, p], axis=0)
    pos = pos_first + lax.broadcasted_iota(jnp.int32, (tt, 1), 0)
    outs = []
    for g, w in enumerate(POOL_WINDOWS):
        s = ext[:, g * POOL_GROUP:(g + 1) * POOL_GROUP]
        span = 1
        while span < w:
            n = s.shape[0]
            s = s[span:n] + s[0:n - span]
            span *= 2
        win = s[POOL_HALO - (w - 1):POOL_HALO - (w - 1) + tt]
        cnt = jnp.minimum(pos + 1, w).astype(F32)
        d = win / cnt - p[:, g * POOL_GROUP:(g + 1) * POOL_GROUP]
        outs.append(jnp.dot(d.astype(BF16), pw_ref[g], preferred_element_type=F32))
    y = jnp.concatenate(outs, axis=1) * ps_ref[...]
    return (y * _silu(gate)).astype(BF16)


def _pool_kernel(pg_ref, halo_ref, hist_ref, pw_ref, ps_ref, o_ref, *, tt, pos0):
    i = pl.program_id(1)
    halo = jnp.where(i == 0, hist_ref[0], halo_ref[0, :, 0:D_POOL])
    o_ref[0] = _pool_math(pg_ref[0, :, 0:D_POOL], pg_ref[0, :, D_POOL:2 * D_POOL], halo, pos0 + i * tt, pw_ref, ps_ref)


def _pool(pg, hist16, pool_w_bf16, pool_scale, pos0, tt):
    b, t, _ = pg.shape
    nh = tt // POOL_HALO
    return pl.pallas_call(
        functools.partial(_pool_kernel, tt=tt, pos0=pos0),
        grid=(b, t // tt),
        in_specs=[
            pl.BlockSpec((1, tt, 2 * D_POOL), lambda bi, i: (bi, i, 0)),
            pl.BlockSpec((1, POOL_HALO, 2 * D_POOL), lambda bi, i: (bi, jnp.maximum(i * nh - 1, 0), 0)),
            pl.BlockSpec((1, POOL_HALO, D_POOL), lambda bi, i: (bi, 0, 0)),
            pl.BlockSpec((4, POOL_GROUP, POOL_GROUP), lambda bi, i: (0, 0, 0)),
            pl.BlockSpec((1, D_POOL), lambda bi, i: (0, 0)),
        ],
        out_specs=pl.BlockSpec((1, tt, D_POOL), lambda bi, i: (bi, i, 0)),
        out_shape=jax.ShapeDtypeStruct((b, t, D_POOL), BF16),
        compiler_params=_cparams(("arbitrary", "arbitrary")),
        name="pool",
    )(pg, pg, hist16, pool_w_bf16, pool_scale.reshape(1, D_POOL))


def _t5_bucket(rel):
    nb = NUM_BUCKETS // 2
    max_exact = nb // 2
    ret = jnp.where(rel > 0, nb, 0)
    n = jnp.abs(rel)
    nf = jnp.maximum(n, 1).astype(F32)
    large = max_exact + (jnp.log(nf / max_exact) / math.log(MAX_DISTANCE / max_exact) * (nb - max_exact)).astype(jnp.int32)
    large = jnp.minimum(large, nb - 1)
    return ret + jnp.where(n < max_exact, n, large)


def _bias_kernel(bucket_ref, tab_ref, o_ref):
    bucket = bucket_ref[...]
    tab = tab_ref[...]
    out = jnp.full(bucket.shape, NEG, F32)
    for b in range(NUM_BUCKETS):
        out = jnp.where(bucket == b, tab[:, b:b + 1] * LOG2E, out)
    o_ref[...] = out


def _attn_bias(table, tq, nk, prefix_valid):
    qi = jnp.arange(tq)
    kj = jnp.arange(nk)
    rel = kj[None, :] - WINDOW - qi[:, None]
    bucket = _t5_bucket(rel)
    qc = qi // CHUNK
    kc = (kj - WINDOW) // CHUNK
    valid = (kc[None, :] <= qc[:, None]) & (kc[None, :] >= qc[:, None] - WIN_CHUNKS)
    valid &= (kj < WINDOW + tq)[None, :]
    if not prefix_valid:
        valid &= (kj >= WINDOW)[None, :]
    bucket = jnp.where(valid, bucket, -1).astype(jnp.int32)
    heads = jnp.array([[[8 * g + 2 * p + par for p in range(4)] for par in range(2)] for g in range(2)]).reshape(-1)
    rows = heads.shape[0] * tq
    bucket_rows = jnp.broadcast_to(bucket[None], (heads.shape[0], tq, nk)).reshape(rows, nk)
    tab_rows = jnp.broadcast_to(table.astype(F32).T[heads][:, None, :], (heads.shape[0], tq, NUM_BUCKETS)).reshape(rows, NUM_BUCKETS)
    tr = 4 * tq
    return pl.pallas_call(
        _bias_kernel,
        grid=(rows // tr,),
        in_specs=[pl.BlockSpec((tr, nk), lambda i: (i, 0)), pl.BlockSpec((tr, NUM_BUCKETS), lambda i: (i, 0))],
        out_specs=pl.BlockSpec((tr, nk), lambda i: (i, 0)),
        out_shape=jax.ShapeDtypeStruct((rows, nk), F32),
        name="attn_bias",
    )(bucket_rows, tab_rows)


def _sink_rows(sinks, tq):
    heads = jnp.array([[[8 * g + 2 * p + par for p in range(4)] for par in range(2)] for g in range(2)]).reshape(-1)
    return jnp.broadcast_to(sinks.astype(F32)[heads][:, None, None] * LOG2E, (heads.shape[0], tq, LANES)).reshape(-1, LANES)


def _attn_tile(q_get, kvp, kvc, gate_get, bias_get, sink_ref, put, tq):
    kv = jnp.concatenate([kvp, kvc], axis=0)
    nk = kv.shape[0]
    k = kv[:, 0:LANES] * (HEAD_DIM ** -0.5 * LOG2E)
    v = kv[:, LANES:2 * LANES]
    low = lax.broadcasted_iota(jnp.int32, (nk, LANES), 1) < HEAD_DIM
    k_sw = pltpu.roll(k, HEAD_DIM, axis=1)
    v_sw = pltpu.roll(v, HEAD_DIM, axis=1)
    ones = jnp.ones_like(v)

    def place(x, x_sw, g, par):
        src = x if g == par else x_sw
        return jnp.where(low, src, 0.0) if par == 0 else jnp.where(low, 0.0, src)

    blocks = [(g, par) for g in range(N_KV_HEADS) for par in range(2)]
    nblk = range(len(blocks))
    kx = [place(k, k_sw, g, par).astype(BF16) for g, par in blocks]
    vx = [jnp.concatenate([place(v, v_sw, g, par), ones], axis=1).astype(BF16) for g, par in blocks]
    qs = [jnp.concatenate([q_get(slice((4 * g + p) * LANES, (4 * g + p + 1) * LANES)) for p in range(4)], axis=0).astype(BF16)
          for g in range(N_KV_HEADS)]
    rows = [slice(n * 4 * tq, (n + 1) * 4 * tq) for n in nblk]
    yield
    s = [_dot(qs[g], kx[n], NT) + bias_get(rows[n]) for n, (g, par) in enumerate(blocks)]
    yield
    sink = [sink_ref[rows[n], :] for n in nblk]
    m = [jnp.maximum(jnp.broadcast_to(jnp.max(s[n], axis=-1, keepdims=True), (4 * tq, LANES)), sink[n]) for n in nblk]
    yield
    widen = lambda x: jnp.concatenate([x] * (nk // LANES) + ([x[:, 0:nk % LANES]] if nk % LANES else []), axis=1)
    e = [jnp.exp2(s[n] - widen(m[n])).astype(BF16) for n in nblk]
    yield
    pv = [_dot(e[n], vx[n]) for n in nblk]
    den = [pv[n][:, LANES:] + jnp.exp2(sink[n] - m[n]) for n in nblk]
    yield
    even_lanes = lax.broadcasted_iota(jnp.int32, (4 * tq, LANES), 1) < HEAD_DIM
    for g in range(N_KV_HEADS):
        acc = (pv[2 * g][:, 0:LANES] + pv[2 * g + 1][:, 0:LANES]) / jnp.where(even_lanes, den[2 * g], den[2 * g + 1])
        for p in range(4):
            cols = slice((4 * g + p) * LANES, (4 * g + p + 1) * LANES)
            put(cols, (acc[p * tq:(p + 1) * tq] * _silu(gate_get(cols))).astype(BF16))
        yield


def _attn_kernel(*refs, tq, has_prev):
    if has_prev:
        q_ref, kvc_ref, kvp_ref, pre_ref, g_ref, bias_ref, sink_ref, o_ref = refs
    else:
        q_ref, kvc_ref, pre_ref, g_ref, bias_ref, sink_ref, o_ref = refs
    i = pl.program_id(1)
    kvp = jnp.where(i == 0, pre_ref[0], kvp_ref[0]) if has_prev else pre_ref[0]

    def put(cols, x):
        o_ref[0, :, cols] = x

    for _ in _attn_tile(lambda cols: q_ref[0, :, cols], kvp, kvc_ref[0], lambda cols: g_ref[0, :, cols],
                        lambda rows: bias_ref[rows, :], sink_ref, put, tq):
        pass


def _attn(q, kv, prefix, gate, bias, sink_rows, tq):
    b, t, _ = q.shape
    n_tiles = t // tq
    has_prev = n_tiles > 1
    nk = WINDOW + tq
    rows = bias.shape[-2]
    in_specs = [
        pl.BlockSpec((1, tq, D_ATTN), lambda bi, i: (bi, i, 0)),
        pl.BlockSpec((1, tq, 4 * HEAD_DIM), lambda bi, i: (bi, i, 0)),
    ]
    args = [q, kv]
    if has_prev:
        in_specs.append(pl.BlockSpec((1, WINDOW, 4 * HEAD_DIM), lambda bi, i: (bi, jnp.maximum(i - 1, 0), 0)))
        args.append(kv)
    in_specs += [
        pl.BlockSpec((1, WINDOW, 4 * HEAD_DIM), lambda bi, i: (bi, 0, 0)),
        pl.BlockSpec((1, tq, D_ATTN), lambda bi, i: (bi, i, 0)),
        pl.BlockSpec((None, rows, nk), lambda bi, i: (jnp.minimum(i, bias.shape[0] - 1), 0, 0)),
        pl.BlockSpec((rows, LANES), lambda bi, i: (0, 0)),
    ]
    args += [prefix, gate, bias, sink_rows]
    return pl.pallas_call(
        functools.partial(_attn_kernel, tq=tq, has_prev=has_prev),
        grid=(b, n_tiles),
        in_specs=in_specs,
        out_specs=pl.BlockSpec((1, tq, D_ATTN), lambda bi, i: (bi, i, 0)),
        out_shape=jax.ShapeDtypeStruct((b, t, D_ATTN), BF16),
        compiler_params=_cparams(("arbitrary", "arbitrary")),
        name="attn",
    )(*args)


def _mm3(a, b):
    return _dot(a[0], b[0]) + (_dot(a[0], b[1]) + _dot(a[1], b[0]))


def _tri_inverse_all(mats, merge_masks, eye, each):
    idx = range(len(mats))
    t = [eye + jnp.where(merge_masks[0], a, 0.0) for a in mats]
    for mask in merge_masks[1:]:
        tb = [x.astype(BF16) for x in t]
        low = each(lambda i: _dot(tb[i], jnp.where(mask, mats[i], 0.0).astype(BF16)).astype(BF16), idx)
        t = each(lambda i: t[i] + _dot(low[i], tb[i]), idx)
    tb = [x.astype(BF16) for x in t]
    t = [x.astype(F32) for x in tb]

    def residual(i):
        a_hi, a_lo = _split(mats[i], 2)
        return ((eye - t[i]) + (_dot(a_hi, tb[i]) + _dot(a_lo, tb[i]))).astype(BF16)

    resid = each(residual, idx)
    return each(lambda i: t[i] + _dot(tb[i], resid[i]), idx)


N_STAGE = 8


def _rwkv_elementwise(q, xc_ref, prev_scr, mu_ref, w0_ref, a0_ref, lora_ref, kk_ref, ka_ref, rk_ref, head_ones, emit, c, t_valid):
    row = lax.broadcasted_iota(jnp.int32, (c, 1), 0)
    live = row < t_valid
    mask = (lambda x: jnp.where(live, x, 0.0)) if t_valid < c else (lambda x: x)

    def shifted(lo_col, width):
        x = xc_ref[q, :, lo_col:lo_col + width]
        prev = jnp.where(row == 0, prev_scr[q, :, lo_col:lo_col + width], pltpu.roll(x, 1, axis=0))
        return x + (prev - x) * mu_ref[:, lo_col:lo_col + width]

    lo = shifted(3 * D_RWKV, 2 * LORA)
    first_half = lax.broadcasted_iota(jnp.int32, (c, LANES), 1) < RWKV_HEAD
    lora = _mm(jnp.where(first_half, jnp.tanh(lo), lo), lora_ref[...], 2)
    tri = (lax.broadcasted_iota(jnp.int32, (c, c), 0) >= lax.broadcasted_iota(jnp.int32, (c, c), 1)).astype(BF16)
    yield
    for p in range(N_RWKV_HEADS // 2):
        cols = slice(p * LANES, (p + 1) * LANES)
        r = shifted(p * LANES, LANES)
        k = shifted(D_RWKV + p * LANES, LANES)
        v = shifted(2 * D_RWKV + p * LANES, LANES)
        emit(6, p, v)
        z = -(w0_ref[:, cols] + lora[:, cols])
        softplus = jnp.maximum(z, 0.0) + jnp.log(1.0 + jnp.exp(-jnp.abs(z)))
        logw = mask(-jnp.exp(-softplus - 0.5))
        cum = None
        for part in _split(logw, 3):
            d = _dot(tri, part)
            cum = d if cum is None else cum + d
        cum_last = cum[c - 1:c]
        a = _sigmoid(a0_ref[:, cols] + lora[:, D_RWKV + p * LANES:D_RWKV + (p + 1) * LANES])
        kk = k * kk_ref[:, cols]
        kk = kk * jnp.minimum(lax.rsqrt(_dot((kk * kk).astype(BF16), head_ones)), 1e12)
        k = k * (1.0 + (a - 1.0) * ka_ref[:, cols])
        emit(7, p, _dot((r * k * rk_ref[:, cols]).astype(BF16), head_ones) * v)
        k = mask(k)
        alpha = mask(-kk)
        beta = mask(kk * a)
        emit(0, p, r * jnp.exp(cum))
        emit(1, p, alpha * jnp.exp(cum - logw))
        g_inv = jnp.exp(-cum)
        emit(2, p, beta * g_inv)
        emit(3, p, k * g_inv)
        g_rest = jnp.exp(cum_last - cum)
        emit(4, p, beta * g_rest)
        emit(5, p, k * g_rest)
        emit(N_STAGE, p, jnp.exp(cum_last))
        yield


def _rwkv_kernel(*refs, c, t_valid, staged_input):
    if staged_input:
        st_ref, gall_ref, g_ref, p0_ref, lnw_ref, lnb_ref, o_ref, pout_ref, state_scr = refs
    else:
        (xc_ref, g_ref, shift_ref, p0_ref, mu_ref, w0_ref, a0_ref, lora_ref, kk_ref, ka_ref, rk_ref,
         lnw_ref, lnb_ref, o_ref, pout_ref, prev_scr, state_scr) = refs
    i = pl.program_id(1)
    nb = g_ref.shape[0]
    n_pairs = N_RWKV_HEADS // 2

    @pl.when(i == 0)
    def _():
        state_scr[...] = p0_ref[...]
        if not staged_input:
            prev_scr[...] = shift_ref[...]

    units = [(q, p) for q in range(nb) for p in range(n_pairs)]
    nu = range(len(units))
    nh = range(2 * len(units))

    pi = lax.broadcasted_iota(jnp.int32, (LANES, LANES), 0)
    pj = lax.broadcasted_iota(jnp.int32, (LANES, LANES), 1)
    head_block = _shr(pi, HEAD_SHIFT) == _shr(pj, HEAD_SHIFT)
    head_ones = head_block.astype(BF16)

    def head_sum(x):
        stacked = jnp.concatenate([x[:, p * LANES:(p + 1) * LANES] for p in range(n_pairs)], axis=0)
        s = _dot(stacked.astype(BF16), head_ones)
        return jnp.concatenate([s[p * c:(p + 1) * c] for p in range(n_pairs)], axis=1)

    staged = {}

    def emit(q, s, p, x):
        staged[s, q, p] = x

    last_row = t_valid - 1 if t_valid < c else c - 1
    for q in range(nb):
        if staged_input:
            for p in range(n_pairs):
                for s in range(N_STAGE):
                    emit(q, s, p, st_ref[q, :, s * D_RWKV + p * LANES:s * D_RWKV + (p + 1) * LANES])
                emit(q, N_STAGE, p, gall_ref[q, 0, 0:1, p * LANES:(p + 1) * LANES])
        else:
            for _ in _rwkv_elementwise(q, xc_ref, prev_scr, mu_ref, w0_ref, a0_ref, lora_ref, kk_ref, ka_ref, rk_ref,
                                       head_ones, functools.partial(emit, q), c, t_valid):
                pass
            prev_scr[q] = xc_ref[q, last_row:last_row + 1, :]
    rb, ab, bb, kb, bt, kt, vp, bonus, g_all = ([staged[s, q, p] for q, p in units] for s in range(N_STAGE + 1))
    each = lambda fn, items: [fn(x) for x in items]

    first_half = lax.broadcasted_iota(jnp.int32, (c, LANES), 1) < RWKV_HEAD
    ti = lax.broadcasted_iota(jnp.int32, (c, c), 0)
    tj = lax.broadcasted_iota(jnp.int32, (c, c), 1)
    tri_incl = ti >= tj
    tri_strict = ti > tj
    tri_incl2 = jnp.concatenate([tri_incl, tri_incl], axis=1)
    eye = (ti == tj).astype(F32)
    merge_masks = []
    sh = 0
    while (1 << sh) < c:
        merge_masks.append((_shr(ti, sh + 1) == _shr(tj, sh + 1)) & ((_shr(ti, sh) & 1) == 1) & ((_shr(tj, sh) & 1) == 0))
        sh += 1
    diag128 = pi == pj
    zeros_c = jnp.zeros((c, LANES), F32)
    pick = lambda x0, x1: jnp.where(first_half, x0, x1)
    half = lambda x, n: x[(n % 2) * c:(n % 2 + 1) * c]

    bk = [jnp.concatenate([bb[u], kb[u]], axis=0) for u in nu]
    ga = each(lambda u: _mm(jnp.concatenate([pick(ab[u], 0.0), pick(0.0, ab[u])], axis=0), bk[u], 3, NT), nu)
    gr = each(lambda u: _mm(jnp.concatenate([pick(rb[u], 0.0), pick(0.0, rb[u])], axis=0), bk[u], 3, NT), nu)
    a_ab = [jnp.where(tri_strict, half(ga[n // 2], n)[:, 0:c], 0.0) for n in nh]
    a_ak = [jnp.where(tri_strict, half(ga[n // 2], n)[:, c:2 * c], 0.0) for n in nh]
    lr = [jnp.where(tri_incl2, half(gr[n // 2], n), 0.0) for n in nh]
    t_inv = _tri_inverse_all(a_ab, merge_masks, eye, each)
    akv = each(lambda n: _mm(a_ak[n], vp[n // 2], 3), nh)
    xh = each(lambda n: _mm(t_inv[n], jnp.concatenate([ab[n // 2], akv[n]], axis=1), 3), nh)
    a_new = [pick(xh[2 * u][:, 0:LANES], xh[2 * u + 1][:, 0:LANES]) for u in nu]
    u0 = [pick(xh[2 * u][:, LANES:], xh[2 * u + 1][:, LANES:]) for u in nu]
    zmat = [jnp.concatenate([jnp.concatenate([a_new[u], u0[u]], axis=1),
                             jnp.concatenate([zeros_c, vp[u]], axis=1)], axis=0) for u in nu]
    yh = each(lambda n: _mm(lr[n], zmat[n // 2], 3), nh)
    mn = each(lambda u: _mm(jnp.concatenate([bt[u], kt[u]], axis=0), zmat[u], 3, TN), nu)
    o_units = []
    for u, (q, p) in enumerate(units):
        r_new = rb[u] + pick(yh[2 * u][:, 0:LANES], yh[2 * u + 1][:, 0:LANES])
        o0 = pick(yh[2 * u][:, LANES:], yh[2 * u + 1][:, LANES:])
        m_mat = jnp.where(diag128, g_all[u], 0.0) + jnp.where(head_block, mn[u][:, 0:LANES], 0.0)
        n0 = jnp.where(head_block, mn[u][:, LANES:], 0.0)
        state = _split(state_scr[q, p], 2)
        o_units.append(_mm3(_split(r_new, 2), state) + o0)
        state_scr[q, p] = _mm3(_split(m_mat, 2), state) + n0

    inv_n = 1.0 / RWKV_HEAD
    for q in range(nb):
        sl = slice(q * n_pairs, (q + 1) * n_pairs)
        o = jnp.concatenate(o_units[sl], axis=1)
        mean = head_sum(o) * inv_n
        cen = o - mean
        var = head_sum(cen * cen) * inv_n
        y = cen * lax.rsqrt(var + LNX_EPS) * lnw_ref[...] + lnb_ref[...]
        o_ref[q] = ((y + jnp.concatenate(bonus[sl], axis=1)) * _silu(g_ref[q])).astype(BF16)

    @pl.when(i == pl.num_programs(1) - 1)
    def _():
        pout_ref[...] = state_scr[...]


RWKV_SEQS = 2


def _rwkv(gate, p0, lp, c, t_valid, xc=None, shift_prev=None, staged=None):
    b, t, _ = gate.shape
    vec = lambda n: pl.BlockSpec((1, n), lambda bi, i: (0, 0))
    n_pairs = N_RWKV_HEADS // 2
    nb = RWKV_SEQS if b % RWKV_SEQS == 0 else 1
    seq_block = lambda width: pl.BlockSpec((nb, c, width), lambda bi, i: (bi, i, 0))
    state_block = pl.BlockSpec((nb, n_pairs, LANES, LANES), lambda bi, i: (bi, 0, 0, 0))
    scratch = [pltpu.VMEM((nb, n_pairs, LANES, LANES), F32)]
    if staged is not None:
        st, gall = staged
        in_specs = [seq_block(N_STAGE * D_RWKV), pl.BlockSpec((nb, 1, SUBLANES, D_RWKV), lambda bi, i: (bi, i, 0, 0)),
                    seq_block(D_RWKV), state_block, vec(D_RWKV), vec(D_RWKV)]
        args = [st.reshape(b, t, N_STAGE * D_RWKV), gall.reshape(b, t // c, SUBLANES, D_RWKV), gate, p0, lp["lnx_w"], lp["lnx_b"]]
    else:
        in_specs = [seq_block(D_SHIFT), seq_block(D_RWKV), pl.BlockSpec((nb, 1, D_SHIFT), lambda bi, i: (bi, 0, 0)),
                    state_block, vec(D_SHIFT), vec(D_RWKV), vec(D_RWKV),
                    pl.BlockSpec((2 * LORA, 2 * D_RWKV), lambda bi, i: (0, 0)),
                    vec(D_RWKV), vec(D_RWKV), vec(D_RWKV), vec(D_RWKV), vec(D_RWKV)]
        args = [xc, gate, shift_prev.reshape(b, 1, D_SHIFT), p0, lp["mu"], lp["w0"], lp["a0"], lp["lora"],
                lp["k_k"], lp["k_a"], lp["r_k"], lp["lnx_w"], lp["lnx_b"]]
        scratch = [pltpu.VMEM((nb, 1, D_SHIFT), F32)] + scratch
    return pl.pallas_call(
        functools.partial(_rwkv_kernel, c=c, t_valid=t_valid, staged_input=staged is not None),
        grid=(b // nb, t // c),
        in_specs=in_specs,
        out_specs=[seq_block(D_RWKV), state_block],
        out_shape=[jax.ShapeDtypeStruct((b, t, D_RWKV), BF16),
                   jax.ShapeDtypeStruct((b, n_pairs, LANES, LANES), F32)],
        scratch_shapes=scratch,
        compiler_params=_cparams(("arbitrary", "arbitrary")),
        name="rwkv",
    )(*args)


def _state_to_pairs(s):
    b = s.shape[0]
    pt = jnp.swapaxes(s, -1, -2).reshape(b, N_RWKV_HEADS // 2, 2, RWKV_HEAD, RWKV_HEAD)
    z = jnp.zeros_like(pt[:, :, 0])
    top = jnp.concatenate([pt[:, :, 0], z], axis=-1)
    bot = jnp.concatenate([z, pt[:, :, 1]], axis=-1)
    return jnp.concatenate([top, bot], axis=-2)


def _pairs_to_state(pm):
    b = pm.shape[0]
    h0 = pm[:, :, 0:RWKV_HEAD, 0:RWKV_HEAD]
    h1 = pm[:, :, RWKV_HEAD:, RWKV_HEAD:]
    pt = jnp.stack([h0, h1], axis=2).reshape(b, N_RWKV_HEADS, RWKV_HEAD, RWKV_HEAD)
    return jnp.swapaxes(pt, -1, -2)


def _layer(h, lp, att, pool_hist, shift_prev, wkv0, kv_prefix, pos0, final_w, final, cfg):
    b, t, _ = h.shape
    hist16 = jnp.concatenate([jnp.zeros((b, POOL_HALO - POOL_HIST, D_POOL), F32), pool_hist], axis=1)
    c = RWKV_CHUNK
    tm = cfg["tm_in"]
    fuse = t % tm == 0 and tm % c == 0
    fused = dict(t=t, pos0=pos0, hist16=hist16, pool_w=lp["pool_w"], pool_scale=lp["pool_scale"],
                 shift_prev=shift_prev, lp=lp) if fuse else None
    outs = _inproj(h.reshape(b * t, D_MODEL), lp["norm_w"], lp["w_in"], tm, fused)
    shape3 = lambda x: x.reshape(b, t, x.shape[-1])
    pg, q, kv, ga, xc, gr = (shape3(x) for x in outs[:len(SEGMENTS)])
    sink_rows = _sink_rows(lp["sinks"], cfg["tq"])
    fuse_attn = t % cfg["tm_out"] == 0 and cfg["tm_out"] % cfg["tq"] == 0 and cfg["tq"] == WINDOW and att["bias"].shape[0] == 2
    if fuse_attn:
        ma = None
        attn_args = dict(q=outs[1], kv=outs[2], prefix=kv_prefix, gate=outs[3], bias=att["bias"], sink_rows=sink_rows,
                         t=t, tq=cfg["tq"])
    else:
        ma = _attn(q, kv, kv_prefix, ga, att["bias"], sink_rows, cfg["tq"])
        attn_args = None
    p0 = _state_to_pairs(wkv0)
    if fuse:
        mp = shape3(outs[len(SEGMENTS)])
        mr, p_new = _rwkv(gr, p0, lp, c, c, staged=outs[len(SEGMENTS) + 1:])
    else:
        mp = _pool(pg, hist16, lp["pool_w"], lp["pool_scale"], pos0, cfg["tt_pool"])
        pad = (-t) % c
        xc_in = jnp.pad(xc, ((0, 0), (0, pad), (0, 0)))
        gr_in = jnp.pad(gr, ((0, 0), (0, pad), (0, 0)))
        mr, p_new = _rwkv(gr_in, p0, lp, c, t if pad else c, xc=xc_in, shift_prev=shift_prev)
        mr = mr[:, :t]

    flat = lambda x: x.reshape(b * t, x.shape[-1])
    h_new = _outproj(flat(h), flat(mp), None if fuse_attn else flat(ma), flat(mr), lp["w_out"], final_w, final,
                     cfg["tm_out"], attn_args)
    new_pool = pg[:, -POOL_HIST:, 0:D_POOL]
    kvf = kv[:, -WINDOW:] if t >= WINDOW else jnp.concatenate([kv_prefix[:, t:], kv], axis=1)
    new_k = kvf[:, :, 0:LANES].reshape(b, WINDOW, N_KV_HEADS, HEAD_DIM)
    new_v = kvf[:, :, LANES:].reshape(b, WINDOW, N_KV_HEADS, HEAD_DIM)
    new_shift = xc[:, -1]
    return h_new.reshape(b, t, D_MODEL), (new_pool, new_k, new_v, new_shift, _pairs_to_state(p_new))


def _group_cfg(b, t):
    m = b * t
    tq = min(t, ATT_TILE)
    return {"tm_in": min(m, 256), "tm_out": min(m, 512), "tt_pool": min(t, 512), "tq": tq}


def kernel(x_prompt, x_sample, state_pool, cache_swa_k, cache_swa_v, state_rwkv_shift, state_rwkv_wkv, norm_w, w_in, w_out, pool_w, pool_scale, attn_sinks, rel_bias_table, rwkv_mu, rwkv_w0, rwkv_w_up, rwkv_a0, rwkv_a_up, rwkv_k_k, rwkv_k_a, rwkv_r_k, rwkv_lnx_w, rwkv_lnx_b, final_norm_w):
    bp, tp, _ = x_prompt.shape
    bs, ts, _ = x_sample.shape
    cfg_p = _group_cfg(bp, tp)
    cfg_s = _group_cfg(bs, ts)
    att_p = {"bias": jnp.stack([_attn_bias(rel_bias_table, cfg_p["tq"], WINDOW + cfg_p["tq"], False),
                                _attn_bias(rel_bias_table, cfg_p["tq"], WINDOW + cfg_p["tq"], True)])}
    att_s = {"bias": _attn_bias(rel_bias_table, cfg_s["tq"], WINDOW + cfg_s["tq"], True)[None]}

    hp, hs = x_prompt, x_sample
    prompt_states, sample_states = [], []
    row = lambda x: x.astype(F32).reshape(1, -1)
    zero_lora = jnp.zeros((LORA, D_RWKV), F32)
    for l in range(DEPTH):
        lp = {
            "norm_w": norm_w[l], "w_in": _layer_bf16(w_in, l), "w_out": _layer_bf16(w_out, l),
            "pool_w": pool_w[l].astype(BF16), "pool_scale": pool_scale[l], "sinks": attn_sinks[l],
            "mu": row(rwkv_mu[l]), "w0": row(rwkv_w0[l]), "a0": row(rwkv_a0[l]),
            "lora": jnp.concatenate([jnp.concatenate([rwkv_w_up[l].astype(F32), zero_lora], axis=1),
                                     jnp.concatenate([zero_lora, rwkv_a_up[l].astype(F32)], axis=1)], axis=0),
            "k_k": row(rwkv_k_k[l]), "k_a": row(rwkv_k_a[l]), "r_k": row(rwkv_r_k[l]),
            "lnx_w": row(rwkv_lnx_w[l]), "lnx_b": row(rwkv_lnx_b[l]),
        }
        final = l == DEPTH - 1
        hp, sp = _layer(hp, lp, att_p, jnp.zeros((bp, POOL_HIST, D_POOL), F32), jnp.zeros((bp, D_SHIFT), F32),
                        jnp.zeros((bp, N_RWKV_HEADS, RWKV_HEAD, RWKV_HEAD), F32),
                        jnp.zeros((bp, WINDOW, 4 * HEAD_DIM), F32), 0, final_norm_w, final, cfg_p)
        prefix_s = jnp.concatenate([cache_swa_k[l].reshape(bs, WINDOW, LANES), cache_swa_v[l].reshape(bs, WINDOW, LANES)], axis=-1)
        hs, ss = _layer(hs, lp, att_s, state_pool[l], state_rwkv_shift[l], state_rwkv_wkv[l], prefix_s, PAST_LEN,
                        final_norm_w, final, cfg_s)
        prompt_states.append(sp)
        sample_states.append(ss)
    outs_p = [jnp.stack(x) for x in zip(*prompt_states)]
    outs_s = [jnp.stack(x) for x in zip(*sample_states)]
    return (hp, hs, *outs_p, *outs_s)
```
